```python
import math
import jax, jax.numpy as jnp
from jax import lax
import numpy as np

D_MODEL = 1024
BATCH = 2
SEQ = 8192
DEPTH = 4

HEAD_DIM = 64
ROPE_THETA = 10000.0
EPS = 1e-6
SWA_WIDTH = D_MODEL // 2
SWA_Q_HEADS = SWA_WIDTH // HEAD_DIM
SWA_KV_HEADS = SWA_Q_HEADS // 4
SWA_GROUP = SWA_Q_HEADS // SWA_KV_HEADS
WINDOW = 128
BLOCK = 128
DIFF_V_DIM = 2 * HEAD_DIM
DIFF_WIDTH = D_MODEL - SWA_WIDTH
DIFF_HEADS = DIFF_WIDTH // DIFF_V_DIM
MIX_WIDTH = SWA_WIDTH + DIFF_WIDTH
COL_QA = SWA_Q_HEADS * HEAD_DIM
COL_KA = SWA_KV_HEADS * HEAD_DIM
COL_VA = SWA_KV_HEADS * HEAD_DIM
COL_QD = DIFF_HEADS * 2 * HEAD_DIM
COL_KD = DIFF_HEADS * 2 * HEAD_DIM
COL_VD = DIFF_HEADS * DIFF_V_DIM
IN_COLS = COL_QA + COL_KA + COL_VA + COL_QD + COL_KD + COL_VD
SPLITS = [COL_QA, COL_QA + COL_KA, COL_QA + COL_KA + COL_VA,
          COL_QA + COL_KA + COL_VA + COL_QD,
          COL_QA + COL_KA + COL_VA + COL_QD + COL_KD]
D_FF = 3584
N_EXPERTS = 8
TOP_K = 2
MOE_ROWS = 256
N_DENSE = (DEPTH + 1) // 2
N_MOE = DEPTH // 2
N_MOD = 6

kernel_name = "hybrid_swa_sink_diffattn_moe_adaln_encoder"


def rms_norm(x, g):
    xf = x.astype(jnp.float32)
    y = xf * lax.rsqrt(jnp.mean(xf * xf, axis=-1, keepdims=True) + EPS)
    return (y * g.astype(jnp.float32)).astype(x.dtype)


def rope_tables(positions):
    inv = 1.0 / (ROPE_THETA ** (jnp.arange(0, HEAD_DIM, 2, dtype=jnp.float32) / HEAD_DIM))
    ang = positions.astype(jnp.float32)[..., None] * inv
    return jnp.cos(ang)[:, :, None, :], jnp.sin(ang)[:, :, None, :]


def apply_rope(t, cos, sin):
    t1, t2 = jnp.split(t.astype(jnp.float32), 2, axis=-1)
    return jnp.concatenate([t1 * cos - t2 * sin, t2 * cos + t1 * sin], axis=-1).astype(t.dtype)


def window_attention(q, k, v, sink):
    B, S = q.shape[0], q.shape[1]
    nb = S // BLOCK
    qb = q.reshape(B, nb, BLOCK, SWA_KV_HEADS, SWA_GROUP, HEAD_DIM)

    def band(t):
        tb = t.reshape(B, nb, BLOCK, SWA_KV_HEADS, HEAD_DIM)
        tp = jnp.pad(tb, ((0, 0), (1, 1), (0, 0), (0, 0), (0, 0)))
        return jnp.concatenate([tp[:, :-2], tp[:, 1:-1], tp[:, 2:]], axis=2)

    kb, vb = band(k), band(v)
    q_pos = jnp.arange(S).reshape(nb, BLOCK)
    k_pos = (jnp.arange(nb)[:, None] - 1) * BLOCK + jnp.arange(3 * BLOCK)[None, :]
    kp = k_pos[:, None, :]
    valid = (jnp.abs(q_pos[:, :, None] - kp) <= WINDOW) & (kp >= 0) & (kp < S)
    s = jnp.einsum('bnqhgd,bnkhd->bnhgqk', qb, kb,
                   preferred_element_type=jnp.float32) * (HEAD_DIM ** -0.5)
    s = jnp.where(valid[None, :, None, None], s, -jnp.inf)
    sk = sink.astype(jnp.float32).reshape(1, 1, SWA_KV_HEADS, SWA_GROUP, 1, 1)
    m = jnp.maximum(s.max(axis=-1, keepdims=True), sk)
    p = jnp.exp(s - m)
    p = p / (p.sum(axis=-1, keepdims=True) + jnp.exp(sk - m))
    o = jnp.einsum('bnhgqk,bnkhd->bnqhgd', p.astype(v.dtype), vb)
    return o.reshape(B, S, SWA_WIDTH)


def diff_attention(q, k, v, lam, subln_g, lambda_init):
    B, S = q.shape[0], q.shape[1]
    nb = S // BLOCK
    qb = jnp.moveaxis(q.reshape(B, nb, BLOCK, DIFF_HEADS, 2, HEAD_DIM), 1, 0)

    def one_block(q_blk):
        s = jnp.einsum('bqhcd,bkhcd->bhcqk', q_blk, k,
                       preferred_element_type=jnp.float32) * (HEAD_DIM ** -0.5)
        p = jax.nn.softmax(s, axis=-1)
        a = p[:, :, 0] - lam * p[:, :, 1]
        return jnp.einsum('bhqk,bkhe->bqhe', a.astype(v.dtype), v)

    o = lax.map(one_block, qb)
    o = jnp.moveaxis(o, 0, 1).reshape(B, S, DIFF_HEADS, DIFF_V_DIM)
    o = rms_norm(o, subln_g) * (1.0 - lambda_init)
    return o.reshape(B, S, DIFF_WIDTH)


def token_mixer(h, cos, sin, w_in, w_out, sink, lq1, lk1, lq2, lk2, subln_g, lambda_init):
    B, S = h.shape[0], h.shape[1]
    proj = h @ w_in
    qa, ka, va, qd, kd, vd = jnp.split(proj, SPLITS, axis=-1)
    qa = apply_rope(qa.reshape(B, S, SWA_Q_HEADS, HEAD_DIM), cos, sin)
    ka = apply_rope(ka.reshape(B, S, SWA_KV_HEADS, HEAD_DIM), cos, sin)
    va = va.reshape(B, S, SWA_KV_HEADS, HEAD_DIM)
    qd = apply_rope(qd.reshape(B, S, 2 * DIFF_HEADS, HEAD_DIM), cos, sin).reshape(B, S, DIFF_HEADS, 2, HEAD_DIM)
    kd = apply_rope(kd.reshape(B, S, 2 * DIFF_HEADS, HEAD_DIM), cos, sin).reshape(B, S, DIFF_HEADS, 2, HEAD_DIM)
    vd = vd.reshape(B, S, DIFF_HEADS, DIFF_V_DIM)
    lam = (jnp.exp(jnp.sum(lq1.astype(jnp.float32) * lk1.astype(jnp.float32)))
           - jnp.exp(jnp.sum(lq2.astype(jnp.float32) * lk2.astype(jnp.float32))) + lambda_init)
    ya = window_attention(qa, ka, va, sink)
    yd = diff_attention(qd, kd, vd, lam, subln_g, lambda_init)
    return jnp.concatenate([ya, yd], axis=-1) @ w_out


def swiglu(h, w_gate, w_up, w_down):
    return (jax.nn.silu(h @ w_gate) * (h @ w_up)) @ w_down


def moe_swiglu(h, router_w, w_gate, w_up, w_down):
    B, S, D = h.shape
    xt = h.reshape(-1, D)
    N = xt.shape[0]
    logits = jnp.dot(xt, router_w, preferred_element_type=jnp.float32)
    top_val, top_idx = lax.top_k(logits, TOP_K)
    top_w = jax.nn.softmax(top_val, axis=-1)
    flat_e = top_idx.reshape(-1)
    flat_tok = jnp.repeat(jnp.arange(N, dtype=jnp.int32), TOP_K)
    flat_w = top_w.reshape(-1)
    order = jnp.argsort(flat_e)
    sorted_e = flat_e[order]
    counts = jnp.bincount(flat_e, length=N_EXPERTS)
    starts = jnp.cumsum(counts) - counts
    padded = (counts + MOE_ROWS - 1) // MOE_ROWS * MOE_ROWS
    pad_ends = jnp.cumsum(padded)
    pad_starts = pad_ends - padded
    dest = pad_starts[sorted_e] + jnp.arange(N * TOP_K) - starts[sorted_e]
    P = (N * TOP_K + MOE_ROWS - 1) // MOE_ROWS * MOE_ROWS + N_EXPERTS * MOE_ROWS
    n_chunks = P // MOE_ROWS
    buf_tok = jnp.full((P,), N, jnp.int32).at[dest].set(flat_tok[order])
    buf_w = jnp.zeros((P,), jnp.float32).at[dest].set(flat_w[order])
    chunk_e = jnp.minimum(jnp.searchsorted(pad_ends, jnp.arange(n_chunks) * MOE_ROWS, side='right'),
                          N_EXPERTS - 1)
    xs = xt[jnp.minimum(buf_tok, N - 1)].reshape(n_chunks, MOE_ROWS, D)

    def expert_block(args):
        xc, e = args
        return swiglu(xc, w_gate[e], w_up[e], w_down[e])

    ys = lax.map(expert_block, (xs, chunk_e)).reshape(P, D)
    out = jnp.zeros((N, D), ys.dtype).at[buf_tok].add(ys * buf_w[:, None].astype(ys.dtype), mode='drop')
    return out.reshape(B, S, D)


def setup_inputs(seed: int = 0) -> dict:
    key = jax.random.key(seed)
    ks = jax.random.split(key, 24)
    f32 = jnp.float32
    nrm = lambda k, shape, s: jax.random.normal(k, shape, f32) * s
    D, F, E = D_MODEL, D_FF, N_EXPERTS
    return {
        "x": nrm(ks[0], (BATCH, SEQ, D), 1.0),
        "c": nrm(ks[1], (BATCH, D), 1.0),
        "positions": jnp.broadcast_to(jnp.arange(SEQ, dtype=jnp.int32)[None, :], (BATCH, SEQ)),
        "ada_w": nrm(ks[2], (DEPTH, D, N_MOD * D), D ** -0.5),
        "ada_b": nrm(ks[3], (DEPTH, N_MOD * D), 0.02),
        "norm_mix_g": 1.0 + nrm(ks[4], (DEPTH, D), 0.02),
        "norm_ffn_g": 1.0 + nrm(ks[5], (DEPTH, D), 0.02),
        "w_in": nrm(ks[6], (DEPTH, D, IN_COLS), D ** -0.5),
        "w_out": nrm(ks[7], (DEPTH, MIX_WIDTH, D), MIX_WIDTH ** -0.5),
        "sink": nrm(ks[8], (DEPTH, SWA_Q_HEADS), 0.5),
        "lam_q1": nrm(ks[9], (DEPTH, HEAD_DIM), 0.1),
        "lam_k1": nrm(ks[10], (DEPTH, HEAD_DIM), 0.1),
        "lam_q2": nrm(ks[11], (DEPTH, HEAD_DIM), 0.1),
        "lam_k2": nrm(ks[12], (DEPTH, HEAD_DIM), 0.1),
        "subln_g": 1.0 + nrm(ks[13], (DEPTH, DIFF_V_DIM), 0.02),
        "ffn_w_gate": nrm(ks[14], (N_DENSE, D, F), D ** -0.5),
        "ffn_w_up": nrm(ks[15], (N_DENSE, D, F), D ** -0.5),
        "ffn_w_down": nrm(ks[16], (N_DENSE, F, D), F ** -0.5),
        "router_w": nrm(ks[17], (N_MOE, D, E), D ** -0.5),
        "moe_w_gate": nrm(ks[18], (N_MOE, E, D, F), D ** -0.5),
        "moe_w_up": nrm(ks[19], (N_MOE, E, D, F), D ** -0.5),
        "moe_w_down": nrm(ks[20], (N_MOE, E, F, D), F ** -0.5),
        "final_g": 1.0 + nrm(ks[21], (D,), 0.02),
    }


def reference(x, c, positions, ada_w, ada_b, norm_mix_g, norm_ffn_g, w_in, w_out, sink,
              lam_q1, lam_k1, lam_q2, lam_k2, subln_g, ffn_w_gate, ffn_w_up, ffn_w_down,
              router_w, moe_w_gate, moe_w_up, moe_w_down, final_g):
    cos, sin = rope_tables(positions)
    c_act = jax.nn.silu(c)
    for layer in range(DEPTH):
        mod = c_act @ ada_w[layer] + ada_b[layer]
        sh_m, sc_m, g_m, sh_f, sc_f, g_f = [t[:, None, :] for t in jnp.split(mod, N_MOD, axis=-1)]
        lambda_init = 0.8 - 0.6 * math.exp(-0.3 * layer)
        h = rms_norm(x, norm_mix_g[layer]) * (1.0 + sc_m) + sh_m
        x = x + g_m * token_mixer(h, cos, sin, w_in[layer], w_out[layer], sink[layer],
                                  lam_q1[layer], lam_k1[layer], lam_q2[layer], lam_k2[layer],
                                  subln_g[layer], lambda_init)
        h = rms_norm(x, norm_ffn_g[layer]) * (1.0 + sc_f) + sh_f
        if layer % 2 == 0:
            i = layer // 2
            y = swiglu(h, ffn_w_gate[i], ffn_w_up[i], ffn_w_down[i])
        else:
            i = layer // 2
            y = moe_swiglu(h, router_w[i], moe_w_gate[i], moe_w_up[i], moe_w_down[i])
        x = x + g_f * y
    return rms_norm(x, final_g)
```

```python
import functools
import math

import numpy as np
import jax
import jax.numpy as jnp
from jax import lax
from jax.experimental import pallas as pl
from jax.experimental.pallas import tpu as pltpu

F32 = jnp.float32
BF16 = jnp.bfloat16

HEAD_DIM = 64
HALF_DIM = HEAD_DIM // 2
ROPE_THETA = 10000.0
EPS = 1e-6
SWA_Q_HEADS = 8
SWA_KV_HEADS = 2
WINDOW = 128
DIFF_HEADS = 4
N_EXPERTS = 8
N_MOD = 6
LANES = 128
VMEM_LIMIT = 56 * 1024 * 1024

QA_BLK, KA_BLK, QD_BLK, KD_BLK, VA_BLK, VD_BLK = 0, 4, 6, 10, 14, 16
ROPE_BLKS = 14
PROJ_COLS = 20 * LANES
Q_SCALE = HEAD_DIM ** -0.5

TM_PROJ = 512
TQ_WIN = 256
TQ_DIFF = 512
TK_DIFF = 512
TM_FFN = 1024
TF_FFN = 512
TM_MOE = 512
ROWS_GATHER = 512
TM_COMB = 256


def _rope_pair_cols(base, head_a, head_b):
    a = base + head_a * HEAD_DIM
    b = base + head_b * HEAD_DIM
    return (list(range(a, a + HALF_DIM)) + list(range(b, b + HALF_DIM))
            + list(range(a + HALF_DIM, a + HEAD_DIM)) + list(range(b + HALF_DIM, b + HEAD_DIM)))


def _in_proj_columns():
    qa, ka, va, qd, kd, vd = 0, 512, 640, 768, 1280, 1792
    cols = []
    for j in range(4):
        cols += _rope_pair_cols(qa, 2 * j, 2 * j + 1)
    for g in range(SWA_KV_HEADS):
        cols += _rope_pair_cols(ka, g, g)
    for h in range(DIFF_HEADS):
        cols += _rope_pair_cols(qd, 2 * h, 2 * h + 1)
    for h in range(DIFF_HEADS):
        cols += _rope_pair_cols(kd, 2 * h, 2 * h + 1)
    for g in range(SWA_KV_HEADS):
        cols += 2 * list(range(va + g * HEAD_DIM, va + (g + 1) * HEAD_DIM))
    cols += list(range(vd, vd + 512))
    return np.asarray(cols, np.int32)


_IN_COLS = _in_proj_columns()


def _first_of_pair_mask():
    lane = lax.broadcasted_iota(jnp.int32, (1, LANES), 1)
    return (lane // HALF_DIM) % 2 == 0


def _silu(g):
    return g / (1.0 + jnp.exp(-g))


def _params(sem):
    return pltpu.CompilerParams(dimension_semantics=sem, vmem_limit_bytes=VMEM_LIMIT)


def _adaln_kernel(c_ref, w_ref, b_ref, o_ref):
    a = _silu(c_ref[...])
    o_ref[0] = jnp.dot(a, w_ref[0], preferred_element_type=F32,
                       precision=lax.Precision.HIGHEST) + b_ref[0]


def _adaln(c_pad, ada_w, ada_b):
    depth, d, cols = ada_w.shape
    tn = 1536
    return pl.pallas_call(
        _adaln_kernel,
        grid=(depth, cols // tn),
        in_specs=[pl.BlockSpec((8, d), lambda l, j: (0, 0)),
                  pl.BlockSpec((1, d, tn), lambda l, j: (l, 0, j)),
                  pl.BlockSpec((1, 1, tn), lambda l, j: (l, 0, j))],
        out_specs=pl.BlockSpec((1, 8, tn), lambda l, j: (l, 0, j)),
        out_shape=jax.ShapeDtypeStruct((depth, 8, cols), F32),
        compiler_params=_params(("arbitrary", "arbitrary")),
        name="adaln",
    )(c_pad, ada_w, ada_b.reshape(depth, 1, cols))


def _rope_table_kernel(pos_ref, cos_ref, sin_ref):
    lane = lax.broadcasted_iota(jnp.int32, (1, LANES), 1)
    freq = (lane % HALF_DIM).astype(F32)
    inv = 1.0 / (ROPE_THETA ** (2.0 * freq / HEAD_DIM))
    ang = pos_ref[...].astype(F32) * inv
    sign = jnp.where(lane < HEAD_DIM, -1.0, 1.0)
    cos_ref[...] = jnp.cos(ang)
    sin_ref[...] = jnp.sin(ang) * sign


def _rope_tables(positions):
    n = positions.size
    tm = 1024
    pos = positions.reshape(n, 1)
    return pl.pallas_call(
        _rope_table_kernel,
        grid=(n // tm,),
        in_specs=[pl.BlockSpec((tm, 1), lambda i: (i, 0))],
        out_specs=[pl.BlockSpec((tm, LANES), lambda i: (i, 0))] * 2,
        out_shape=[jax.ShapeDtypeStruct((n, LANES), F32)] * 2,
        compiler_params=_params(("arbitrary",)),
        name="rope_tables",
    )(pos)


def _rms_mod(x, g, scale, shift):
    y = x * lax.rsqrt(jnp.mean(x * x, axis=-1, keepdims=True) + EPS) * g
    return y * (1.0 + scale) + shift


def _inproj_kernel(x_ref, mod_ref, g_ref, w_ref, cos_ref, sin_ref, o_ref):
    h = _rms_mod(x_ref[...], g_ref[...], mod_ref[0, 1:2, :], mod_ref[0, 0:1, :])
    proj = jnp.dot(h.astype(BF16), w_ref[...], preferred_element_type=F32)
    cos = cos_ref[...]
    sin = sin_ref[...]
    for blk in range(ROPE_BLKS):
        t = proj[:, blk * LANES:(blk + 1) * LANES]
        r = t * cos + pltpu.roll(t, HEAD_DIM, axis=1) * sin
        if blk < KA_BLK or QD_BLK <= blk < KD_BLK:
            r = r * Q_SCALE
        o_ref[:, blk * LANES:(blk + 1) * LANES] = r.astype(BF16)
    o_ref[:, ROPE_BLKS * LANES:] = proj[:, ROPE_BLKS * LANES:].astype(BF16)


def _inproj(x2, mod, g, w, cos_t, sin_t, seq):
    n, d = x2.shape
    tm = TM_PROJ
    per_b = seq // tm
    return pl.pallas_call(
        _inproj_kernel,
        grid=(n // tm,),
        in_specs=[pl.BlockSpec((tm, d), lambda i: (i, 0)),
                  pl.BlockSpec((1, N_MOD, d), lambda i: (i // per_b, 0, 0)),
                  pl.BlockSpec((1, d), lambda i: (0, 0)),
                  pl.BlockSpec((d, PROJ_COLS), lambda i: (0, 0)),
                  pl.BlockSpec((tm, LANES), lambda i: (i, 0)),
                  pl.BlockSpec((tm, LANES), lambda i: (i, 0))],
        out_specs=pl.BlockSpec((tm, PROJ_COLS), lambda i: (i, 0)),
        out_shape=jax.ShapeDtypeStruct((n, PROJ_COLS), BF16),
        compiler_params=_params(("arbitrary",)),
        name="inproj",
    )(x2, mod, g, w, cos_t, sin_t)


def _win_kernel(sink_ref, q_ref, k_ref, v_ref, o_ref, *, tq, seq):
    j = pl.program_id(1)
    i = pl.program_id(2)
    width = tq + 2 * WINDOW
    start = pl.multiple_of(jnp.clip(i * tq - WINDOW, 0, seq - width), LANES)
    q = q_ref[...]
    kw = k_ref[pl.ds(start, width), :]
    vw = v_ref[pl.ds(start, width), :]
    first = _first_of_pair_mask()
    qpos = i * tq + lax.broadcasted_iota(jnp.int32, (tq, width), 0)
    kpos = start + lax.broadcasted_iota(jnp.int32, (tq, width), 1)
    valid = jnp.abs(qpos - kpos) <= WINDOW
    outs = []
    for half in range(2):
        qm = jnp.where(first if half == 0 else jnp.logical_not(first), q, jnp.zeros_like(q))
        s = lax.dot_general(qm, kw, (((1,), (1,)), ((), ())), preferred_element_type=F32)
        s = jnp.where(valid, s, -1e30)
        sk = sink_ref[2 * j + half]
        m = jnp.maximum(jnp.max(s, axis=-1, keepdims=True), sk)
        p = jnp.exp(s - m)
        den = jnp.sum(p, axis=-1, keepdims=True) + jnp.exp(sk - m)
        outs.append(jnp.dot(p.astype(BF16), vw, preferred_element_type=F32) / den)
    lane = lax.broadcasted_iota(jnp.int32, (1, LANES), 1)
    o_ref[...] = jnp.where(lane < HEAD_DIM, outs[0], outs[1]).astype(o_ref.dtype)


def _window_attention(proj, sink, batch, seq):
    n = proj.shape[0]
    tq = TQ_WIN
    nq = seq // tq
    kern = functools.partial(_win_kernel, tq=tq, seq=seq)
    return pl.pallas_call(
        kern,
        grid=(batch, SWA_Q_HEADS // 2, nq),
        in_specs=[pl.BlockSpec(memory_space=pltpu.SMEM),
                  pl.BlockSpec((tq, LANES), lambda b, j, i: (b * nq + i, QA_BLK + j)),
                  pl.BlockSpec((seq, LANES), lambda b, j, i: (b, KA_BLK + j // 2)),
                  pl.BlockSpec((seq, LANES), lambda b, j, i: (b, VA_BLK + j // 2))],
        out_specs=pl.BlockSpec((tq, LANES), lambda b, j, i: (b * nq + i, j)),
        out_shape=jax.ShapeDtypeStruct((n, 4 * LANES), BF16),
        compiler_params=_params(("arbitrary", "arbitrary", "arbitrary")),
        name="window_attn",
    )(sink, proj, proj, proj)


def _diff_kernel(lam_ref, g_ref, q_ref, k_ref, v_ref, o_ref, m_sc, l_sc, acc_sc, *, tk, seq, lambda_init):
    q = q_ref[...]
    first = _first_of_pair_mask()
    zero = jnp.zeros_like(q)
    qs = (jnp.where(first, q, zero), jnp.where(first, zero, q))
    m_sc[...] = jnp.full(m_sc.shape, -jnp.inf, F32)
    l_sc[...] = jnp.zeros(l_sc.shape, F32)
    acc_sc[...] = jnp.zeros(acc_sc.shape, F32)

    def body(t, carry):
        off = pl.multiple_of(t * tk, tk)
        kb = k_ref[pl.ds(off, tk), :]
        vb = v_ref[pl.ds(off, tk), :]
        for c in range(2):
            s = lax.dot_general(qs[c], kb, (((1,), (1,)), ((), ())), preferred_element_type=F32)
            m_prev = m_sc[c]
            m_new = jnp.maximum(m_prev, jnp.max(s, axis=-1, keepdims=True))
            alpha = jnp.exp(m_prev - m_new)
            p = jnp.exp(s - m_new)
            l_sc[c] = alpha * l_sc[c] + jnp.sum(p, axis=-1, keepdims=True)
            acc_sc[c] = alpha * acc_sc[c] + jnp.dot(p.astype(BF16), vb, preferred_element_type=F32)
            m_sc[c] = m_new
        return carry

    lax.fori_loop(0, seq // tk, body, 0)

    lam1 = jnp.sum(lam_ref[0:1, :] * lam_ref[1:2, :], axis=-1, keepdims=True)
    lam2 = jnp.sum(lam_ref[2:3, :] * lam_ref[3:4, :], axis=-1, keepdims=True)
    lam = jnp.exp(lam1) - jnp.exp(lam2) + lambda_init
    o = acc_sc[0] / l_sc[0] - lam * (acc_sc[1] / l_sc[1])
    y = o * lax.rsqrt(jnp.mean(o * o, axis=-1, keepdims=True) + EPS) * g_ref[...]
    o_ref[...] = (y * (1.0 - lambda_init)).astype(o_ref.dtype)


def _diff_attention(proj, lam_vecs, subln_g, batch, seq, lambda_init):
    n = proj.shape[0]
    tq, tk = TQ_DIFF, TK_DIFF
    nq = seq // tq
    kern = functools.partial(_diff_kernel, tk=tk, seq=seq, lambda_init=lambda_init)
    return pl.pallas_call(
        kern,
        grid=(batch, DIFF_HEADS, nq),
        in_specs=[pl.BlockSpec((4, HEAD_DIM), lambda b, h, i: (0, 0)),
                  pl.BlockSpec((1, LANES), lambda b, h, i: (0, 0)),
                  pl.BlockSpec((tq, LANES), lambda b, h, i: (b * nq + i, QD_BLK + h)),
                  pl.BlockSpec((seq, LANES), lambda b, h, i: (b, KD_BLK + h)),
                  pl.BlockSpec((seq, LANES), lambda b, h, i: (b, VD_BLK + h))],
        out_specs=pl.BlockSpec((tq, LANES), lambda b, h, i: (b * nq + i, h)),
        out_shape=jax.ShapeDtypeStruct((n, DIFF_HEADS * LANES), BF16),
        scratch_shapes=[pltpu.VMEM((2, tq, 1), F32), pltpu.VMEM((2, tq, 1), F32),
                        pltpu.VMEM((2, tq, LANES), F32)],
        compiler_params=_params(("arbitrary", "arbitrary", "arbitrary")),
        name="diff_attn",
    )(lam_vecs, subln_g, proj, proj, proj)


def _route_top2(logits):
    lane = lax.broadcasted_iota(jnp.int32, logits.shape, 1)
    lg = jnp.where(lane < N_EXPERTS, logits, -jnp.inf)
    v1 = jnp.max(lg, axis=-1, keepdims=True)
    i1 = jnp.min(jnp.where(lg == v1, lane, LANES), axis=-1, keepdims=True)
    lg2 = jnp.where(lane == i1, -jnp.inf, lg)
    v2 = jnp.max(lg2, axis=-1, keepdims=True)
    i2 = jnp.min(jnp.where(lg2 == v2, lane, LANES), axis=-1, keepdims=True)
    e = jnp.exp(v2 - v1)
    w1 = 1.0 / (1.0 + e)
    w2 = e / (1.0 + e)
    idx = jnp.where(lane == 0, i1, jnp.where(lane == 1, i2, 0))
    wgt = jnp.where(lane == 0, w1, jnp.where(lane == 1, w2, 0.0))
    return idx, wgt


def _outproj_kernel(ya_ref, yd_ref, w_ref, x_ref, mod_ref, g_ref, *rest, moe):
    if moe:
        rw_ref, x1_ref, h_ref, idx_ref, wgt_ref = rest
    else:
        x1_ref, h_ref = rest
    half = ya_ref.shape[1]
    y = (jnp.dot(ya_ref[...], w_ref[:half, :], preferred_element_type=F32)
         + jnp.dot(yd_ref[...], w_ref[half:, :], preferred_element_type=F32))
    x1 = x_ref[...] + mod_ref[0, 2:3, :] * y
    x1_ref[...] = x1
    h = _rms_mod(x1, g_ref[...], mod_ref[0, 4:5, :], mod_ref[0, 3:4, :])
    h_ref[...] = h.astype(h_ref.dtype)
    if moe:
        logits = jnp.dot(h, rw_ref[...], preferred_element_type=F32, precision=lax.Precision.HIGHEST)
        idx, wgt = _route_top2(logits)
        idx_ref[...] = idx
        wgt_ref[...] = wgt


def _outproj(ya, yd, w, x2, mod, g, seq, router_w=None):
    n, d = x2.shape
    tm = TM_PROJ
    per_b = seq // tm
    moe = router_w is not None
    row = lambda i: (i, 0)
    in_specs = [pl.BlockSpec((tm, ya.shape[1]), row),
                pl.BlockSpec((tm, yd.shape[1]), row),
                pl.BlockSpec(w.shape, lambda i: (0, 0)),
                pl.BlockSpec((tm, d), row),
                pl.BlockSpec((1, N_MOD, d), lambda i: (i // per_b, 0, 0)),
                pl.BlockSpec((1, d), lambda i: (0, 0))]
    out_specs = [pl.BlockSpec((tm, d), row), pl.BlockSpec((tm, d), row)]
    out_shape = [jax.ShapeDtypeStruct((n, d), F32), jax.ShapeDtypeStruct((n, d), BF16)]
    args = [ya, yd, w, x2, mod, g]
    if moe:
        in_specs.append(pl.BlockSpec((d, LANES), lambda i: (0, 0)))
        out_specs += [pl.BlockSpec((tm, LANES), row)] * 2
        out_shape += [jax.ShapeDtypeStruct((n, LANES), jnp.int32), jax.ShapeDtypeStruct((n, LANES), F32)]
        args.append(router_w)
    return pl.pallas_call(
        functools.partial(_outproj_kernel, moe=moe),
        grid=(n // tm,),
        in_specs=in_specs, out_specs=out_specs, out_shape=out_shape,
        compiler_params=_params(("arbitrary",)),
        name="outproj_moe" if moe else "outproj",
    )(*args)


def _swiglu_step(h, wg_ref, wu_ref, wd_ref):
    g = jnp.dot(h, wg_ref[...], preferred_element_type=F32)
    u = jnp.dot(h, wu_ref[...], preferred_element_type=F32)
    return jnp.dot((_silu(g) * u).astype(BF16), wd_ref[...], preferred_element_type=F32)


def _ffn_kernel(h_ref, wg_ref, wu_ref, wd_ref, x_ref, mod_ref, o_ref, acc_ref):
    j = pl.program_id(1)

    @pl.when(j == 0)
    def _():
        acc_ref[...] = jnp.zeros(acc_ref.shape, F32)

    acc_ref[...] += _swiglu_step(h_ref[...], wg_ref, wu_ref, wd_ref)

    @pl.when(j == pl.num_programs(1) - 1)
    def _():
        o_ref[...] = x_ref[...] + mod_ref[0, 5:6, :] * acc_ref[...]


def _dense_ffn(h, wg, wu, wd, x1, mod, seq):
    n, d = x1.shape
    f = wg.shape[1]
    tm, tf = TM_FFN, TF_FFN
    per_b = seq // tm
    return pl.pallas_call(
        _ffn_kernel,
        grid=(n // tm, f // tf),
        in_specs=[pl.BlockSpec((tm, d), lambda i, j: (i, 0)),
                  pl.BlockSpec((d, tf), lambda i, j: (0, j)),
                  pl.BlockSpec((d, tf), lambda i, j: (0, j)),
                  pl.BlockSpec((tf, d), lambda i, j: (j, 0)),
                  pl.BlockSpec((tm, d), lambda i, j: (i, 0)),
                  pl.BlockSpec((1, N_MOD, d), lambda i, j: (i // per_b, 0, 0))],
        out_specs=pl.BlockSpec((tm, d), lambda i, j: (i, 0)),
        out_shape=jax.ShapeDtypeStruct((n, d), F32),
        scratch_shapes=[pltpu.VMEM((tm, d), F32)],
        compiler_params=_params(("arbitrary", "arbitrary")),
        name="dense_ffn",
    )(h, wg, wu, wd, x1, mod)


def _moe_plan(idx, tm):
    n = idx.shape[0]
    flat_e = idx.reshape(-1)
    onehot = (flat_e[:, None] == jnp.arange(N_EXPERTS, dtype=jnp.int32)[None, :]).astype(jnp.int32)
    csum = jnp.cumsum(onehot, axis=0)
    rank = jnp.sum((csum - onehot) * onehot, axis=1)
    counts = csum[-1]
    padded = (counts + tm - 1) // tm * tm
    pad_ends = jnp.cumsum(padded)
    pad_starts = pad_ends - padded
    dest = (pad_starts[flat_e] + rank).astype(jnp.int32)
    rows = 2 * n + N_EXPERTS * tm
    flat_tok = jnp.arange(2 * n, dtype=jnp.int32) // 2
    buf_tok = jnp.zeros((rows,), jnp.int32).at[dest].set(flat_tok)
    n_tiles = (pad_ends[-1] // tm).astype(jnp.int32).reshape(1)
    tile_start = jnp.arange(rows // tm, dtype=jnp.int32) * tm
    tile_e = jnp.minimum(jnp.sum((pad_ends[None, :] <= tile_start[:, None]).astype(jnp.int32), axis=1),
                         N_EXPERTS - 1).astype(jnp.int32)
    return buf_tok, dest.reshape(n, 2), tile_e, n_tiles


def _gather_kernel(tok_ref, h_hbm, xs_hbm, sem, *, rows):
    base = pl.program_id(0) * rows

    def issue(r, carry):
        tok = tok_ref[base + r]
        pltpu.make_async_copy(h_hbm.at[pl.ds(tok, 1)], xs_hbm.at[pl.ds(base + r, 1)], sem).start()
        return carry

    lax.fori_loop(0, rows, issue, 0)
    pltpu.make_async_copy(h_hbm.at[pl.ds(0, rows)], xs_hbm.at[pl.ds(base, rows)], sem).wait()


def _moe_gather(h3, buf_tok):
    rows_total = buf_tok.shape[0]
    rows = ROWS_GATHER
    return pl.pallas_call(
        functools.partial(_gather_kernel, rows=rows),
        grid_spec=pltpu.PrefetchScalarGridSpec(
            num_scalar_prefetch=1,
            grid=(rows_total // rows,),
            in_specs=[pl.BlockSpec(memory_space=pl.ANY)],
            out_specs=pl.BlockSpec(memory_space=pl.ANY),
            scratch_shapes=[pltpu.SemaphoreType.DMA(())]),
        out_shape=jax.ShapeDtypeStruct((rows_total,) + h3.shape[1:], h3.dtype),
        compiler_params=_params(("arbitrary",)),
        name="moe_gather",
    )(buf_tok, h3)


def _moe_ffn_kernel(te_ref, nt_ref, xs_ref, wg_ref, wu_ref, wd_ref, o_ref, acc_ref):
    i = pl.program_id(0)
    j = pl.program_id(1)

    @pl.when(i < nt_ref[0])
    def _():
        @pl.when(j == 0)
        def _():
            acc_ref[...] = jnp.zeros(acc_ref.shape, F32)

        acc_ref[...] += _swiglu_step(xs_ref[...], wg_ref.at[0], wu_ref.at[0], wd_ref.at[0])

        @pl.when(j == pl.num_programs(1) - 1)
        def _():
            o_ref[...] = acc_ref[...].astype(o_ref.dtype)

    @pl.when(jnp.logical_and(i >= nt_ref[0], j == pl.num_programs(1) - 1))
    def _():
        o_ref[...] = jnp.zeros(o_ref.shape, o_ref.dtype)


def _moe_ffn(xs, wg, wu, wd, tile_e, n_tiles, tm):
    rows_total, d = xs.shape
    f = wg.shape[2]
    tf = TF_FFN
    nf = f // tf

    def row_map(i, j, te, nt):
        return (jnp.minimum(i, nt[0] - 1), 0)

    def _fj(i, j, nt):
        return jnp.where(i < nt[0], j, nf - 1)

    def up_map(i, j, te, nt):
        return (te[jnp.minimum(i, nt[0] - 1)], 0, _fj(i, j, nt))

    def down_map(i, j, te, nt):
        return (te[jnp.minimum(i, nt[0] - 1)], _fj(i, j, nt), 0)

    return pl.pallas_call(
        _moe_ffn_kernel,
        grid_spec=pltpu.PrefetchScalarGridSpec(
            num_scalar_prefetch=2,
            grid=(rows_total // tm, nf),
            in_specs=[pl.BlockSpec((tm, d), row_map),
                      pl.BlockSpec((1, d, tf), up_map),
                      pl.BlockSpec((1, d, tf), up_map),
                      pl.BlockSpec((1, tf, d), down_map)],
            out_specs=pl.BlockSpec((tm, d), lambda i, j, te, nt: (i, 0)),
            scratch_shapes=[pltpu.VMEM((tm, d), F32)]),
        out_shape=jax.ShapeDtypeStruct((rows_total, d), BF16),
        compiler_params=_params(("arbitrary", "arbitrary")),
        name="moe_ffn",
    )(tile_e, n_tiles, xs, wg, wu, wd)


def _combine_kernel(pos_ref, ys_hbm, wgt_ref, x_ref, mod_ref, fg_ref, o_ref, buf, sem, *, tm, final):
    i = pl.program_id(0)
    base = i * tm

    def issue(r, carry):
        for k in range(2):
            p = pos_ref[2 * (base + r) + k]
            pltpu.make_async_copy(ys_hbm.at[pl.ds(p, 1)], buf.at[k, pl.ds(r, 1)], sem).start()
        return carry

    lax.fori_loop(0, tm, issue, 0)
    for k in range(2):
        pltpu.make_async_copy(ys_hbm.at[pl.ds(0, tm)], buf.at[k], sem).wait()

    wgt = wgt_ref[...]
    g_f = mod_ref[0, 5:6, :]
    sub = x_ref.shape[1] // LANES
    cols = []
    for cblk in range(sub):
        y0 = buf[0, :, cblk, :].astype(F32)
        y1 = buf[1, :, cblk, :].astype(F32)
        cols.append(wgt[:, 0:1] * y0 + wgt[:, 1:2] * y1)
    y = jnp.concatenate(cols, axis=-1)
    out = x_ref[...] + g_f * y
    if final:
        out = out * lax.rsqrt(jnp.mean(out * out, axis=-1, keepdims=True) + EPS) * fg_ref[...]
    o_ref[...] = out


def _moe_combine(ys3, pos, wgt, x1, mod, final_g, seq, final):
    n, d = x1.shape
    tm = TM_COMB
    per_b = seq // tm
    sub = d // LANES
    return pl.pallas_call(
        functools.partial(_combine_kernel, tm=tm, final=final),
        grid_spec=pltpu.PrefetchScalarGridSpec(
            num_scalar_prefetch=1,
            grid=(n // tm,),
            in_specs=[pl.BlockSpec(memory_space=pl.ANY),
                      pl.BlockSpec((tm, LANES), lambda i, p: (i, 0)),
                      pl.BlockSpec((tm, d), lambda i, p: (i, 0)),
                      pl.BlockSpec((1, N_MOD, d), lambda i, p: (i // per_b, 0, 0)),
                      pl.BlockSpec((1, d), lambda i, p: (0, 0))],
            out_specs=pl.BlockSpec((tm, d), lambda i, p: (i, 0)),
            scratch_shapes=[pltpu.VMEM((2, tm, sub, LANES), ys3.dtype),
                            pltpu.SemaphoreType.DMA(())]),
        out_shape=jax.ShapeDtypeStruct((n, d), F32),
        compiler_params=_params(("arbitrary",)),
        name="moe_combine_final" if final else "moe_combine",
    )(pos.reshape(-1), ys3, wgt, x1, mod, final_g)


def _final_norm_kernel(x_ref, g_ref, o_ref):
    x = x_ref[...]
    o_ref[...] = x * lax.rsqrt(jnp.mean(x * x, axis=-1, keepdims=True) + EPS) * g_ref[...]


def _final_norm(x2, g):
    n, d = x2.shape
    tm = TM_PROJ
    return pl.pallas_call(
        _final_norm_kernel,
        grid=(n // tm,),
        in_specs=[pl.BlockSpec((tm, d), lambda i: (i, 0)), pl.BlockSpec((1, d), lambda i: (0, 0))],
        out_specs=pl.BlockSpec((tm, d), lambda i: (i, 0)),
        out_shape=jax.ShapeDtypeStruct((n, d), F32),
        compiler_params=_params(("arbitrary",)),
        name="final_norm",
    )(x2, g)


def kernel(x, c, positions, ada_w, ada_b, norm_mix_g, norm_ffn_g, w_in, w_out, sink, lam_q1, lam_k1, lam_q2, lam_k2, subln_g, ffn_w_gate, ffn_w_up, ffn_w_down, router_w, moe_w_gate, moe_w_up, moe_w_down, final_g):
    batch, seq, d = x.shape
    depth = w_in.shape[0]
    n = batch * seq
    sub = d // LANES

    c_pad = jnp.zeros((8, d), F32).at[:batch].set(c)
    mod_all = _adaln(c_pad, ada_w, ada_b)
    cos_t, sin_t = _rope_tables(positions)
    w_in_p = jnp.take(w_in, _IN_COLS, axis=2).astype(BF16)
    w_out_b = w_out.astype(BF16)
    final_g2 = final_g.reshape(1, d)

    x2 = x.reshape(n, d)
    for layer in range(depth):
        lambda_init = 0.8 - 0.6 * math.exp(-0.3 * layer)
        mod = mod_all[layer, :batch].reshape(batch, N_MOD, d)
        proj = _inproj(x2, mod, norm_mix_g[layer].reshape(1, d), w_in_p[layer], cos_t, sin_t, seq)
        ya = _window_attention(proj, sink[layer], batch, seq)
        lam_vecs = jnp.stack([lam_q1[layer], lam_k1[layer], lam_q2[layer], lam_k2[layer]])
        yd = _diff_attention(proj, lam_vecs, subln_g[layer].reshape(1, LANES), batch, seq, lambda_init)
        i = layer // 2
        g_ffn = norm_ffn_g[layer].reshape(1, d)
        if layer % 2 == 0:
            x1, h = _outproj(ya, yd, w_out_b[layer], x2, mod, g_ffn, seq)
            x2 = _dense_ffn(h, ffn_w_gate[i].astype(BF16), ffn_w_up[i].astype(BF16),
                            ffn_w_down[i].astype(BF16), x1, mod, seq)
        else:
            rw = jnp.zeros((d, LANES), F32).at[:, :N_EXPERTS].set(router_w[i])
            x1, h, idx, wgt = _outproj(ya, yd, w_out_b[layer], x2, mod, g_ffn, seq, router_w=rw)
            buf_tok, pos, tile_e, n_tiles = _moe_plan(idx[:, :2], TM_MOE)
            xs3 = _moe_gather(h.reshape(n, sub, LANES), buf_tok)
            ys = _moe_ffn(xs3.reshape(-1, d), moe_w_gate[i].astype(BF16), moe_w_up[i].astype(BF16),
                          moe_w_down[i].astype(BF16), tile_e, n_tiles, TM_MOE)
            final = layer == depth - 1
            x2 = _moe_combine(ys.reshape(-1, sub, LANES), pos, wgt, x1, mod, final_g2, seq, final)
    if depth % 2 == 1:
        x2 = _final_norm(x2, final_g2)
    return x2.reshape(batch, seq, d)
```

```python
import functools
import math

import numpy as np
import jax
import jax.numpy as jnp
from jax import lax
from jax.experimental import pallas as pl
from jax.experimental.pallas import tpu as pltpu

F32 = jnp.float32
BF16 = jnp.bfloat16

HEAD_DIM = 64
HALF_DIM = HEAD_DIM // 2
ROPE_THETA = 10000.0
EPS = 1e-6
SWA_Q_HEADS = 8
SWA_KV_HEADS = 2
WINDOW = 128
DIFF_HEADS = 4
N_EXPERTS = 8
N_MOD = 6
LANES = 128
VMEM_LIMIT = 56 * 1024 * 1024

QA_BLK, KA_BLK, QD_BLK, KD_BLK, VA_BLK = 0, 4, 6, 10, 14
ROPE_BLKS = 14
PROJ_COLS = 16 * LANES
VD_COL0 = 1792
Q_SCALE = HEAD_DIM ** -0.5
LOG2E = math.log2(math.e)

TM_PROJ = 512
TQ_WIN = 256
TQ_DIFF = 256
TK_DIFF = 512
TM_FFN = 1024
TF_FFN = 512
TM_MOE = 512
ROWS_GATHER = 512
TM_COMB = 256


def _rope_pair_cols(base, head_a, head_b):
    a = base + head_a * HEAD_DIM
    b = base + head_b * HEAD_DIM
    return (list(range(a, a + HALF_DIM)) + list(range(b, b + HALF_DIM))
            + list(range(a + HALF_DIM, a + HEAD_DIM)) + list(range(b + HALF_DIM, b + HEAD_DIM)))


def _in_proj_columns():
    qa, ka, va, qd, kd, vd = 0, 512, 640, 768, 1280, 1792
    cols = []
    for j in range(4):
        cols += _rope_pair_cols(qa, 2 * j, 2 * j + 1)
    for g in range(SWA_KV_HEADS):
        cols += _rope_pair_cols(ka, g, g)
    for h in range(DIFF_HEADS):
        cols += _rope_pair_cols(qd, 2 * h, 2 * h + 1)
    for h in range(DIFF_HEADS):
        cols += _rope_pair_cols(kd, 2 * h, 2 * h + 1)
    for g in range(SWA_KV_HEADS):
        cols += 2 * list(range(va + g * HEAD_DIM, va + (g + 1) * HEAD_DIM))
    assert vd == VD_COL0
    return np.asarray(cols, np.int32)


_IN_COLS = _in_proj_columns()


def _first_of_pair_mask():
    lane = lax.broadcasted_iota(jnp.int32, (1, LANES), 1)
    return (lane // HALF_DIM) % 2 == 0


def _silu(g):
    return g / (1.0 + jnp.exp(-g))


def _params(sem):
    return pltpu.CompilerParams(dimension_semantics=sem, vmem_limit_bytes=VMEM_LIMIT)


def _adaln_kernel(c_ref, w_ref, b_ref, o_ref):
    a = _silu(c_ref[...])
    o_ref[0] = jnp.dot(a, w_ref[0], preferred_element_type=F32,
                       precision=lax.Precision.HIGHEST) + b_ref[0]


def _adaln(c_pad, ada_w, ada_b):
    depth, d, cols = ada_w.shape
    tn = 1536
    return pl.pallas_call(
        _adaln_kernel,
        grid=(depth, cols // tn),
        in_specs=[pl.BlockSpec((8, d), lambda l, j: (0, 0)),
                  pl.BlockSpec((1, d, tn), lambda l, j: (l, 0, j)),
                  pl.BlockSpec((1, 1, tn), lambda l, j: (l, 0, j))],
        out_specs=pl.BlockSpec((1, 8, tn), lambda l, j: (l, 0, j)),
        out_shape=jax.ShapeDtypeStruct((depth, 8, cols), F32),
        compiler_params=_params(("arbitrary", "arbitrary")),
        name="adaln",
    )(c_pad, ada_w, ada_b.reshape(depth, 1, cols))


def _rope_table_kernel(pos_ref, cos_ref, sin_ref):
    lane = lax.broadcasted_iota(jnp.int32, (1, LANES), 1)
    freq = (lane % HALF_DIM).astype(F32)
    inv = 1.0 / (ROPE_THETA ** (2.0 * freq / HEAD_DIM))
    ang = pos_ref[...].astype(F32) * inv
    sign = jnp.where(lane < HEAD_DIM, -1.0, 1.0)
    cos_ref[...] = jnp.cos(ang)
    sin_ref[...] = jnp.sin(ang) * sign


def _rope_tables(positions):
    n = positions.size
    tm = 1024
    pos = positions.reshape(n, 1)
    return pl.pallas_call(
        _rope_table_kernel,
        grid=(n // tm,),
        in_specs=[pl.BlockSpec((tm, 1), lambda i: (i, 0))],
        out_specs=[pl.BlockSpec((tm, LANES), lambda i: (i, 0))] * 2,
        out_shape=[jax.ShapeDtypeStruct((n, LANES), F32)] * 2,
        compiler_params=_params(("arbitrary",)),
        name="rope_tables",
    )(pos)


def _rms_mod(x, g, scale, shift):
    y = x * lax.rsqrt(jnp.mean(x * x, axis=-1, keepdims=True) + EPS) * g
    return y * (1.0 + scale) + shift


def _inproj_kernel(x_ref, mod_ref, g_ref, w_ref, wvt_ref, cos_ref, sin_ref, o_ref, vt_ref):
    h = _rms_mod(x_ref[...], g_ref[...], mod_ref[0, 1:2, :], mod_ref[0, 0:1, :]).astype(BF16)
    proj = jnp.dot(h, w_ref[...], preferred_element_type=F32)
    vt_ref[...] = lax.dot_general(wvt_ref[...], h, (((1,), (1,)), ((), ())),
                                  preferred_element_type=F32).astype(BF16)
    cos = cos_ref[...]
    sin = sin_ref[...]
    for blk in range(ROPE_BLKS):
        t = proj[:, blk * LANES:(blk + 1) * LANES]
        r = t * cos + pltpu.roll(t, HEAD_DIM, axis=1) * sin
        if blk < KA_BLK:
            r = r * Q_SCALE
        elif QD_BLK <= blk < KD_BLK:
            r = r * (Q_SCALE * LOG2E)
        o_ref[:, blk * LANES:(blk + 1) * LANES] = r.astype(BF16)
    o_ref[:, ROPE_BLKS * LANES:] = proj[:, ROPE_BLKS * LANES:].astype(BF16)


def _inproj(x2, mod, g, w, wvt, cos_t, sin_t, seq):
    n, d = x2.shape
    tm = TM_PROJ
    per_b = seq // tm
    vt_rows = wvt.shape[0]
    return pl.pallas_call(
        _inproj_kernel,
        grid=(n // tm,),
        in_specs=[pl.BlockSpec((tm, d), lambda i: (i, 0)),
                  pl.BlockSpec((1, N_MOD, d), lambda i: (i // per_b, 0, 0)),
                  pl.BlockSpec((1, d), lambda i: (0, 0)),
                  pl.BlockSpec((d, PROJ_COLS), lambda i: (0, 0)),
                  pl.BlockSpec((vt_rows, d), lambda i: (0, 0)),
                  pl.BlockSpec((tm, LANES), lambda i: (i, 0)),
                  pl.BlockSpec((tm, LANES), lambda i: (i, 0))],
        out_specs=[pl.BlockSpec((tm, PROJ_COLS), lambda i: (i, 0)),
                   pl.BlockSpec((vt_rows, tm), lambda i: (0, i))],
        out_shape=[jax.ShapeDtypeStruct((n, PROJ_COLS), BF16),
                   jax.ShapeDtypeStruct((vt_rows, n), BF16)],
        compiler_params=_params(("arbitrary",)),
        name="inproj",
    )(x2, mod, g, w, wvt, cos_t, sin_t)


def _win_kernel(sink_ref, q_ref, k_ref, v_ref, o_ref, *, tq, seq):
    j = pl.program_id(1)
    i = pl.program_id(2)
    width = tq + 2 * WINDOW
    start = pl.multiple_of(jnp.clip(i * tq - WINDOW, 0, seq - width), LANES)
    q = q_ref[...]
    kw = k_ref[pl.ds(start, width), :]
    vw = v_ref[pl.ds(start, width), :]
    first = _first_of_pair_mask()
    qpos = i * tq + lax.broadcasted_iota(jnp.int32, (tq, width), 0)
    kpos = start + lax.broadcasted_iota(jnp.int32, (tq, width), 1)
    valid = jnp.abs(qpos - kpos) <= WINDOW
    outs = []
    for half in range(2):
        qm = jnp.where(first if half == 0 else jnp.logical_not(first), q, jnp.zeros_like(q))
        s = lax.dot_general(qm, kw, (((1,), (1,)), ((), ())), preferred_element_type=F32)
        s = jnp.where(valid, s, -1e30)
        sk = sink_ref[2 * j + half]
        m = jnp.maximum(jnp.max(s, axis=-1, keepdims=True), sk)
        p = jnp.exp(s - m)
        den = jnp.sum(p, axis=-1, keepdims=True) + jnp.exp(sk - m)
        outs.append(jnp.dot(p.astype(BF16), vw, preferred_element_type=F32) / den)
    lane = lax.broadcasted_iota(jnp.int32, (1, LANES), 1)
    o_ref[...] = jnp.where(lane < HEAD_DIM, outs[0], outs[1]).astype(o_ref.dtype)


def _window_attention(proj, sink, batch, seq):
    n = proj.shape[0]
    tq = TQ_WIN
    nq = seq // tq
    kern = functools.partial(_win_kernel, tq=tq, seq=seq)
    return pl.pallas_call(
        kern,
        grid=(batch, SWA_Q_HEADS // 2, nq),
        in_specs=[pl.BlockSpec(memory_space=pltpu.SMEM),
                  pl.BlockSpec((tq, LANES), lambda b, j, i: (b * nq + i, QA_BLK + j)),
                  pl.BlockSpec((seq, LANES), lambda b, j, i: (b, KA_BLK + j // 2)),
                  pl.BlockSpec((seq, LANES), lambda b, j, i: (b, VA_BLK + j // 2))],
        out_specs=pl.BlockSpec((tq, LANES), lambda b, j, i: (b * nq + i, j)),
        out_shape=jax.ShapeDtypeStruct((n, 4 * LANES), BF16),
        compiler_params=_params(("arbitrary", "arbitrary", "arbitrary")),
        name="window_attn",
    )(sink, proj, proj, proj)


def _diff_kernel(lam_ref, g_ref, q_ref, k_ref, vt_ref, o_ref, qt_sc, s_a, s_b, cm_a, cm_b, m_sc, l_sc, acc_sc,
                 *, tk, seq, lambda_init):
    tq = q_ref.shape[0]
    nk = seq // tk
    q = q_ref[...].astype(F32)
    first = _first_of_pair_mask()
    zero = jnp.zeros_like(q)
    qt_sc[:, :tq] = jnp.where(first, q, zero).T.astype(BF16)
    qt_sc[:, tq:] = jnp.where(first, zero, q).T.astype(BF16)
    m_sc[...] = jnp.full(m_sc.shape, -1e30, F32)
    l_sc[...] = jnp.zeros(l_sc.shape, F32)
    acc_sc[...] = jnp.zeros(acc_sc.shape, F32)

    def scores(t, s_ref, cm_ref):
        off = pl.multiple_of(t * tk, tk)
        s = jnp.dot(k_ref[pl.ds(off, tk), :], qt_sc[...], preferred_element_type=F32)
        s_ref[...] = s
        cm_ref[...] = jnp.max(s, axis=0, keepdims=True)

    def consume(t, s_ref, cm_ref):
        off = pl.multiple_of(t * tk, tk)
        m_prev = m_sc[...]
        m_new = jnp.maximum(m_prev, cm_ref[...])
        alpha = jnp.exp2(m_prev - m_new)
        p = jnp.exp2(s_ref[...] - m_new)
        l_sc[...] = alpha * l_sc[...] + jnp.sum(p, axis=0, keepdims=True)
        vb = vt_ref[:, pl.ds(off, tk)]
        acc_sc[...] = alpha * acc_sc[...] + jnp.dot(vb, p.astype(BF16), preferred_element_type=F32)
        m_sc[...] = m_new

    scores(0, s_a, cm_a)

    def body(u, carry):
        t = 2 * u
        scores(t + 1, s_b, cm_b)
        consume(t, s_a, cm_a)
        scores(t + 2, s_a, cm_a)
        consume(t + 1, s_b, cm_b)
        return carry

    lax.fori_loop(0, nk // 2 - 1, body, 0)
    scores(nk - 1, s_b, cm_b)
    consume(nk - 2, s_a, cm_a)
    consume(nk - 1, s_b, cm_b)

    lam1 = jnp.sum(lam_ref[0:1, :] * lam_ref[1:2, :], axis=-1, keepdims=True)
    lam2 = jnp.sum(lam_ref[2:3, :] * lam_ref[3:4, :], axis=-1, keepdims=True)
    lam = jnp.exp(lam1) - jnp.exp(lam2) + lambda_init
    o = acc_sc[...] / l_sc[...]
    o = o[:, :tq] - lam * o[:, tq:]
    y = o * lax.rsqrt(jnp.mean(o * o, axis=0, keepdims=True) + EPS)
    y = y.T * (g_ref[...] * (1.0 - lambda_init))
    o_ref[...] = y.astype(o_ref.dtype)


def _diff_attention(proj, vt, lam_vecs, subln_g, batch, seq, lambda_init):
    n = proj.shape[0]
    tq, tk = TQ_DIFF, TK_DIFF
    nq = seq // tq
    kern = functools.partial(_diff_kernel, tk=tk, seq=seq, lambda_init=lambda_init)
    return pl.pallas_call(
        kern,
        grid=(batch, DIFF_HEADS, nq),
        in_specs=[pl.BlockSpec((4, HEAD_DIM), lambda b, h, i: (0, 0)),
                  pl.BlockSpec((1, LANES), lambda b, h, i: (0, 0)),
                  pl.BlockSpec((tq, LANES), lambda b, h, i: (b * nq + i, QD_BLK + h)),
                  pl.BlockSpec((seq, LANES), lambda b, h, i: (b, KD_BLK + h)),
                  pl.BlockSpec((LANES, seq), lambda b, h, i: (h, b))],
        out_specs=pl.BlockSpec((tq, LANES), lambda b, h, i: (b * nq + i, h)),
        out_shape=jax.ShapeDtypeStruct((n, DIFF_HEADS * LANES), BF16),
        scratch_shapes=[pltpu.VMEM((LANES, 2 * tq), BF16),
                        pltpu.VMEM((tk, 2 * tq), F32), pltpu.VMEM((tk, 2 * tq), F32),
                        pltpu.VMEM((1, 2 * tq), F32), pltpu.VMEM((1, 2 * tq), F32),
                        pltpu.VMEM((1, 2 * tq), F32), pltpu.VMEM((1, 2 * tq), F32),
                        pltpu.VMEM((LANES, 2 * tq), F32)],
        compiler_params=_params(("arbitrary", "arbitrary", "arbitrary")),
        name="diff_attn",
    )(lam_vecs, subln_g, proj, proj, vt)


def _route_top2(logits):
    lane = lax.broadcasted_iota(jnp.int32, logits.shape, 1)
    lg = jnp.where(lane < N_EXPERTS, logits, -jnp.inf)
    v1 = jnp.max(lg, axis=-1, keepdims=True)
    i1 = jnp.min(jnp.where(lg == v1, lane, LANES), axis=-1, keepdims=True)
    lg2 = jnp.where(lane == i1, -jnp.inf, lg)
    v2 = jnp.max(lg2, axis=-1, keepdims=True)
    i2 = jnp.min(jnp.where(lg2 == v2, lane, LANES), axis=-1, keepdims=True)
    e = jnp.exp(v2 - v1)
    w1 = 1.0 / (1.0 + e)
    w2 = e / (1.0 + e)
    idx = jnp.where(lane == 0, i1, jnp.where(lane == 1, i2, 0))
    wgt = jnp.where(lane == 0, w1, jnp.where(lane == 1, w2, 0.0))
    return idx, wgt


def _outproj_kernel(ya_ref, yd_ref, w_ref, x_ref, mod_ref, g_ref, *rest, moe):
    if moe:
        rw_ref, x1_ref, h_ref, idx_ref, wgt_ref = rest
    else:
        x1_ref, h_ref = rest
    half = ya_ref.shape[1]
    y = (jnp.dot(ya_ref[...], w_ref[:half, :], preferred_element_type=F32)
         + jnp.dot(yd_ref[...], w_ref[half:, :], preferred_element_type=F32))
    x1 = x_ref[...] + mod_ref[0, 2:3, :] * y
    x1_ref[...] = x1
    h = _rms_mod(x1, g_ref[...], mod_ref[0, 4:5, :], mod_ref[0, 3:4, :])
    h_ref[...] = h.astype(h_ref.dtype)
    if moe:
        logits = jnp.dot(h, rw_ref[...], preferred_element_type=F32, precision=lax.Precision.HIGHEST)
        idx, wgt = _route_top2(logits)
        idx_ref[...] = idx
        wgt_ref[...] = wgt


def _outproj(ya, yd, w, x2, mod, g, seq, router_w=None):
    n, d = x2.shape
    tm = TM_PROJ
    per_b = seq // tm
    moe = router_w is not None
    row = lambda i: (i, 0)
    in_specs = [pl.BlockSpec((tm, ya.shape[1]), row),
                pl.BlockSpec((tm, yd.shape[1]), row),
                pl.BlockSpec(w.shape, lambda i: (0, 0)),
                pl.BlockSpec((tm, d), row),
                pl.BlockSpec((1, N_MOD, d), lambda i: (i // per_b, 0, 0)),
                pl.BlockSpec((1, d), lambda i: (0, 0))]
    out_specs = [pl.BlockSpec((tm, d), row), pl.BlockSpec((tm, d), row)]
    out_shape = [jax.ShapeDtypeStruct((n, d), F32), jax.ShapeDtypeStruct((n, d), BF16)]
    args = [ya, yd, w, x2, mod, g]
    if moe:
        in_specs.append(pl.BlockSpec((d, LANES), lambda i: (0, 0)))
        out_specs += [pl.BlockSpec((tm, LANES), row)] * 2
        out_shape += [jax.ShapeDtypeStruct((n, LANES), jnp.int32), jax.ShapeDtypeStruct((n, LANES), F32)]
        args.append(router_w)
    return pl.pallas_call(
        functools.partial(_outproj_kernel, moe=moe),
        grid=(n // tm,),
        in_specs=in_specs, out_specs=out_specs, out_shape=out_shape,
        compiler_params=_params(("arbitrary",)),
        name="outproj_moe" if moe else "outproj",
    )(*args)


def _swiglu_step(h, wg_ref, wu_ref, wd_ref):
    g = jnp.dot(h, wg_ref[...], preferred_element_type=F32)
    u = jnp.dot(h, wu_ref[...], preferred_element_type=F32)
    return jnp.dot((_silu(g) * u).astype(BF16), wd_ref[...], preferred_element_type=F32)


def _ffn_kernel(h_ref, wg_ref, wu_ref, wd_ref, x_ref, mod_ref, o_ref, acc_ref):
    j = pl.program_id(1)

    @pl.when(j == 0)
    def _():
        acc_ref[...] = jnp.zeros(acc_ref.shape, F32)

    acc_ref[...] += _swiglu_step(h_ref[...], wg_ref, wu_ref, wd_ref)

    @pl.when(j == pl.num_programs(1) - 1)
    def _():
        o_ref[...] = x_ref[...] + mod_ref[0, 5:6, :] * acc_ref[...]


def _dense_ffn(h, wg, wu, wd, x1, mod, seq):
    n, d = x1.shape
    f = wg.shape[1]
    tm, tf = TM_FFN, TF_FFN
    per_b = seq // tm
    return pl.pallas_call(
        _ffn_kernel,
        grid=(n // tm, f // tf),
        in_specs=[pl.BlockSpec((tm, d), lambda i, j: (i, 0)),
                  pl.BlockSpec((d, tf), lambda i, j: (0, j)),
                  pl.BlockSpec((d, tf), lambda i, j: (0, j)),
                  pl.BlockSpec((tf, d), lambda i, j: (j, 0)),
                  pl.BlockSpec((tm, d), lambda i, j: (i, 0)),
                  pl.BlockSpec((1, N_MOD, d), lambda i, j: (i // per_b, 0, 0))],
        out_specs=pl.BlockSpec((tm, d), lambda i, j: (i, 0)),
        out_shape=jax.ShapeDtypeStruct((n, d), F32),
        scratch_shapes=[pltpu.VMEM((tm, d), F32)],
        compiler_params=_params(("arbitrary", "arbitrary")),
        name="dense_ffn",
    )(h, wg, wu, wd, x1, mod)


def _moe_plan(idx, tm):
    n = idx.shape[0]
    flat_e = idx.reshape(-1)
    onehot = (flat_e[:, None] == jnp.arange(N_EXPERTS, dtype=jnp.int32)[None, :]).astype(jnp.int32)
    csum = jnp.cumsum(onehot, axis=0)
    rank = jnp.sum((csum - onehot) * onehot, axis=1)
    counts = csum[-1]
    padded = (counts + tm - 1) // tm * tm
    pad_ends = jnp.cumsum(padded)
    pad_starts = pad_ends - padded
    dest = (pad_starts[flat_e] + rank).astype(jnp.int32)
    rows = 2 * n + N_EXPERTS * tm
    flat_tok = jnp.arange(2 * n, dtype=jnp.int32) // 2
    buf_tok = jnp.zeros((rows,), jnp.int32).at[dest].set(flat_tok)
    n_tiles = (pad_ends[-1] // tm).astype(jnp.int32).reshape(1)
    tile_start = jnp.arange(rows // tm, dtype=jnp.int32) * tm
    tile_e = jnp.minimum(jnp.sum((pad_ends[None, :] <= tile_start[:, None]).astype(jnp.int32), axis=1),
                         N_EXPERTS - 1).astype(jnp.int32)
    return buf_tok, dest.reshape(n, 2), tile_e, n_tiles


def _gather_kernel(tok_ref, h_hbm, xs_ref, sem, *, rows):
    base = pl.program_id(0) * rows

    def issue(r, carry):
        tok = tok_ref[base + r]
        pltpu.make_async_copy(h_hbm.at[pl.ds(tok, 1)], xs_ref.at[pl.ds(r, 1)], sem).start()
        return carry

    lax.fori_loop(0, rows, issue, 0, unroll=8)
    pltpu.make_async_copy(h_hbm.at[pl.ds(0, rows)], xs_ref, sem).wait()


def _moe_gather(h3, buf_tok):
    rows_total = buf_tok.shape[0]
    rows = ROWS_GATHER
    return pl.pallas_call(
        functools.partial(_gather_kernel, rows=rows),
        grid_spec=pltpu.PrefetchScalarGridSpec(
            num_scalar_prefetch=1,
            grid=(rows_total // rows,),
            in_specs=[pl.BlockSpec(memory_space=pl.ANY)],
            out_specs=pl.BlockSpec((rows,) + h3.shape[1:], lambda i, tok: (i, 0, 0)),
            scratch_shapes=[pltpu.SemaphoreType.DMA(())]),
        out_shape=jax.ShapeDtypeStruct((rows_total,) + h3.shape[1:], h3.dtype),
        compiler_params=_params(("arbitrary",)),
        name="moe_gather",
    )(buf_tok, h3)


def _moe_ffn_kernel(te_ref, nt_ref, xs_ref, wg_ref, wu_ref, wd_ref, o_ref, acc_ref):
    i = pl.program_id(0)
    j = pl.program_id(1)

    @pl.when(i < nt_ref[0])
    def _():
        @pl.when(j == 0)
        def _():
            acc_ref[...] = jnp.zeros(acc_ref.shape, F32)

        acc_ref[...] += _swiglu_step(xs_ref[...], wg_ref.at[0], wu_ref.at[0], wd_ref.at[0])

        @pl.when(j == pl.num_programs(1) - 1)
        def _():
            o_ref[...] = acc_ref[...].astype(o_ref.dtype)

    @pl.when(jnp.logical_and(i >= nt_ref[0], j == pl.num_programs(1) - 1))
    def _():
        o_ref[...] = jnp.zeros(o_ref.shape, o_ref.dtype)


def _moe_ffn(xs, wg, wu, wd, tile_e, n_tiles, tm):
    rows_total, d = xs.shape
    f = wg.shape[2]
    tf = TF_FFN
    nf = f // tf

    def row_map(i, j, te, nt):
        return (jnp.minimum(i, nt[0] - 1), 0)

    def _fj(i, j, nt):
        return jnp.where(i < nt[0], j, nf - 1)

    def up_map(i, j, te, nt):
        return (te[jnp.minimum(i, nt[0] - 1)], 0, _fj(i, j, nt))

    def down_map(i, j, te, nt):
        return (te[jnp.minimum(i, nt[0] - 1)], _fj(i, j, nt), 0)

    return pl.pallas_call(
        _moe_ffn_kernel,
        grid_spec=pltpu.PrefetchScalarGridSpec(
            num_scalar_prefetch=2,
            grid=(rows_total // tm, nf),
            in_specs=[pl.BlockSpec((tm, d), row_map),
                      pl.BlockSpec((1, d, tf), up_map),
                      pl.BlockSpec((1, d, tf), up_map),
                      pl.BlockSpec((1, tf, d), down_map)],
            out_specs=pl.BlockSpec((tm, d), lambda i, j, te, nt: (i, 0)),
            scratch_shapes=[pltpu.VMEM((tm, d), F32)]),
        out_shape=jax.ShapeDtypeStruct((rows_total, d), BF16),
        compiler_params=_params(("arbitrary", "arbitrary")),
        name="moe_ffn",
    )(tile_e, n_tiles, xs, wg, wu, wd)


def _combine_kernel(pos_ref, ys_hbm, wgt_ref, x_ref, mod_ref, fg_ref, o_ref, buf, sem, *, tm, final):
    i = pl.program_id(0)
    base = i * tm

    def issue(r, carry):
        for k in range(2):
            p = pos_ref[2 * (base + r) + k]
            pltpu.make_async_copy(ys_hbm.at[pl.ds(p, 1)], buf.at[k, pl.ds(r, 1)], sem).start()
        return carry

    lax.fori_loop(0, tm, issue, 0)
    for k in range(2):
        pltpu.make_async_copy(ys_hbm.at[pl.ds(0, tm)], buf.at[k], sem).wait()

    wgt = wgt_ref[...]
    g_f = mod_ref[0, 5:6, :]
    sub = x_ref.shape[1] // LANES
    cols = []
    for cblk in range(sub):
        y0 = buf[0, :, cblk, :].astype(F32)
        y1 = buf[1, :, cblk, :].astype(F32)
        cols.append(wgt[:, 0:1] * y0 + wgt[:, 1:2] * y1)
    y = jnp.concatenate(cols, axis=-1)
    out = x_ref[...] + g_f * y
    if final:
        out = out * lax.rsqrt(jnp.mean(out * out, axis=-1, keepdims=True) + EPS) * fg_ref[...]
    o_ref[...] = out


def _moe_combine(ys3, pos, wgt, x1, mod, final_g, seq, final):
    n, d = x1.shape
    tm = TM_COMB
    per_b = seq // tm
    sub = d // LANES
    return pl.pallas_call(
        functools.partial(_combine_kernel, tm=tm, final=final),
        grid_spec=pltpu.PrefetchScalarGridSpec(
            num_scalar_prefetch=1,
            grid=(n // tm,),
            in_specs=[pl.BlockSpec(memory_space=pl.ANY),
                      pl.BlockSpec((tm, LANES), lambda i, p: (i, 0)),
                      pl.BlockSpec((tm, d), lambda i, p: (i, 0)),
                      pl.BlockSpec((1, N_MOD, d), lambda i, p: (i // per_b, 0, 0)),
                      pl.BlockSpec((1, d), lambda i, p: (0, 0))],
            out_specs=pl.BlockSpec((tm, d), lambda i, p: (i, 0)),
            scratch_shapes=[pltpu.VMEM((2, tm, sub, LANES), ys3.dtype),
                            pltpu.SemaphoreType.DMA(())]),
        out_shape=jax.ShapeDtypeStruct((n, d), F32),
        compiler_params=_params(("arbitrary",)),
        name="moe_combine_final" if final else "moe_combine",
    )(pos.reshape(-1), ys3, wgt, x1, mod, final_g)


def _final_norm_kernel(x_ref, g_ref, o_ref):
    x = x_ref[...]
    o_ref[...] = x * lax.rsqrt(jnp.mean(x * x, axis=-1, keepdims=True) + EPS) * g_ref[...]


def _final_norm(x2, g):
    n, d = x2.shape
    tm = TM_PROJ
    return pl.pallas_call(
        _final_norm_kernel,
        grid=(n // tm,),
        in_specs=[pl.BlockSpec((tm, d), lambda i: (i, 0)), pl.BlockSpec((1, d), lambda i: (0, 0))],
        out_specs=pl.BlockSpec((tm, d), lambda i: (i, 0)),
        out_shape=jax.ShapeDtypeStruct((n, d), F32),
        compiler_params=_params(("arbitrary",)),
        name="final_norm",
    )(x2, g)


def kernel(x, c, positions, ada_w, ada_b, norm_mix_g, norm_ffn_g, w_in, w_out, sink, lam_q1, lam_k1, lam_q2, lam_k2, subln_g, ffn_w_gate, ffn_w_up, ffn_w_down, router_w, moe_w_gate, moe_w_up, moe_w_down, final_g):
    batch, seq, d = x.shape
    depth = w_in.shape[0]
    n = batch * seq
    sub = d // LANES

    c_pad = jnp.zeros((8, d), F32).at[:batch].set(c)
    mod_all = _adaln(c_pad, ada_w, ada_b)
    cos_t, sin_t = _rope_tables(positions)
    w_in_p = jnp.take(w_in, _IN_COLS, axis=2).astype(BF16)
    w_vdt = jnp.swapaxes(w_in[:, :, VD_COL0:], 1, 2).astype(BF16)
    w_out_b = w_out.astype(BF16)
    final_g2 = final_g.reshape(1, d)

    x2 = x.reshape(n, d)
    for layer in range(depth):
        lambda_init = 0.8 - 0.6 * math.exp(-0.3 * layer)
        mod = mod_all[layer, :batch].reshape(batch, N_MOD, d)
        proj, vt = _inproj(x2, mod, norm_mix_g[layer].reshape(1, d), w_in_p[layer], w_vdt[layer],
                           cos_t, sin_t, seq)
        ya = _window_attention(proj, sink[layer], batch, seq)
        lam_vecs = jnp.stack([lam_q1[layer], lam_k1[layer], lam_q2[layer], lam_k2[layer]])
        yd = _diff_attention(proj, vt, lam_vecs, subln_g[layer].reshape(1, LANES), batch, seq, lambda_init)
        i = layer // 2
        g_ffn = norm_ffn_g[layer].reshape(1, d)
        if layer % 2 == 0:
            x1, h = _outproj(ya, yd, w_out_b[layer], x2, mod, g_ffn, seq)
            x2 = _dense_ffn(h, ffn_w_gate[i].astype(BF16), ffn_w_up[i].astype(BF16),
                            ffn_w_down[i].astype(BF16), x1, mod, seq)
        else:
            rw = jnp.zeros((d, LANES), F32).at[:, :N_EXPERTS].set(router_w[i])
            x1, h, idx, wgt = _outproj(ya, yd, w_out_b[layer], x2, mod, g_ffn, seq, router_w=rw)
            buf_tok, pos, tile_e, n_tiles = _moe_plan(idx[:, :2], TM_MOE)
            xs3 = _moe_gather(h.reshape(n, sub, LANES), buf_tok)
            ys = _moe_ffn(xs3.reshape(-1, d), moe_w_gate[i].astype(BF16), moe_w_up[i].astype(BF16),
                          moe_w_down[i].astype(BF16), tile_e, n_tiles, TM_MOE)
            final = layer == depth - 1
            x2 = _moe_combine(ys.reshape(-1, sub, LANES), pos, wgt, x1, mod, final_g2, seq, final)
    if depth % 2 == 1:
        x2 = _final_norm(x2, final_g2)
    return x2.reshape(batch, seq, d)
```

```python
import functools
import math

import jax
import jax.numpy as jnp
from jax import lax
from jax.experimental import pallas as pl
from jax.experimental.pallas import tpu as pltpu

F32 = jnp.float32
BF16 = jnp.bfloat16

HEAD_DIM = 64
HALF_DIM = HEAD_DIM // 2
ROPE_THETA = 10000.0
EPS = 1e-6
SWA_Q_HEADS = 8
SWA_KV_HEADS = 2
WINDOW = 128
DIFF_HEADS = 4
N_EXPERTS = 8
N_MOD = 6
LANES = 128
VMEM_LIMIT = 56 * 1024 * 1024

QA_BLK, KA_BLK, QD_BLK, KD_BLK, VA_BLK = 0, 4, 6, 10, 14
ROPE_BLKS = 14
PROJ_COLS = 16 * LANES
VD_COL0 = 1792
Q_SCALE = HEAD_DIM ** -0.5
LOG2E = math.log2(math.e)

TM_PROJ = 512
TQ_WIN = 256
TQ_DIFF = 512
TK_DIFF = 1024
TM_FFN = 1024
TF_FFN = 512
TM_MOE = 512
TM_COMB = 256


def _permute_in_proj(w_in):
    depth, d, _ = w_in.shape
    qa, ka, va, qd, kd = 0, 512, 640, 768, 1280

    def pairs(lo, n_pairs):
        t = w_in[:, :, lo:lo + n_pairs * LANES].reshape(depth, d, n_pairs, 2, 2, HALF_DIM)
        return t.transpose(0, 1, 2, 4, 3, 5).reshape(depth, d, n_pairs * LANES)

    def dup_halves(lo):
        t = w_in[:, :, lo:lo + SWA_KV_HEADS * HEAD_DIM].reshape(depth, d, SWA_KV_HEADS, 2, 1, HALF_DIM)
        return jnp.broadcast_to(t, (depth, d, SWA_KV_HEADS, 2, 2, HALF_DIM)).reshape(depth, d, SWA_KV_HEADS * LANES)

    def dup_heads(lo):
        t = w_in[:, :, lo:lo + SWA_KV_HEADS * HEAD_DIM].reshape(depth, d, SWA_KV_HEADS, 1, HEAD_DIM)
        return jnp.broadcast_to(t, (depth, d, SWA_KV_HEADS, 2, HEAD_DIM)).reshape(depth, d, SWA_KV_HEADS * LANES)

    return jnp.concatenate([pairs(qa, 4), dup_halves(ka), pairs(qd, DIFF_HEADS), pairs(kd, DIFF_HEADS),
                            dup_heads(va)], axis=-1)


def _first_of_pair_mask():
    lane = lax.broadcasted_iota(jnp.int32, (1, LANES), 1)
    return (lane // HALF_DIM) % 2 == 0


def _silu(g):
    return g / (1.0 + jnp.exp(-g))


def _params(sem):
    return pltpu.CompilerParams(dimension_semantics=sem, vmem_limit_bytes=VMEM_LIMIT)


def _adaln_kernel(c_ref, w_ref, b_ref, o_ref):
    a = _silu(c_ref[...])
    o_ref[0] = jnp.dot(a, w_ref[0], preferred_element_type=F32,
                       precision=lax.Precision.HIGHEST) + b_ref[0]


def _adaln(c_pad, ada_w, ada_b):
    depth, d, cols = ada_w.shape
    tn = 1536
    return pl.pallas_call(
        _adaln_kernel,
        grid=(depth, cols // tn),
        in_specs=[pl.BlockSpec((8, d), lambda l, j: (0, 0)),
                  pl.BlockSpec((1, d, tn), lambda l, j: (l, 0, j)),
                  pl.BlockSpec((1, 1, tn), lambda l, j: (l, 0, j))],
        out_specs=pl.BlockSpec((1, 8, tn), lambda l, j: (l, 0, j)),
        out_shape=jax.ShapeDtypeStruct((depth, 8, cols), F32),
        compiler_params=_params(("arbitrary", "arbitrary")),
        name="adaln",
    )(c_pad, ada_w, ada_b.reshape(depth, 1, cols))


def _rope_table_kernel(pos_ref, cos_ref, sin_ref):
    lane = lax.broadcasted_iota(jnp.int32, (1, LANES), 1)
    freq = (lane % HALF_DIM).astype(F32)
    inv = 1.0 / (ROPE_THETA ** (2.0 * freq / HEAD_DIM))
    ang = pos_ref[...].astype(F32) * inv
    sign = jnp.where(lane < HEAD_DIM, -1.0, 1.0)
    cos_ref[...] = jnp.cos(ang)
    sin_ref[...] = jnp.sin(ang) * sign


def _rope_tables(positions):
    n = positions.size
    tm = 1024
    pos = positions.reshape(n, 1)
    return pl.pallas_call(
        _rope_table_kernel,
        grid=(n // tm,),
        in_specs=[pl.BlockSpec((tm, 1), lambda i: (i, 0))],
        out_specs=[pl.BlockSpec((tm, LANES), lambda i: (i, 0))] * 2,
        out_shape=[jax.ShapeDtypeStruct((n, LANES), F32)] * 2,
        compiler_params=_params(("arbitrary",)),
        name="rope_tables",
    )(pos)


def _rms_mod(x, g, scale, shift):
    y = x * lax.rsqrt(jnp.mean(x * x, axis=-1, keepdims=True) + EPS) * g
    return y * (1.0 + scale) + shift


def _inproj_kernel(x_ref, mod_ref, g_ref, w_ref, wvt_ref, cos_ref, sin_ref, o_ref, vt_ref):
    h = _rms_mod(x_ref[...], g_ref[...], mod_ref[0, 1:2, :], mod_ref[0, 0:1, :]).astype(BF16)
    proj = jnp.dot(h, w_ref[...], preferred_element_type=F32)
    vt_ref[...] = lax.dot_general(wvt_ref[...], h, (((1,), (1,)), ((), ())),
                                  preferred_element_type=F32).astype(BF16)
    cos = cos_ref[...]
    sin = sin_ref[...]
    for blk in range(ROPE_BLKS):
        t = proj[:, blk * LANES:(blk + 1) * LANES]
        r = t * cos + pltpu.roll(t, HEAD_DIM, axis=1) * sin
        if blk < KA_BLK:
            r = r * Q_SCALE
        elif QD_BLK <= blk < KD_BLK:
            r = r * (Q_SCALE * LOG2E)
        o_ref[:, blk * LANES:(blk + 1) * LANES] = r.astype(BF16)
    o_ref[:, ROPE_BLKS * LANES:] = proj[:, ROPE_BLKS * LANES:].astype(BF16)


def _inproj(x2, mod, g, w, wvt, cos_t, sin_t, seq):
    n, d = x2.shape
    tm = TM_PROJ
    per_b = seq // tm
    vt_rows = wvt.shape[0]
    return pl.pallas_call(
        _inproj_kernel,
        grid=(n // tm,),
        in_specs=[pl.BlockSpec((tm, d), lambda i: (i, 0)),
                  pl.BlockSpec((1, N_MOD, d), lambda i: (i // per_b, 0, 0)),
                  pl.BlockSpec((1, d), lambda i: (0, 0)),
                  pl.BlockSpec((d, PROJ_COLS), lambda i: (0, 0)),
                  pl.BlockSpec((vt_rows, d), lambda i: (0, 0)),
                  pl.BlockSpec((tm, LANES), lambda i: (i, 0)),
                  pl.BlockSpec((tm, LANES), lambda i: (i, 0))],
        out_specs=[pl.BlockSpec((tm, PROJ_COLS), lambda i: (i, 0)),
                   pl.BlockSpec((vt_rows, tm), lambda i: (0, i))],
        out_shape=[jax.ShapeDtypeStruct((n, PROJ_COLS), BF16),
                   jax.ShapeDtypeStruct((vt_rows, n), BF16)],
        compiler_params=_params(("arbitrary",)),
        name="inproj",
    )(x2, mod, g, w, wvt, cos_t, sin_t)


def _win_kernel(sink_ref, q_ref, k_ref, v_ref, o_ref, *, tq, seq):
    j = pl.program_id(1)
    i = pl.program_id(2)
    width = tq + 2 * WINDOW
    start = pl.multiple_of(jnp.clip(i * tq - WINDOW, 0, seq - width), LANES)
    q = q_ref[...]
    kw = k_ref[pl.ds(start, width), :]
    vw = v_ref[pl.ds(start, width), :]
    first = _first_of_pair_mask()
    qpos = i * tq + lax.broadcasted_iota(jnp.int32, (tq, width), 0)
    kpos = start + lax.broadcasted_iota(jnp.int32, (tq, width), 1)
    valid = jnp.abs(qpos - kpos) <= WINDOW
    outs = []
    for half in range(2):
        qm = jnp.where(first if half == 0 else jnp.logical_not(first), q, jnp.zeros_like(q))
        s = lax.dot_general(qm, kw, (((1,), (1,)), ((), ())), preferred_element_type=F32)
        s = jnp.where(valid, s, -1e30)
        sk = sink_ref[2 * j + half]
        m = jnp.maximum(jnp.max(s, axis=-1, keepdims=True), sk)
        p = jnp.exp(s - m)
        den = jnp.sum(p, axis=-1, keepdims=True) + jnp.exp(sk - m)
        outs.append(jnp.dot(p.astype(BF16), vw, preferred_element_type=F32) / den)
    lane = lax.broadcasted_iota(jnp.int32, (1, LANES), 1)
    o_ref[...] = jnp.where(lane < HEAD_DIM, outs[0], outs[1]).astype(o_ref.dtype)


def _window_attention(proj, sink, batch, seq):
    n = proj.shape[0]
    tq = TQ_WIN
    nq = seq // tq
    kern = functools.partial(_win_kernel, tq=tq, seq=seq)
    return pl.pallas_call(
        kern,
        grid=(batch, SWA_Q_HEADS // 2, nq),
        in_specs=[pl.BlockSpec(memory_space=pltpu.SMEM),
                  pl.BlockSpec((tq, LANES), lambda b, j, i: (b * nq + i, QA_BLK + j)),
                  pl.BlockSpec((seq, LANES), lambda b, j, i: (b, KA_BLK + j // 2)),
                  pl.BlockSpec((seq, LANES), lambda b, j, i: (b, VA_BLK + j // 2))],
        out_specs=pl.BlockSpec((tq, LANES), lambda b, j, i: (b * nq + i, j)),
        out_shape=jax.ShapeDtypeStruct((n, 4 * LANES), BF16),
        compiler_params=_params(("arbitrary", "arbitrary", "arbitrary")),
        name="window_attn",
    )(sink, proj, proj, proj)


def _diff_kernel(lam_ref, g_ref, q_ref, k_ref, vt_ref, o_ref, qt_sc, s_a, s_b, cm_a, cm_b, m_sc, l_sc, acc_sc,
                 *, tk, seq, lambda_init):
    tq = q_ref.shape[0]
    nk = seq // tk
    q = q_ref[...].astype(F32)
    first = _first_of_pair_mask()
    zero = jnp.zeros_like(q)
    qt_sc[:, :tq] = jnp.where(first, q, zero).T.astype(BF16)
    qt_sc[:, tq:] = jnp.where(first, zero, q).T.astype(BF16)
    m_sc[...] = jnp.full(m_sc.shape, -1e30, F32)
    l_sc[...] = jnp.zeros(l_sc.shape, F32)
    acc_sc[...] = jnp.zeros(acc_sc.shape, F32)

    def scores(t, s_ref, cm_ref):
        off = pl.multiple_of(t * tk, tk)
        s = jnp.dot(k_ref[pl.ds(off, tk), :], qt_sc[...], preferred_element_type=F32)
        s_ref[...] = s
        cm_ref[...] = jnp.max(s, axis=0, keepdims=True)

    def consume(t, s_ref, cm_ref):
        off = pl.multiple_of(t * tk, tk)
        m_prev = m_sc[...]
        m_new = jnp.maximum(m_prev, cm_ref[...])
        alpha = jnp.exp2(m_prev - m_new)
        p = jnp.exp2(s_ref[...] - m_new)
        l_sc[...] = alpha * l_sc[...] + jnp.sum(p, axis=0, keepdims=True)
        vb = vt_ref[:, pl.ds(off, tk)]
        acc_sc[...] = alpha * acc_sc[...] + jnp.dot(vb, p.astype(BF16), preferred_element_type=F32)
        m_sc[...] = m_new

    scores(0, s_a, cm_a)

    def body(u, carry):
        t = 2 * u
        scores(t + 1, s_b, cm_b)
        consume(t, s_a, cm_a)
        scores(t + 2, s_a, cm_a)
        consume(t + 1, s_b, cm_b)
        return carry

    lax.fori_loop(0, nk // 2 - 1, body, 0)
    scores(nk - 1, s_b, cm_b)
    consume(nk - 2, s_a, cm_a)
    consume(nk - 1, s_b, cm_b)

    lam1 = jnp.sum(lam_ref[0:1, :] * lam_ref[1:2, :], axis=-1, keepdims=True)
    lam2 = jnp.sum(lam_ref[2:3, :] * lam_ref[3:4, :], axis=-1, keepdims=True)
    lam = jnp.exp(lam1) - jnp.exp(lam2) + lambda_init
    o = acc_sc[...] / l_sc[...]
    o = o[:, :tq] - lam * o[:, tq:]
    y = o * lax.rsqrt(jnp.mean(o * o, axis=0, keepdims=True) + EPS)
    y = y.T * (g_ref[...] * (1.0 - lambda_init))
    o_ref[...] = y.astype(o_ref.dtype)


def _diff_attention(proj, vt, lam_vecs, subln_g, batch, seq, lambda_init):
    n = proj.shape[0]
    tq, tk = TQ_DIFF, TK_DIFF
    nq = seq // tq
    kern = functools.partial(_diff_kernel, tk=tk, seq=seq, lambda_init=lambda_init)
    return pl.pallas_call(
        kern,
        grid=(batch, DIFF_HEADS, nq),
        in_specs=[pl.BlockSpec((4, HEAD_DIM), lambda b, h, i: (0, 0)),
                  pl.BlockSpec((1, LANES), lambda b, h, i: (0, 0)),
                  pl.BlockSpec((tq, LANES), lambda b, h, i: (b * nq + i, QD_BLK + h)),
                  pl.BlockSpec((seq, LANES), lambda b, h, i: (b, KD_BLK + h)),
                  pl.BlockSpec((LANES, seq), lambda b, h, i: (h, b))],
        out_specs=pl.BlockSpec((tq, LANES), lambda b, h, i: (b * nq + i, h)),
        out_shape=jax.ShapeDtypeStruct((n, DIFF_HEADS * LANES), BF16),
        scratch_shapes=[pltpu.VMEM((LANES, 2 * tq), BF16),
                        pltpu.VMEM((tk, 2 * tq), F32), pltpu.VMEM((tk, 2 * tq), F32),
                        pltpu.VMEM((1, 2 * tq), F32), pltpu.VMEM((1, 2 * tq), F32),
                        pltpu.VMEM((1, 2 * tq), F32), pltpu.VMEM((1, 2 * tq), F32),
                        pltpu.VMEM((LANES, 2 * tq), F32)],
        compiler_params=_params(("arbitrary", "arbitrary", "arbitrary")),
        name="diff_attn",
    )(lam_vecs, subln_g, proj, proj, vt)


def _route_top2(logits):
    lane = lax.broadcasted_iota(jnp.int32, logits.shape, 1)
    lg = jnp.where(lane < N_EXPERTS, logits, -jnp.inf)
    v1 = jnp.max(lg, axis=-1, keepdims=True)
    i1 = jnp.min(jnp.where(lg == v1, lane, LANES), axis=-1, keepdims=True)
    lg2 = jnp.where(lane == i1, -jnp.inf, lg)
    v2 = jnp.max(lg2, axis=-1, keepdims=True)
    i2 = jnp.min(jnp.where(lg2 == v2, lane, LANES), axis=-1, keepdims=True)
    e = jnp.exp(v2 - v1)
    w1 = 1.0 / (1.0 + e)
    w2 = e / (1.0 + e)
    idx = jnp.where(lane == 0, i1, jnp.where(lane == 1, i2, 0))
    wgt = jnp.where(lane == 0, w1, jnp.where(lane == 1, w2, 0.0))
    return idx, wgt


def _outproj_kernel(ya_ref, yd_ref, w_ref, x_ref, mod_ref, g_ref, *rest, moe):
    if moe:
        rw_ref, x1_ref, h_ref, idx_ref, wgt_ref = rest
    else:
        x1_ref, h_ref = rest
    half = ya_ref.shape[1]
    y = (jnp.dot(ya_ref[...], w_ref[:half, :], preferred_element_type=F32)
         + jnp.dot(yd_ref[...], w_ref[half:, :], preferred_element_type=F32))
    x1 = x_ref[...] + mod_ref[0, 2:3, :] * y
    x1_ref[...] = x1
    h = _rms_mod(x1, g_ref[...], mod_ref[0, 4:5, :], mod_ref[0, 3:4, :])
    h_ref[...] = h.astype(h_ref.dtype)
    if moe:
        logits = jnp.dot(h, rw_ref[...], preferred_element_type=F32, precision=lax.Precision.HIGHEST)
        idx, wgt = _route_top2(logits)
        idx_ref[...] = idx
        wgt_ref[...] = wgt


def _outproj(ya, yd, w, x2, mod, g, seq, router_w=None):
    n, d = x2.shape
    tm = TM_PROJ
    per_b = seq // tm
    moe = router_w is not None
    row = lambda i: (i, 0)
    in_specs = [pl.BlockSpec((tm, ya.shape[1]), row),
                pl.BlockSpec((tm, yd.shape[1]), row),
                pl.BlockSpec(w.shape, lambda i: (0, 0)),
                pl.BlockSpec((tm, d), row),
                pl.BlockSpec((1, N_MOD, d), lambda i: (i // per_b, 0, 0)),
                pl.BlockSpec((1, d), lambda i: (0, 0))]
    out_specs = [pl.BlockSpec((tm, d), row), pl.BlockSpec((tm, d), row)]
    out_shape = [jax.ShapeDtypeStruct((n, d), F32), jax.ShapeDtypeStruct((n, d), F32 if moe else BF16)]
    args = [ya, yd, w, x2, mod, g]
    if moe:
        in_specs.append(pl.BlockSpec((d, LANES), lambda i: (0, 0)))
        out_specs += [pl.BlockSpec((tm, LANES), row)] * 2
        out_shape += [jax.ShapeDtypeStruct((n, LANES), jnp.int32), jax.ShapeDtypeStruct((n, LANES), F32)]
        args.append(router_w)
    return pl.pallas_call(
        functools.partial(_outproj_kernel, moe=moe),
        grid=(n // tm,),
        in_specs=in_specs, out_specs=out_specs, out_shape=out_shape,
        compiler_params=_params(("arbitrary",)),
        name="outproj_moe" if moe else "outproj",
    )(*args)


def _swiglu_step(h, wg_ref, wu_ref, wd_ref):
    g = jnp.dot(h, wg_ref[...], preferred_element_type=F32)
    u = jnp.dot(h, wu_ref[...], preferred_element_type=F32)
    return jnp.dot((_silu(g) * u).astype(BF16), wd_ref[...], preferred_element_type=F32)


def _ffn_kernel(h_ref, wg_ref, wu_ref, wd_ref, x_ref, mod_ref, o_ref, acc_ref):
    j = pl.program_id(1)

    @pl.when(j == 0)
    def _():
        acc_ref[...] = jnp.zeros(acc_ref.shape, F32)

    acc_ref[...] += _swiglu_step(h_ref[...], wg_ref, wu_ref, wd_ref)

    @pl.when(j == pl.num_programs(1) - 1)
    def _():
        o_ref[...] = x_ref[...] + mod_ref[0, 5:6, :] * acc_ref[...]


def _dense_ffn(h, wg, wu, wd, x1, mod, seq):
    n, d = x1.shape
    f = wg.shape[1]
    tm, tf = TM_FFN, TF_FFN
    per_b = seq // tm
    return pl.pallas_call(
        _ffn_kernel,
        grid=(n // tm, f // tf),
        in_specs=[pl.BlockSpec((tm, d), lambda i, j: (i, 0)),
                  pl.BlockSpec((d, tf), lambda i, j: (0, j)),
                  pl.BlockSpec((d, tf), lambda i, j: (0, j)),
                  pl.BlockSpec((tf, d), lambda i, j: (j, 0)),
                  pl.BlockSpec((tm, d), lambda i, j: (i, 0)),
                  pl.BlockSpec((1, N_MOD, d), lambda i, j: (i // per_b, 0, 0))],
        out_specs=pl.BlockSpec((tm, d), lambda i, j: (i, 0)),
        out_shape=jax.ShapeDtypeStruct((n, d), F32),
        scratch_shapes=[pltpu.VMEM((tm, d), F32)],
        compiler_params=_params(("arbitrary", "arbitrary")),
        name="dense_ffn",
    )(h, wg, wu, wd, x1, mod)


def _moe_plan(idx, tm):
    n = idx.shape[0]
    flat_e = idx.reshape(-1)
    onehot = (flat_e[:, None] == jnp.arange(N_EXPERTS, dtype=jnp.int32)[None, :]).astype(jnp.int32)
    csum = jnp.cumsum(onehot, axis=0)
    rank = jnp.sum((csum - onehot) * onehot, axis=1)
    counts = csum[-1]
    padded = (counts + tm - 1) // tm * tm
    pad_ends = jnp.cumsum(padded)
    pad_starts = pad_ends - padded
    dest = (pad_starts[flat_e] + rank).astype(jnp.int32)
    rows = 2 * n + N_EXPERTS * tm
    flat_tok = jnp.arange(2 * n, dtype=jnp.int32) // 2
    buf_tok = jnp.zeros((rows,), jnp.int32).at[dest].set(flat_tok)
    n_tiles = (pad_ends[-1] // tm).astype(jnp.int32).reshape(1)
    tile_start = jnp.arange(rows // tm, dtype=jnp.int32) * tm
    tile_e = jnp.minimum(jnp.sum((pad_ends[None, :] <= tile_start[:, None]).astype(jnp.int32), axis=1),
                         N_EXPERTS - 1).astype(jnp.int32)
    return buf_tok, dest.reshape(n, 2), tile_e, n_tiles


def _row_gather(idx_of, src_hbm, dst_ref, sem, rows):
    def issue(r, carry):
        pltpu.make_async_copy(src_hbm.at[pl.ds(idx_of(r), 1)], dst_ref.at[pl.ds(r, 1)], sem).start()
        return carry

    lax.fori_loop(0, rows, issue, 0, unroll=8)


def _row_gather_wait(src_hbm, dst_ref, sem, rows):
    pltpu.make_async_copy(src_hbm.at[pl.ds(0, rows)], dst_ref, sem).wait()


def _moe_ffn_kernel(te_ref, nt_ref, tok_ref, h_hbm, wg_ref, wu_ref, wd_ref, o_ref, xbuf, xb16, acc_ref, sem):
    i = pl.program_id(0)
    j = pl.program_id(1)
    tm = xb16.shape[0]
    n_tiles = nt_ref[0]
    slot = i % 2

    def gather(tile, to_slot):
        _row_gather(lambda r: tok_ref[tile * tm + r], h_hbm, xbuf.at[to_slot], sem.at[to_slot], tm)

    @pl.when(jnp.logical_and(i == 0, j == 0))
    def _():
        gather(0, 0)

    @pl.when(jnp.logical_and(j == 1, i + 1 < n_tiles))
    def _():
        gather(i + 1, 1 - slot)

    @pl.when(i < n_tiles)
    def _():
        @pl.when(j == 0)
        def _():
            _row_gather_wait(h_hbm, xbuf.at[slot], sem.at[slot], tm)
            xb16[...] = xbuf[slot].astype(BF16)
            acc_ref[...] = jnp.zeros(acc_ref.shape, F32)

        acc_ref[...] += _swiglu_step(xb16[...], wg_ref.at[0], wu_ref.at[0], wd_ref.at[0])

        @pl.when(j == pl.num_programs(1) - 1)
        def _():
            o_ref[...] = acc_ref[...].astype(o_ref.dtype)

    @pl.when(jnp.logical_and(i >= nt_ref[0], j == pl.num_programs(1) - 1))
    def _():
        o_ref[...] = jnp.zeros(o_ref.shape, o_ref.dtype)


def _moe_ffn(h, buf_tok, wg, wu, wd, tile_e, n_tiles, tm):
    rows_total = buf_tok.shape[0]
    d = h.shape[1]
    f = wg.shape[2]
    tf = TF_FFN
    nf = f // tf

    def _fj(i, j, nt):
        return jnp.where(i < nt[0], j, nf - 1)

    def up_map(i, j, te, nt, tok):
        return (te[jnp.minimum(i, nt[0] - 1)], 0, _fj(i, j, nt))

    def down_map(i, j, te, nt, tok):
        return (te[jnp.minimum(i, nt[0] - 1)], _fj(i, j, nt), 0)

    return pl.pallas_call(
        _moe_ffn_kernel,
        grid_spec=pltpu.PrefetchScalarGridSpec(
            num_scalar_prefetch=3,
            grid=(rows_total // tm, nf),
            in_specs=[pl.BlockSpec(memory_space=pl.ANY),
                      pl.BlockSpec((1, d, tf), up_map),
                      pl.BlockSpec((1, d, tf), up_map),
                      pl.BlockSpec((1, tf, d), down_map)],
            out_specs=pl.BlockSpec((tm, d), lambda i, j, te, nt, tok: (i, 0)),
            scratch_shapes=[pltpu.VMEM((2, tm, d), F32), pltpu.VMEM((tm, d), BF16),
                            pltpu.VMEM((tm, d), F32), pltpu.SemaphoreType.DMA((2,))]),
        out_shape=jax.ShapeDtypeStruct((rows_total, d), F32),
        compiler_params=_params(("arbitrary", "arbitrary")),
        name="moe_ffn",
    )(tile_e, n_tiles, buf_tok, h, wg, wu, wd)


def _combine_kernel(pos_ref, ys_hbm, wgt_ref, x_ref, mod_ref, fg_ref, o_ref, buf, sem, *, tm, final):
    i = pl.program_id(0)
    slot = i % 2

    def gather(step, to_slot):
        for k in range(2):
            _row_gather(lambda r: pos_ref[2 * (step * tm + r) + k], ys_hbm, buf.at[to_slot, k],
                        sem.at[to_slot], tm)

    @pl.when(i == 0)
    def _():
        gather(0, 0)

    @pl.when(i + 1 < pl.num_programs(0))
    def _():
        gather(i + 1, 1 - slot)

    for k in range(2):
        _row_gather_wait(ys_hbm, buf.at[slot, k], sem.at[slot], tm)

    wgt = wgt_ref[...]
    y = wgt[:, 0:1] * buf[slot, 0] + wgt[:, 1:2] * buf[slot, 1]
    out = x_ref[...] + mod_ref[0, 5:6, :] * y
    if final:
        out = out * lax.rsqrt(jnp.mean(out * out, axis=-1, keepdims=True) + EPS) * fg_ref[...]
    o_ref[...] = out


def _moe_combine(ys, pos, wgt, x1, mod, final_g, seq, final):
    n, d = x1.shape
    tm = TM_COMB
    per_b = seq // tm
    return pl.pallas_call(
        functools.partial(_combine_kernel, tm=tm, final=final),
        grid_spec=pltpu.PrefetchScalarGridSpec(
            num_scalar_prefetch=1,
            grid=(n // tm,),
            in_specs=[pl.BlockSpec(memory_space=pl.ANY),
                      pl.BlockSpec((tm, LANES), lambda i, p: (i, 0)),
                      pl.BlockSpec((tm, d), lambda i, p: (i, 0)),
                      pl.BlockSpec((1, N_MOD, d), lambda i, p: (i // per_b, 0, 0)),
                      pl.BlockSpec((1, d), lambda i, p: (0, 0))],
            out_specs=pl.BlockSpec((tm, d), lambda i, p: (i, 0)),
            scratch_shapes=[pltpu.VMEM((2, 2, tm, d), F32),
                            pltpu.SemaphoreType.DMA((2,))]),
        out_shape=jax.ShapeDtypeStruct((n, d), F32),
        compiler_params=_params(("arbitrary",)),
        name="moe_combine_final" if final else "moe_combine",
    )(pos.reshape(-1), ys, wgt, x1, mod, final_g)


def _final_norm_kernel(x_ref, g_ref, o_ref):
    x = x_ref[...]
    o_ref[...] = x * lax.rsqrt(jnp.mean(x * x, axis=-1, keepdims=True) + EPS) * g_ref[...]


def _final_norm(x2, g):
    n, d = x2.shape
    tm = TM_PROJ
    return pl.pallas_call(
        _final_norm_kernel,
        grid=(n // tm,),
        in_specs=[pl.BlockSpec((tm, d), lambda i: (i, 0)), pl.BlockSpec((1, d), lambda i: (0, 0))],
        out_specs=pl.BlockSpec((tm, d), lambda i: (i, 0)),
        out_shape=jax.ShapeDtypeStruct((n, d), F32),
        compiler_params=_params(("arbitrary",)),
        name="final_norm",
    )(x2, g)


def kernel(x, c, positions, ada_w, ada_b, norm_mix_g, norm_ffn_g, w_in, w_out, sink, lam_q1, lam_k1, lam_q2, lam_k2, subln_g, ffn_w_gate, ffn_w_up, ffn_w_down, router_w, moe_w_gate, moe_w_up, moe_w_down, final_g):
    batch, seq, d = x.shape
    depth = w_in.shape[0]
    n = batch * seq

    c_pad = jnp.zeros((8, d), F32).at[:batch].set(c)
    mod_all = _adaln(c_pad, ada_w, ada_b)
    cos_t, sin_t = _rope_tables(positions)
    w_in_p = _permute_in_proj(w_in).astype(BF16)
    w_vdt = jnp.swapaxes(w_in[:, :, VD_COL0:], 1, 2).astype(BF16)
    w_out_b = w_out.astype(BF16)
    final_g2 = final_g.reshape(1, d)

    x2 = x.reshape(n, d)
    for layer in range(depth):
        lambda_init = 0.8 - 0.6 * math.exp(-0.3 * layer)
        mod = mod_all[layer, :batch].reshape(batch, N_MOD, d)
        proj, vt = _inproj(x2, mod, norm_mix_g[layer].reshape(1, d), w_in_p[layer], w_vdt[layer],
                           cos_t, sin_t, seq)
        ya = _window_attention(proj, sink[layer], batch, seq)
        lam_vecs = jnp.stack([lam_q1[layer], lam_k1[layer], lam_q2[layer], lam_k2[layer]])
        yd = _diff_attention(proj, vt, lam_vecs, subln_g[layer].reshape(1, LANES), batch, seq, lambda_init)
        i = layer // 2
        g_ffn = norm_ffn_g[layer].reshape(1, d)
        if layer % 2 == 0:
            x1, h = _outproj(ya, yd, w_out_b[layer], x2, mod, g_ffn, seq)
            x2 = _dense_ffn(h, ffn_w_gate[i].astype(BF16), ffn_w_up[i].astype(BF16),
                            ffn_w_down[i].astype(BF16), x1, mod, seq)
        else:
            rw = jnp.zeros((d, LANES), F32).at[:, :N_EXPERTS].set(router_w[i])
            x1, h, idx, wgt = _outproj(ya, yd, w_out_b[layer], x2, mod, g_ffn, seq, router_w=rw)
            buf_tok, pos, tile_e, n_tiles = _moe_plan(idx[:, :2], TM_MOE)
            ys = _moe_ffn(h, buf_tok, moe_w_gate[i].astype(BF16), moe_w_up[i].astype(BF16),
                          moe_w_down[i].astype(BF16), tile_e, n_tiles, TM_MOE)
            final = layer == depth - 1
            x2 = _moe_combine(ys, pos, wgt, x1, mod, final_g2, seq, final)
    if depth % 2 == 1:
        x2 = _final_norm(x2, final_g2)
    return x2.reshape(batch, seq, d)
```

```python
import functools
import math

import jax
import jax.numpy as jnp
from jax import lax
from jax.experimental import pallas as pl
from jax.experimental.pallas import tpu as pltpu

F32 = jnp.float32
BF16 = jnp.bfloat16

HEAD_DIM = 64
HALF_DIM = HEAD_DIM // 2
ROPE_THETA = 10000.0
EPS = 1e-6
SWA_Q_HEADS = 8
SWA_KV_HEADS = 2
WINDOW = 128
DIFF_HEADS = 4
N_EXPERTS = 8
N_MOD = 6
LANES = 128
VMEM_LIMIT = 56 * 1024 * 1024

QA_BLK, KA_BLK, QD_BLK, KD_BLK, VA_BLK = 0, 4, 6, 10, 14
ROPE_BLKS = 14
PROJ_COLS = 16 * LANES
VD_COL0 = 1792
Q_SCALE = HEAD_DIM ** -0.5
LOG2E = math.log2(math.e)

TM_PROJ = 512
TQ_WIN = 256
TQ_DIFF = 512
TK_DIFF = 1024
TM_FFN = 1024
TF_FFN = 512
TM_MOE = 512
TM_COMB = 256


def _permute_in_proj(w_in):
    depth, d, _ = w_in.shape
    qa, ka, va, qd, kd = 0, 512, 640, 768, 1280

    def pairs(lo, n_pairs):
        t = w_in[:, :, lo:lo + n_pairs * LANES].reshape(depth, d, n_pairs, 2, 2, HALF_DIM)
        return t.transpose(0, 1, 2, 4, 3, 5).reshape(depth, d, n_pairs * LANES)

    def dup_halves(lo):
        t = w_in[:, :, lo:lo + SWA_KV_HEADS * HEAD_DIM].reshape(depth, d, SWA_KV_HEADS, 2, 1, HALF_DIM)
        return jnp.broadcast_to(t, (depth, d, SWA_KV_HEADS, 2, 2, HALF_DIM)).reshape(depth, d, SWA_KV_HEADS * LANES)

    def dup_heads(lo):
        t = w_in[:, :, lo:lo + SWA_KV_HEADS * HEAD_DIM].reshape(depth, d, SWA_KV_HEADS, 1, HEAD_DIM)
        return jnp.broadcast_to(t, (depth, d, SWA_KV_HEADS, 2, HEAD_DIM)).reshape(depth, d, SWA_KV_HEADS * LANES)

    return jnp.concatenate([pairs(qa, 4), dup_halves(ka), pairs(qd, DIFF_HEADS), pairs(kd, DIFF_HEADS),
                            dup_heads(va)], axis=-1)


def _first_of_pair_mask():
    lane = lax.broadcasted_iota(jnp.int32, (1, LANES), 1)
    return (lane // HALF_DIM) % 2 == 0


def _silu(g):
    return g / (1.0 + jnp.exp(-g))


def _params(sem):
    return pltpu.CompilerParams(dimension_semantics=sem, vmem_limit_bytes=VMEM_LIMIT)


def _adaln_kernel(c_ref, w_ref, b_ref, o_ref):
    a = _silu(c_ref[...])
    o_ref[0] = jnp.dot(a, w_ref[0], preferred_element_type=F32,
                       precision=lax.Precision.HIGHEST) + b_ref[0]


def _adaln(c_pad, ada_w, ada_b):
    depth, d, cols = ada_w.shape
    tn = 1536
    return pl.pallas_call(
        _adaln_kernel,
        grid=(depth, cols // tn),
        in_specs=[pl.BlockSpec((8, d), lambda l, j: (0, 0)),
                  pl.BlockSpec((1, d, tn), lambda l, j: (l, 0, j)),
                  pl.BlockSpec((1, 1, tn), lambda l, j: (l, 0, j))],
        out_specs=pl.BlockSpec((1, 8, tn), lambda l, j: (l, 0, j)),
        out_shape=jax.ShapeDtypeStruct((depth, 8, cols), F32),
        compiler_params=_params(("arbitrary", "arbitrary")),
        name="adaln",
    )(c_pad, ada_w, ada_b.reshape(depth, 1, cols))


def _rope_table_kernel(pos_ref, cos_ref, sin_ref):
    lane = lax.broadcasted_iota(jnp.int32, (1, LANES), 1)
    freq = (lane % HALF_DIM).astype(F32)
    inv = 1.0 / (ROPE_THETA ** (2.0 * freq / HEAD_DIM))
    ang = pos_ref[...].astype(F32) * inv
    sign = jnp.where(lane < HEAD_DIM, -1.0, 1.0)
    cos_ref[...] = jnp.cos(ang)
    sin_ref[...] = jnp.sin(ang) * sign


def _rope_tables(positions):
    n = positions.size
    tm = 1024
    pos = positions.reshape(n, 1)
    return pl.pallas_call(
        _rope_table_kernel,
        grid=(n // tm,),
        in_specs=[pl.BlockSpec((tm, 1), lambda i: (i, 0))],
        out_specs=[pl.BlockSpec((tm, LANES), lambda i: (i, 0))] * 2,
        out_shape=[jax.ShapeDtypeStruct((n, LANES), F32)] * 2,
        compiler_params=_params(("arbitrary",)),
        name="rope_tables",
    )(pos)


def _rms_mod(x, g, scale, shift):
    y = x * lax.rsqrt(jnp.mean(x * x, axis=-1, keepdims=True) + EPS) * g
    return y * (1.0 + scale) + shift


def _inproj_kernel(x_ref, mod_ref, g_ref, w_ref, wvt_ref, cos_ref, sin_ref, o_ref, vt_ref):
    h = _rms_mod(x_ref[...], g_ref[...], mod_ref[0, 1:2, :], mod_ref[0, 0:1, :]).astype(BF16)
    proj = jnp.dot(h, w_ref[...], preferred_element_type=F32)
    vt_ref[...] = lax.dot_general(wvt_ref[...], h, (((1,), (1,)), ((), ())),
                                  preferred_element_type=F32).astype(BF16)
    cos = cos_ref[...]
    sin = sin_ref[...]
    for blk in range(ROPE_BLKS):
        t = proj[:, blk * LANES:(blk + 1) * LANES]
        r = t * cos + pltpu.roll(t, HEAD_DIM, axis=1) * sin
        if blk < KA_BLK:
            r = r * Q_SCALE
        elif QD_BLK <= blk < KD_BLK:
            r = r * (Q_SCALE * LOG2E)
        o_ref[:, blk * LANES:(blk + 1) * LANES] = r.astype(BF16)
    o_ref[:, ROPE_BLKS * LANES:] = proj[:, ROPE_BLKS * LANES:].astype(BF16)


def _inproj(x2, mod, g, w, wvt, layer, cos_t, sin_t, seq):
    n, d = x2.shape
    tm = TM_PROJ
    per_b = seq // tm
    vt_rows = wvt.shape[1]
    return pl.pallas_call(
        _inproj_kernel,
        grid=(n // tm,),
        in_specs=[pl.BlockSpec((tm, d), lambda i: (i, 0)),
                  pl.BlockSpec((1, N_MOD, d), lambda i: (i // per_b, 0, 0)),
                  pl.BlockSpec((1, d), lambda i: (0, 0)),
                  pl.BlockSpec((None, d, PROJ_COLS), lambda i: (layer, 0, 0)),
                  pl.BlockSpec((None, vt_rows, d), lambda i: (layer, 0, 0)),
                  pl.BlockSpec((tm, LANES), lambda i: (i, 0)),
                  pl.BlockSpec((tm, LANES), lambda i: (i, 0))],
        out_specs=[pl.BlockSpec((tm, PROJ_COLS), lambda i: (i, 0)),
                   pl.BlockSpec((vt_rows, tm), lambda i: (0, i))],
        out_shape=[jax.ShapeDtypeStruct((n, PROJ_COLS), BF16),
                   jax.ShapeDtypeStruct((vt_rows, n), BF16)],
        compiler_params=_params(("arbitrary",)),
        name="inproj",
    )(x2, mod, g, w, wvt, cos_t, sin_t)


def _win_kernel(sink_ref, q_ref, k_ref, v_ref, o_ref, *, tq, seq):
    j = pl.program_id(1)
    i = pl.program_id(2)
    width = tq + 2 * WINDOW
    start = pl.multiple_of(jnp.clip(i * tq - WINDOW, 0, seq - width), LANES)
    q = q_ref[...]
    kw = k_ref[pl.ds(start, width), :]
    vw = v_ref[pl.ds(start, width), :]
    first = _first_of_pair_mask()
    qpos = i * tq + lax.broadcasted_iota(jnp.int32, (tq, width), 0)
    kpos = start + lax.broadcasted_iota(jnp.int32, (tq, width), 1)
    valid = jnp.abs(qpos - kpos) <= WINDOW
    outs = []
    for half in range(2):
        qm = jnp.where(first if half == 0 else jnp.logical_not(first), q, jnp.zeros_like(q))
        s = lax.dot_general(qm, kw, (((1,), (1,)), ((), ())), preferred_element_type=F32)
        s = jnp.where(valid, s, -1e30)
        sk = sink_ref[2 * j + half]
        m = jnp.maximum(jnp.max(s, axis=-1, keepdims=True), sk)
        p = jnp.exp(s - m)
        den = jnp.sum(p, axis=-1, keepdims=True) + jnp.exp(sk - m)
        outs.append(jnp.dot(p.astype(BF16), vw, preferred_element_type=F32) / den)
    lane = lax.broadcasted_iota(jnp.int32, (1, LANES), 1)
    o_ref[...] = jnp.where(lane < HEAD_DIM, outs[0], outs[1]).astype(o_ref.dtype)


def _window_attention(proj, sink, batch, seq):
    n = proj.shape[0]
    tq = TQ_WIN
    nq = seq // tq
    kern = functools.partial(_win_kernel, tq=tq, seq=seq)
    return pl.pallas_call(
        kern,
        grid=(batch, SWA_Q_HEADS // 2, nq),
        in_specs=[pl.BlockSpec(memory_space=pltpu.SMEM),
                  pl.BlockSpec((tq, LANES), lambda b, j, i: (b * nq + i, QA_BLK + j)),
                  pl.BlockSpec((seq, LANES), lambda b, j, i: (b, KA_BLK + j // 2)),
                  pl.BlockSpec((seq, LANES), lambda b, j, i: (b, VA_BLK + j // 2))],
        out_specs=pl.BlockSpec((tq, LANES), lambda b, j, i: (b * nq + i, j)),
        out_shape=jax.ShapeDtypeStruct((n, 4 * LANES), BF16),
        compiler_params=_params(("arbitrary", "arbitrary", "arbitrary")),
        name="window_attn",
    )(sink, proj, proj, proj)


def _diff_kernel(lam_ref, g_ref, q_ref, k_ref, vt_ref, o_ref, qt_sc, s_a, s_b, cm_a, cm_b, m_sc, l_sc, acc_sc,
                 *, tk, seq, lambda_init):
    tq = q_ref.shape[0]
    nk = seq // tk
    q = q_ref[...].astype(F32)
    first = _first_of_pair_mask()
    zero = jnp.zeros_like(q)
    qt_sc[:, :tq] = jnp.where(first, q, zero).T.astype(BF16)
    qt_sc[:, tq:] = jnp.where(first, zero, q).T.astype(BF16)
    m_sc[...] = jnp.full(m_sc.shape, -1e30, F32)
    l_sc[...] = jnp.zeros(l_sc.shape, F32)
    acc_sc[...] = jnp.zeros(acc_sc.shape, F32)

    def scores(t, s_ref, cm_ref):
        off = pl.multiple_of(t * tk, tk)
        s = jnp.dot(k_ref[pl.ds(off, tk), :], qt_sc[...], preferred_element_type=F32)
        s_ref[...] = s
        cm_ref[...] = jnp.max(s, axis=0, keepdims=True)

    def consume(t, s_ref, cm_ref):
        off = pl.multiple_of(t * tk, tk)
        m_prev = m_sc[...]
        m_new = jnp.maximum(m_prev, cm_ref[...])
        alpha = jnp.exp2(m_prev - m_new)
        p = jnp.exp2(s_ref[...] - m_new)
        l_sc[...] = alpha * l_sc[...] + jnp.sum(p, axis=0, keepdims=True)
        vb = vt_ref[:, pl.ds(off, tk)]
        acc_sc[...] = alpha * acc_sc[...] + jnp.dot(vb, p.astype(BF16), preferred_element_type=F32)
        m_sc[...] = m_new

    scores(0, s_a, cm_a)

    def body(u, carry):
        t = 2 * u
        scores(t + 1, s_b, cm_b)
        consume(t, s_a, cm_a)
        scores(t + 2, s_a, cm_a)
        consume(t + 1, s_b, cm_b)
        return carry

    lax.fori_loop(0, nk // 2 - 1, body, 0)
    scores(nk - 1, s_b, cm_b)
    consume(nk - 2, s_a, cm_a)
    consume(nk - 1, s_b, cm_b)

    lam1 = jnp.sum(lam_ref[0:1, :] * lam_ref[1:2, :], axis=-1, keepdims=True)
    lam2 = jnp.sum(lam_ref[2:3, :] * lam_ref[3:4, :], axis=-1, keepdims=True)
    lam = jnp.exp(lam1) - jnp.exp(lam2) + lambda_init
    o = acc_sc[...] / l_sc[...]
    o = o[:, :tq] - lam * o[:, tq:]
    y = o * lax.rsqrt(jnp.mean(o * o, axis=0, keepdims=True) + EPS)
    y = y.T * (g_ref[...] * (1.0 - lambda_init))
    o_ref[...] = y.astype(o_ref.dtype)


def _diff_attention(proj, vt, lam_vecs, subln_g, batch, seq, lambda_init):
    n = proj.shape[0]
    tq, tk = TQ_DIFF, TK_DIFF
    nq = seq // tq
    assert seq % (2 * tk) == 0 and seq % tq == 0, "the pipelined key loop consumes key blocks in pairs"
    kern = functools.partial(_diff_kernel, tk=tk, seq=seq, lambda_init=lambda_init)
    return pl.pallas_call(
        kern,
        grid=(batch, DIFF_HEADS, nq),
        in_specs=[pl.BlockSpec((4, HEAD_DIM), lambda b, h, i: (0, 0)),
                  pl.BlockSpec((1, LANES), lambda b, h, i: (0, 0)),
                  pl.BlockSpec((tq, LANES), lambda b, h, i: (b * nq + i, QD_BLK + h)),
                  pl.BlockSpec((seq, LANES), lambda b, h, i: (b, KD_BLK + h)),
                  pl.BlockSpec((LANES, seq), lambda b, h, i: (h, b))],
        out_specs=pl.BlockSpec((tq, LANES), lambda b, h, i: (b * nq + i, h)),
        out_shape=jax.ShapeDtypeStruct((n, DIFF_HEADS * LANES), BF16),
        scratch_shapes=[pltpu.VMEM((LANES, 2 * tq), BF16),
                        pltpu.VMEM((tk, 2 * tq), F32), pltpu.VMEM((tk, 2 * tq), F32),
                        pltpu.VMEM((1, 2 * tq), F32), pltpu.VMEM((1, 2 * tq), F32),
                        pltpu.VMEM((1, 2 * tq), F32), pltpu.VMEM((1, 2 * tq), F32),
                        pltpu.VMEM((LANES, 2 * tq), F32)],
        compiler_params=_params(("arbitrary", "arbitrary", "arbitrary")),
        name="diff_attn",
    )(lam_vecs, subln_g, proj, proj, vt)


def _route_top2(logits):
    lane = lax.broadcasted_iota(jnp.int32, logits.shape, 1)
    lg = jnp.where(lane < N_EXPERTS, logits, -jnp.inf)
    v1 = jnp.max(lg, axis=-1, keepdims=True)
    i1 = jnp.min(jnp.where(lg == v1, lane, LANES), axis=-1, keepdims=True)
    lg2 = jnp.where(lane == i1, -jnp.inf, lg)
    v2 = jnp.max(lg2, axis=-1, keepdims=True)
    i2 = jnp.min(jnp.where(lg2 == v2, lane, LANES), axis=-1, keepdims=True)
    e = jnp.exp(v2 - v1)
    w1 = 1.0 / (1.0 + e)
    w2 = e / (1.0 + e)
    idx = jnp.where(lane == 0, i1, jnp.where(lane == 1, i2, 0))
    wgt = jnp.where(lane == 0, w1, jnp.where(lane == 1, w2, 0.0))
    return idx, wgt


def _outproj_kernel(ya_ref, yd_ref, w_ref, x_ref, mod_ref, g_ref, *rest, moe):
    if moe:
        rwh_ref, rwl_ref, x1_ref, h_ref, idx_ref, wgt_ref = rest
    else:
        x1_ref, h_ref = rest
    half = ya_ref.shape[1]
    y = (jnp.dot(ya_ref[...], w_ref[:half, :], preferred_element_type=F32)
         + jnp.dot(yd_ref[...], w_ref[half:, :], preferred_element_type=F32))
    x1 = x_ref[...] + mod_ref[0, 2:3, :] * y
    x1_ref[...] = x1
    h = _rms_mod(x1, g_ref[...], mod_ref[0, 4:5, :], mod_ref[0, 3:4, :])
    h_ref[...] = h.astype(h_ref.dtype)
    if moe:
        h_hi = h.astype(BF16)
        h_lo = (h - h_hi.astype(F32)).astype(BF16)
        logits = (jnp.dot(h_hi, rwh_ref[...], preferred_element_type=F32)
                  + jnp.dot(h_hi, rwl_ref[...], preferred_element_type=F32)
                  + jnp.dot(h_lo, rwh_ref[...], preferred_element_type=F32))
        idx, wgt = _route_top2(logits)
        idx_ref[...] = idx
        wgt_ref[...] = wgt


def _outproj(ya, yd, w, layer, x2, mod, g, seq, router_w=None):
    n, d = x2.shape
    tm = TM_PROJ
    per_b = seq // tm
    moe = router_w is not None
    row = lambda i: (i, 0)
    in_specs = [pl.BlockSpec((tm, ya.shape[1]), row),
                pl.BlockSpec((tm, yd.shape[1]), row),
                pl.BlockSpec((None,) + w.shape[1:], lambda i: (layer, 0, 0)),
                pl.BlockSpec((tm, d), row),
                pl.BlockSpec((1, N_MOD, d), lambda i: (i // per_b, 0, 0)),
                pl.BlockSpec((1, d), lambda i: (0, 0))]
    out_specs = [pl.BlockSpec((tm, d), row), pl.BlockSpec((tm, d), row)]
    out_shape = [jax.ShapeDtypeStruct((n, d), F32), jax.ShapeDtypeStruct((n, d), F32 if moe else BF16)]
    args = [ya, yd, w, x2, mod, g]
    if moe:
        in_specs += [pl.BlockSpec((d, LANES), lambda i: (0, 0))] * 2
        out_specs += [pl.BlockSpec((tm, LANES), row)] * 2
        out_shape += [jax.ShapeDtypeStruct((n, LANES), jnp.int32), jax.ShapeDtypeStruct((n, LANES), F32)]
        args += list(router_w)
    return pl.pallas_call(
        functools.partial(_outproj_kernel, moe=moe),
        grid=(n // tm,),
        in_specs=in_specs, out_specs=out_specs, out_shape=out_shape,
        compiler_params=_params(("arbitrary",)),
        name="outproj_moe" if moe else "outproj",
    )(*args)


def _swiglu_step(h, wg_ref, wu_ref, wd_ref):
    g = jnp.dot(h, wg_ref[...], preferred_element_type=F32)
    u = jnp.dot(h, wu_ref[...], preferred_element_type=F32)
    return jnp.dot((_silu(g) * u).astype(BF16), wd_ref[...], preferred_element_type=F32)


def _ffn_kernel(h_ref, wg_ref, wu_ref, wd_ref, x_ref, mod_ref, o_ref, acc_ref):
    j = pl.program_id(1)

    @pl.when(j == 0)
    def _():
        acc_ref[...] = jnp.zeros(acc_ref.shape, F32)

    acc_ref[...] += _swiglu_step(h_ref[...], wg_ref, wu_ref, wd_ref)

    @pl.when(j == pl.num_programs(1) - 1)
    def _():
        o_ref[...] = x_ref[...] + mod_ref[0, 5:6, :] * acc_ref[...]


def _dense_ffn(h, wg, wu, wd, li, x1, mod, seq):
    n, d = x1.shape
    f = wg.shape[2]
    tm, tf = TM_FFN, TF_FFN
    per_b = seq // tm
    return pl.pallas_call(
        _ffn_kernel,
        grid=(n // tm, f // tf),
        in_specs=[pl.BlockSpec((tm, d), lambda i, j: (i, 0)),
                  pl.BlockSpec((None, d, tf), lambda i, j: (li, 0, j)),
                  pl.BlockSpec((None, d, tf), lambda i, j: (li, 0, j)),
                  pl.BlockSpec((None, tf, d), lambda i, j: (li, j, 0)),
                  pl.BlockSpec((tm, d), lambda i, j: (i, 0)),
                  pl.BlockSpec((1, N_MOD, d), lambda i, j: (i // per_b, 0, 0))],
        out_specs=pl.BlockSpec((tm, d), lambda i, j: (i, 0)),
        out_shape=jax.ShapeDtypeStruct((n, d), F32),
        scratch_shapes=[pltpu.VMEM((tm, d), F32)],
        compiler_params=_params(("arbitrary", "arbitrary")),
        name="dense_ffn",
    )(h, wg, wu, wd, x1, mod)


def _moe_plan(idx, tm):
    n = idx.shape[0]
    flat_e = idx.reshape(-1)
    onehot = (flat_e[:, None] == jnp.arange(N_EXPERTS, dtype=jnp.int32)[None, :]).astype(jnp.int32)
    csum = jnp.cumsum(onehot, axis=0)
    rank = jnp.sum((csum - onehot) * onehot, axis=1)
    counts = csum[-1]
    padded = (counts + tm - 1) // tm * tm
    pad_ends = jnp.cumsum(padded)
    pad_starts = pad_ends - padded
    dest = (pad_starts[flat_e] + rank).astype(jnp.int32)
    rows = 2 * n + N_EXPERTS * tm
    flat_tok = jnp.arange(2 * n, dtype=jnp.int32) // 2
    buf_tok = jnp.zeros((rows,), jnp.int32).at[dest].set(flat_tok)
    n_tiles = (pad_ends[-1] // tm).astype(jnp.int32).reshape(1)
    tile_start = jnp.arange(rows // tm, dtype=jnp.int32) * tm
    tile_e = jnp.minimum(jnp.sum((pad_ends[None, :] <= tile_start[:, None]).astype(jnp.int32), axis=1),
                         N_EXPERTS - 1).astype(jnp.int32)
    return buf_tok, dest.reshape(n, 2), tile_e, n_tiles


def _row_gather(idx_of, src_hbm, dst_ref, sem, rows):
    def issue(r, carry):
        pltpu.make_async_copy(src_hbm.at[pl.ds(idx_of(r), 1)], dst_ref.at[pl.ds(r, 1)], sem).start()
        return carry

    lax.fori_loop(0, rows, issue, 0, unroll=8)


def _row_gather_wait(src_hbm, dst_ref, sem, rows):
    pltpu.make_async_copy(src_hbm.at[pl.ds(0, rows)], dst_ref, sem).wait()


def _moe_ffn_kernel(te_ref, nt_ref, tok_ref, h_hbm, wg_ref, wu_ref, wd_ref, o_ref, xbuf, xb16, acc_ref, sem, *, nf):
    i = pl.program_id(0)
    j = pl.program_id(1)
    tm = xb16.shape[0]
    n_tiles = nt_ref[0]
    slot = i % 2
    other = 1 - slot
    nxt = jnp.minimum(i + 1, n_tiles - 1)
    per_step = tm // nf

    def start_next(r):
        tok = tok_ref[nxt * tm + r]
        pltpu.make_async_copy(h_hbm.at[pl.ds(tok, 1)], xbuf.at[other, pl.ds(r, 1)], sem.at[other]).start()

    @pl.when(jnp.logical_and(i == 0, j == 0))
    def _():
        _row_gather(lambda r: tok_ref[r], h_hbm, xbuf.at[0], sem.at[0], tm)

    @pl.when(i < n_tiles)
    def _():
        @pl.when(j == 0)
        def _():
            _row_gather_wait(h_hbm, xbuf.at[slot], sem.at[slot], tm)
            xb16[...] = xbuf[slot].astype(BF16)
            acc_ref[...] = jnp.zeros(acc_ref.shape, F32)
            for r in range(per_step * nf, tm):
                start_next(r)

        for u in range(per_step):
            start_next(j * per_step + u)
        acc_ref[...] += _swiglu_step(xb16[...], wg_ref.at[0], wu_ref.at[0], wd_ref.at[0])

        @pl.when(j == nf - 1)
        def _():
            o_ref[...] = acc_ref[...].astype(o_ref.dtype)

        @pl.when(jnp.logical_and(j == nf - 1, i == n_tiles - 1))
        def _():
            _row_gather_wait(h_hbm, xbuf.at[other], sem.at[other], tm)

    @pl.when(jnp.logical_and(i >= nt_ref[0], j == pl.num_programs(1) - 1))
    def _():
        o_ref[...] = jnp.zeros(o_ref.shape, o_ref.dtype)


def _moe_ffn(h, buf_tok, wg, wu, wd, li, tile_e, n_tiles, tm):
    rows_total = buf_tok.shape[0]
    d = h.shape[1]
    f = wg.shape[3]
    tf = TF_FFN
    nf = f // tf

    def _fj(i, j, nt):
        return jnp.where(i < nt[0], j, nf - 1)

    def up_map(i, j, te, nt, tok):
        return (li, te[jnp.minimum(i, nt[0] - 1)], 0, _fj(i, j, nt))

    def down_map(i, j, te, nt, tok):
        return (li, te[jnp.minimum(i, nt[0] - 1)], _fj(i, j, nt), 0)

    return pl.pallas_call(
        functools.partial(_moe_ffn_kernel, nf=nf),
        grid_spec=pltpu.PrefetchScalarGridSpec(
            num_scalar_prefetch=3,
            grid=(rows_total // tm, nf),
            in_specs=[pl.BlockSpec(memory_space=pl.ANY),
                      pl.BlockSpec((None, 1, d, tf), up_map),
                      pl.BlockSpec((None, 1, d, tf), up_map),
                      pl.BlockSpec((None, 1, tf, d), down_map)],
            out_specs=pl.BlockSpec((tm, d), lambda i, j, te, nt, tok: (i, 0)),
            scratch_shapes=[pltpu.VMEM((2, tm, d), F32), pltpu.VMEM((tm, d), BF16),
                            pltpu.VMEM((tm, d), F32), pltpu.SemaphoreType.DMA((2,))]),
        out_shape=jax.ShapeDtypeStruct((rows_total, d), F32),
        compiler_params=_params(("arbitrary", "arbitrary")),
        name="moe_ffn",
    )(tile_e, n_tiles, buf_tok, h, wg, wu, wd)


def _combine_kernel(pos_ref, ys_hbm, wgt_ref, x_ref, mod_ref, fg_ref, o_ref, buf, sem, *, tm, final):
    i = pl.program_id(0)
    slot = i % 2

    def gather(step, to_slot):
        for k in range(2):
            _row_gather(lambda r: pos_ref[2 * (step * tm + r) + k], ys_hbm, buf.at[to_slot, k],
                        sem.at[to_slot], tm)

    @pl.when(i == 0)
    def _():
        gather(0, 0)

    @pl.when(i + 1 < pl.num_programs(0))
    def _():
        gather(i + 1, 1 - slot)

    for k in range(2):
        _row_gather_wait(ys_hbm, buf.at[slot, k], sem.at[slot], tm)

    wgt = wgt_ref[...]
    y = wgt[:, 0:1] * buf[slot, 0] + wgt[:, 1:2] * buf[slot, 1]
    out = x_ref[...] + mod_ref[0, 5:6, :] * y
    if final:
        out = out * lax.rsqrt(jnp.mean(out * out, axis=-1, keepdims=True) + EPS) * fg_ref[...]
    o_ref[...] = out


def _moe_combine(ys, pos, wgt, x1, mod, final_g, seq, final):
    n, d = x1.shape
    tm = TM_COMB
    per_b = seq // tm
    return pl.pallas_call(
        functools.partial(_combine_kernel, tm=tm, final=final),
        grid_spec=pltpu.PrefetchScalarGridSpec(
            num_scalar_prefetch=1,
            grid=(n // tm,),
            in_specs=[pl.BlockSpec(memory_space=pl.ANY),
                      pl.BlockSpec((tm, LANES), lambda i, p: (i, 0)),
                      pl.BlockSpec((tm, d), lambda i, p: (i, 0)),
                      pl.BlockSpec((1, N_MOD, d), lambda i, p: (i // per_b, 0, 0)),
                      pl.BlockSpec((1, d), lambda i, p: (0, 0))],
            out_specs=pl.BlockSpec((tm, d), lambda i, p: (i, 0)),
            scratch_shapes=[pltpu.VMEM((2, 2, tm, d), F32),
                            pltpu.SemaphoreType.DMA((2,))]),
        out_shape=jax.ShapeDtypeStruct((n, d), F32),
        compiler_params=_params(("arbitrary",)),
        name="moe_combine_final" if final else "moe_combine",
    )(pos.reshape(-1), ys, wgt, x1, mod, final_g)


def _final_norm_kernel(x_ref, g_ref, o_ref):
    x = x_ref[...]
    o_ref[...] = x * lax.rsqrt(jnp.mean(x * x, axis=-1, keepdims=True) + EPS) * g_ref[...]


def _final_norm(x2, g):
    n, d = x2.shape
    tm = TM_PROJ
    return pl.pallas_call(
        _final_norm_kernel,
        grid=(n // tm,),
        in_specs=[pl.BlockSpec((tm, d), lambda i: (i, 0)), pl.BlockSpec((1, d), lambda i: (0, 0))],
        out_specs=pl.BlockSpec((tm, d), lambda i: (i, 0)),
        out_shape=jax.ShapeDtypeStruct((n, d), F32),
        compiler_params=_params(("arbitrary",)),
        name="final_norm",
    )(x2, g)


def kernel(x, c, positions, ada_w, ada_b, norm_mix_g, norm_ffn_g, w_in, w_out, sink, lam_q1, lam_k1, lam_q2, lam_k2, subln_g, ffn_w_gate, ffn_w_up, ffn_w_down, router_w, moe_w_gate, moe_w_up, moe_w_down, final_g):
    batch, seq, d = x.shape
    depth = w_in.shape[0]
    n = batch * seq

    c_pad = jnp.zeros((8, d), F32).at[:batch].set(c)
    mod_all = _adaln(c_pad, ada_w, ada_b)
    cos_t, sin_t = _rope_tables(positions)
    w_in_p = _permute_in_proj(w_in).astype(BF16)
    w_vdt = jnp.swapaxes(w_in[:, :, VD_COL0:], 1, 2).astype(BF16)
    w_out_b = w_out.astype(BF16)
    ffn_w = [w.astype(BF16) for w in (ffn_w_gate, ffn_w_up, ffn_w_down)]
    moe_w = [w.astype(BF16) for w in (moe_w_gate, moe_w_up, moe_w_down)]
    final_g2 = final_g.reshape(1, d)

    x2 = x.reshape(n, d)
    for layer in range(depth):
        lambda_init = 0.8 - 0.6 * math.exp(-0.3 * layer)
        mod = mod_all[layer, :batch].reshape(batch, N_MOD, d)
        proj, vt = _inproj(x2, mod, norm_mix_g[layer].reshape(1, d), w_in_p, w_vdt, layer, cos_t, sin_t, seq)
        ya = _window_attention(proj, sink[layer], batch, seq)
        lam_vecs = jnp.stack([lam_q1[layer], lam_k1[layer], lam_q2[layer], lam_k2[layer]])
        yd = _diff_attention(proj, vt, lam_vecs, subln_g[layer].reshape(1, LANES), batch, seq, lambda_init)
        i = layer // 2
        g_ffn = norm_ffn_g[layer].reshape(1, d)
        if layer % 2 == 0:
            x1, h = _outproj(ya, yd, w_out_b, layer, x2, mod, g_ffn, seq)
            x2 = _dense_ffn(h, *ffn_w, i, x1, mod, seq)
        else:
            rw = jnp.zeros((d, LANES), F32).at[:, :N_EXPERTS].set(router_w[i])
            rw_hi = rw.astype(BF16)
            rw_lo = (rw - rw_hi.astype(F32)).astype(BF16)
            x1, h, idx, wgt = _outproj(ya, yd, w_out_b, layer, x2, mod, g_ffn, seq, router_w=(rw_hi, rw_lo))
            buf_tok, pos, tile_e, n_tiles = _moe_plan(idx[:, :2], TM_MOE)
            ys = _moe_ffn(h, buf_tok, *moe_w, i, tile_e, n_tiles, TM_MOE)
            final = layer == depth - 1
            x2 = _moe_combine(ys, pos, wgt, x1, mod, final_g2, seq, final)
    if depth % 2 == 1:
        x2 = _final_norm(x2, final_g2)
    return x2.reshape(batch, seq, d)
```

```python
import functools
import math

import jax
import jax.numpy as jnp
from jax import lax
from jax.experimental import pallas as pl
from jax.experimental.pallas import tpu as pltpu

F32 = jnp.float32
BF16 = jnp.bfloat16

HEAD_DIM = 64
HALF_DIM = HEAD_DIM // 2
ROPE_THETA = 10000.0
EPS = 1e-6
SWA_Q_HEADS = 8
SWA_KV_HEADS = 2
WINDOW = 128
DIFF_HEADS = 4
N_EXPERTS = 8
N_MOD = 6
LANES = 128
VMEM_LIMIT = 56 * 1024 * 1024

QA_BLK, KA_BLK, QD_BLK, KD_BLK = 0, 4, 6, 10
ROPE_BLKS = 14
PROJ_COLS = ROPE_BLKS * LANES
VA_COL0 = 640
VD_COL0 = 1792
VT_WIN_BLK = 4
Q_SCALE = HEAD_DIM ** -0.5
LOG2E = math.log2(math.e)

TM_PROJ = 512
TQ_WIN = 256
TQ_DIFF = 512
TK_DIFF = 1024
TM_FFN = 1024
TF_FFN = 512
TM_MOE = 512
TM_COMB = 256


def _permute_in_proj(w_in):
    depth, d, _ = w_in.shape
    qa, ka, qd, kd = 0, 512, 768, 1280

    def pairs(lo, n_pairs):
        t = w_in[:, :, lo:lo + n_pairs * LANES].reshape(depth, d, n_pairs, 2, 2, HALF_DIM)
        return t.transpose(0, 1, 2, 4, 3, 5).reshape(depth, d, n_pairs * LANES)

    def dup_halves(lo):
        t = w_in[:, :, lo:lo + SWA_KV_HEADS * HEAD_DIM].reshape(depth, d, SWA_KV_HEADS, 2, 1, HALF_DIM)
        return jnp.broadcast_to(t, (depth, d, SWA_KV_HEADS, 2, 2, HALF_DIM)).reshape(depth, d, SWA_KV_HEADS * LANES)

    return jnp.concatenate([pairs(qa, 4), dup_halves(ka), pairs(qd, DIFF_HEADS), pairs(kd, DIFF_HEADS)], axis=-1)


def _value_rows(w_in):
    depth, d, _ = w_in.shape
    vd = jnp.swapaxes(w_in[:, :, VD_COL0:], 1, 2)
    va = jnp.swapaxes(w_in[:, :, VA_COL0:VA_COL0 + SWA_KV_HEADS * HEAD_DIM], 1, 2)
    va = jnp.broadcast_to(va.reshape(depth, SWA_KV_HEADS, 1, HEAD_DIM, d), (depth, SWA_KV_HEADS, 2, HEAD_DIM, d))
    return jnp.concatenate([vd, va.reshape(depth, SWA_KV_HEADS * LANES, d)], axis=1)


def _first_of_pair_mask():
    lane = lax.broadcasted_iota(jnp.int32, (1, LANES), 1)
    return (lane // HALF_DIM) % 2 == 0


def _silu(g):
    return g / (1.0 + jnp.exp(-g))


def _params(sem):
    return pltpu.CompilerParams(dimension_semantics=sem, vmem_limit_bytes=VMEM_LIMIT)


def _adaln_kernel(c_ref, w_ref, b_ref, o_ref):
    a = _silu(c_ref[...])
    o_ref[0] = jnp.dot(a, w_ref[0], preferred_element_type=F32,
                       precision=lax.Precision.HIGHEST) + b_ref[0]


def _adaln(c_pad, ada_w, ada_b):
    depth, d, cols = ada_w.shape
    tn = 1536
    return pl.pallas_call(
        _adaln_kernel,
        grid=(depth, cols // tn),
        in_specs=[pl.BlockSpec((8, d), lambda l, j: (0, 0)),
                  pl.BlockSpec((1, d, tn), lambda l, j: (l, 0, j)),
                  pl.BlockSpec((1, 1, tn), lambda l, j: (l, 0, j))],
        out_specs=pl.BlockSpec((1, 8, tn), lambda l, j: (l, 0, j)),
        out_shape=jax.ShapeDtypeStruct((depth, 8, cols), F32),
        compiler_params=_params(("arbitrary", "arbitrary")),
        name="adaln",
    )(c_pad, ada_w, ada_b.reshape(depth, 1, cols))


def _rope_table_kernel(pos_ref, cos_ref, sin_ref):
    lane = lax.broadcasted_iota(jnp.int32, (1, LANES), 1)
    freq = (lane % HALF_DIM).astype(F32)
    inv = 1.0 / (ROPE_THETA ** (2.0 * freq / HEAD_DIM))
    ang = pos_ref[...].astype(F32) * inv
    sign = jnp.where(lane < HEAD_DIM, -1.0, 1.0)
    cos_ref[...] = jnp.cos(ang)
    sin_ref[...] = jnp.sin(ang) * sign


def _rope_tables(positions):
    n = positions.size
    tm = 1024
    pos = positions.reshape(n, 1)
    return pl.pallas_call(
        _rope_table_kernel,
        grid=(n // tm,),
        in_specs=[pl.BlockSpec((tm, 1), lambda i: (i, 0))],
        out_specs=[pl.BlockSpec((tm, LANES), lambda i: (i, 0))] * 2,
        out_shape=[jax.ShapeDtypeStruct((n, LANES), F32)] * 2,
        compiler_params=_params(("arbitrary",)),
        name="rope_tables",
    )(pos)


def _rms_mod(x, g, scale, shift):
    y = x * lax.rsqrt(jnp.mean(x * x, axis=-1, keepdims=True) + EPS) * g
    return y * (1.0 + scale) + shift


def _inproj_kernel(x_ref, mod_ref, g_ref, w_ref, wvt_ref, cos_ref, sin_ref, o_ref, vt_ref):
    h = _rms_mod(x_ref[...], g_ref[...], mod_ref[0, 1:2, :], mod_ref[0, 0:1, :]).astype(BF16)
    proj = jnp.dot(h, w_ref[...], preferred_element_type=F32)
    vt_ref[...] = lax.dot_general(wvt_ref[...], h, (((1,), (1,)), ((), ())),
                                  preferred_element_type=F32).astype(BF16)
    cos = cos_ref[...]
    sin = sin_ref[...]
    for blk in range(ROPE_BLKS):
        t = proj[:, blk * LANES:(blk + 1) * LANES]
        r = t * cos + pltpu.roll(t, HEAD_DIM, axis=1) * sin
        if blk < KA_BLK or QD_BLK <= blk < KD_BLK:
            r = r * (Q_SCALE * LOG2E)
        o_ref[:, blk * LANES:(blk + 1) * LANES] = r.astype(BF16)


def _inproj(x2, mod, g, w, wvt, layer, cos_t, sin_t, seq):
    n, d = x2.shape
    tm = TM_PROJ
    per_b = seq // tm
    vt_rows = wvt.shape[1]
    return pl.pallas_call(
        _inproj_kernel,
        grid=(n // tm,),
        in_specs=[pl.BlockSpec((tm, d), lambda i: (i, 0)),
                  pl.BlockSpec((1, N_MOD, d), lambda i: (i // per_b, 0, 0)),
                  pl.BlockSpec((1, d), lambda i: (0, 0)),
                  pl.BlockSpec((None, d, PROJ_COLS), lambda i: (layer, 0, 0)),
                  pl.BlockSpec((None, vt_rows, d), lambda i: (layer, 0, 0)),
                  pl.BlockSpec((tm, LANES), lambda i: (i, 0)),
                  pl.BlockSpec((tm, LANES), lambda i: (i, 0))],
        out_specs=[pl.BlockSpec((tm, PROJ_COLS), lambda i: (i, 0)),
                   pl.BlockSpec((vt_rows, tm), lambda i: (0, i))],
        out_shape=[jax.ShapeDtypeStruct((n, PROJ_COLS), BF16),
                   jax.ShapeDtypeStruct((vt_rows, n), BF16)],
        compiler_params=_params(("arbitrary",)),
        name="inproj",
    )(x2, mod, g, w, wvt, cos_t, sin_t)


def _win_kernel(sink_ref, q_ref, k_ref, vt_ref, o_ref, qt_sc, *, tq, seq):
    g = pl.program_id(1)
    i = pl.program_id(2)
    heads = SWA_Q_HEADS // SWA_KV_HEADS
    width = tq + 2 * WINDOW
    start = pl.multiple_of(jnp.clip(i * tq - WINDOW, 0, seq - width), LANES)
    first = _first_of_pair_mask()
    for blk in range(heads // 2):
        q = q_ref[:, blk * LANES:(blk + 1) * LANES].astype(F32)
        zero = jnp.zeros_like(q)
        qt_sc[:, (2 * blk) * tq:(2 * blk + 1) * tq] = jnp.where(first, q, zero).T.astype(BF16)
        qt_sc[:, (2 * blk + 1) * tq:(2 * blk + 2) * tq] = jnp.where(first, zero, q).T.astype(BF16)

    s = jnp.dot(k_ref[pl.ds(start, width), :], qt_sc[...], preferred_element_type=F32)
    kpos = start + lax.broadcasted_iota(jnp.int32, (width, tq), 0)
    qpos = i * tq + lax.broadcasted_iota(jnp.int32, (width, tq), 1)
    valid = jnp.abs(qpos - kpos) <= WINDOW
    s = jnp.concatenate([jnp.where(valid, s[:, c * tq:(c + 1) * tq], -1e30) for c in range(heads)], axis=1)
    col = lax.broadcasted_iota(jnp.int32, (1, heads * tq), 1)
    sk = jnp.zeros((1, heads * tq), F32)
    for c in range(heads):
        sk = jnp.where(col // tq == c, sink_ref[heads * g + c] * LOG2E, sk)
    m = jnp.maximum(jnp.max(s, axis=0, keepdims=True), sk)
    p = jnp.exp2(s - m)
    den = jnp.sum(p, axis=0, keepdims=True) + jnp.exp2(sk - m)
    o = jnp.dot(vt_ref[:, pl.ds(start, width)], p.astype(BF16), preferred_element_type=F32) / den
    lane = lax.broadcasted_iota(jnp.int32, (1, LANES), 1)
    for blk in range(heads // 2):
        oa = o[:, (2 * blk) * tq:(2 * blk + 1) * tq].T
        ob = o[:, (2 * blk + 1) * tq:(2 * blk + 2) * tq].T
        o_ref[:, blk * LANES:(blk + 1) * LANES] = jnp.where(lane < HEAD_DIM, oa, ob).astype(o_ref.dtype)


def _window_attention(proj, vt, sink, batch, seq):
    n = proj.shape[0]
    tq = TQ_WIN
    nq = seq // tq
    heads = SWA_Q_HEADS // SWA_KV_HEADS
    cols = heads // 2 * LANES
    kern = functools.partial(_win_kernel, tq=tq, seq=seq)
    return pl.pallas_call(
        kern,
        grid=(batch, SWA_KV_HEADS, nq),
        in_specs=[pl.BlockSpec(memory_space=pltpu.SMEM),
                  pl.BlockSpec((tq, cols), lambda b, g, i: (b * nq + i, g)),
                  pl.BlockSpec((seq, LANES), lambda b, g, i: (b, KA_BLK + g)),
                  pl.BlockSpec((LANES, seq), lambda b, g, i: (VT_WIN_BLK + g, b))],
        out_specs=pl.BlockSpec((tq, cols), lambda b, g, i: (b * nq + i, g)),
        out_shape=jax.ShapeDtypeStruct((n, SWA_KV_HEADS * cols), BF16),
        scratch_shapes=[pltpu.VMEM((LANES, heads * tq), BF16)],
        compiler_params=_params(("arbitrary", "arbitrary", "arbitrary")),
        name="window_attn",
    )(sink, proj, proj, vt)


def _diff_kernel(lam_ref, g_ref, q_ref, k_ref, vt_ref, o_ref, qt_sc, s_a, s_b, cm_a, cm_b, m_sc, l_sc, acc_sc,
                 *, tk, seq, lambda_init):
    tq = q_ref.shape[0]
    nk = seq // tk
    q = q_ref[...].astype(F32)
    first = _first_of_pair_mask()
    zero = jnp.zeros_like(q)
    qt_sc[:, :tq] = jnp.where(first, q, zero).T.astype(BF16)
    qt_sc[:, tq:] = jnp.where(first, zero, q).T.astype(BF16)

    m_sc[...] = jnp.full(m_sc.shape, -1e30, F32)
    l_sc[...] = jnp.zeros(l_sc.shape, F32)
    acc_sc[...] = jnp.zeros(acc_sc.shape, F32)

    def scores(t, s_ref, cm_ref):
        off = pl.multiple_of(t * tk, tk)
        s = jnp.dot(k_ref[pl.ds(off, tk), :], qt_sc[...], preferred_element_type=F32)
        s_ref[...] = s
        cm_ref[...] = jnp.max(s, axis=0, keepdims=True)

    def consume(t, s_ref, cm_ref):
        off = pl.multiple_of(t * tk, tk)
        m_prev = m_sc[...]
        m_new = jnp.maximum(m_prev, cm_ref[...])
        alpha = jnp.exp2(m_prev - m_new)
        p = jnp.exp2(s_ref[...] - m_new)
        l_sc[...] = alpha * l_sc[...] + jnp.sum(p, axis=0, keepdims=True)
        vb = vt_ref[:, pl.ds(off, tk)]
        acc_sc[...] = alpha * acc_sc[...] + jnp.dot(vb, p.astype(BF16), preferred_element_type=F32)
        m_sc[...] = m_new

    scores(0, s_a, cm_a)

    def body(u, carry):
        t = 2 * u
        scores(t + 1, s_b, cm_b)
        consume(t, s_a, cm_a)
        scores(t + 2, s_a, cm_a)
        consume(t + 1, s_b, cm_b)
        return carry

    lax.fori_loop(0, nk // 2 - 1, body, 0)
    scores(nk - 1, s_b, cm_b)
    consume(nk - 2, s_a, cm_a)
    consume(nk - 1, s_b, cm_b)

    lam1 = jnp.sum(lam_ref[0:1, :] * lam_ref[1:2, :], axis=-1, keepdims=True)
    lam2 = jnp.sum(lam_ref[2:3, :] * lam_ref[3:4, :], axis=-1, keepdims=True)
    lam = jnp.exp(lam1) - jnp.exp(lam2) + lambda_init
    o = acc_sc[...] / l_sc[...]
    o = o[:, :tq] - lam * o[:, tq:]
    y = o * lax.rsqrt(jnp.mean(o * o, axis=0, keepdims=True) + EPS)
    y = y.T * (g_ref[...] * (1.0 - lambda_init))
    o_ref[...] = y.astype(o_ref.dtype)


def _diff_attention(proj, vt, lam_vecs, subln_g, batch, seq, lambda_init):
    n = proj.shape[0]
    tq, tk = TQ_DIFF, TK_DIFF
    nq = seq // tq
    assert seq % (2 * tk) == 0 and seq % tq == 0, "the pipelined key loop consumes key blocks in pairs"
    kern = functools.partial(_diff_kernel, tk=tk, seq=seq, lambda_init=lambda_init)
    return pl.pallas_call(
        kern,
        grid=(batch, DIFF_HEADS, nq),
        in_specs=[pl.BlockSpec((4, HEAD_DIM), lambda b, h, i: (0, 0)),
                  pl.BlockSpec((1, LANES), lambda b, h, i: (0, 0)),
                  pl.BlockSpec((tq, LANES), lambda b, h, i: (b * nq + i, QD_BLK + h)),
                  pl.BlockSpec((seq, LANES), lambda b, h, i: (b, KD_BLK + h)),
                  pl.BlockSpec((LANES, seq), lambda b, h, i: (h, b))],
        out_specs=pl.BlockSpec((tq, LANES), lambda b, h, i: (b * nq + i, h)),
        out_shape=jax.ShapeDtypeStruct((n, DIFF_HEADS * LANES), BF16),
        scratch_shapes=[pltpu.VMEM((LANES, 2 * tq), BF16),
                        pltpu.VMEM((tk, 2 * tq), F32), pltpu.VMEM((tk, 2 * tq), F32),
                        pltpu.VMEM((1, 2 * tq), F32), pltpu.VMEM((1, 2 * tq), F32),
                        pltpu.VMEM((1, 2 * tq), F32), pltpu.VMEM((1, 2 * tq), F32),
                        pltpu.VMEM((LANES, 2 * tq), F32)],
        compiler_params=_params(("arbitrary", "arbitrary", "arbitrary")),
        name="diff_attn",
    )(lam_vecs, subln_g, proj, proj, vt)


def _route_top2(logits):
    lane = lax.broadcasted_iota(jnp.int32, logits.shape, 1)
    lg = jnp.where(lane < N_EXPERTS, logits, -jnp.inf)
    v1 = jnp.max(lg, axis=-1, keepdims=True)
    i1 = jnp.min(jnp.where(lg == v1, lane, LANES), axis=-1, keepdims=True)
    lg2 = jnp.where(lane == i1, -jnp.inf, lg)
    v2 = jnp.max(lg2, axis=-1, keepdims=True)
    i2 = jnp.min(jnp.where(lg2 == v2, lane, LANES), axis=-1, keepdims=True)
    e = jnp.exp(v2 - v1)
    w1 = 1.0 / (1.0 + e)
    w2 = e / (1.0 + e)
    idx = jnp.where(lane == 0, i1, jnp.where(lane == 1, i2, 0))
    wgt = jnp.where(lane == 0, w1, jnp.where(lane == 1, w2, 0.0))
    return idx, wgt


def _outproj_kernel(ya_ref, yd_ref, w_ref, x_ref, mod_ref, g_ref, *rest, moe):
    if moe:
        rwh_ref, rwl_ref, x1_ref, h_ref, idx_ref, wgt_ref = rest
    else:
        x1_ref, h_ref = rest
    half = ya_ref.shape[1]
    y = (jnp.dot(ya_ref[...], w_ref[:half, :], preferred_element_type=F32)
         + jnp.dot(yd_ref[...], w_ref[half:, :], preferred_element_type=F32))
    x1 = x_ref[...] + mod_ref[0, 2:3, :] * y
    x1_ref[...] = x1
    h = _rms_mod(x1, g_ref[...], mod_ref[0, 4:5, :], mod_ref[0, 3:4, :])
    h_ref[...] = h.astype(h_ref.dtype)
    if moe:
        h_hi = h.astype(BF16)
        h_lo = (h - h_hi.astype(F32)).astype(BF16)
        logits = (jnp.dot(h_hi, rwh_ref[...], preferred_element_type=F32)
                  + jnp.dot(h_hi, rwl_ref[...], preferred_element_type=F32)
                  + jnp.dot(h_lo, rwh_ref[...], preferred_element_type=F32))
        idx, wgt = _route_top2(logits)
        idx_ref[...] = idx
        wgt_ref[...] = wgt


def _outproj(ya, yd, w, layer, x2, mod, g, seq, router_w=None):
    n, d = x2.shape
    tm = TM_PROJ
    per_b = seq // tm
    moe = router_w is not None
    row = lambda i: (i, 0)
    in_specs = [pl.BlockSpec((tm, ya.shape[1]), row),
                pl.BlockSpec((tm, yd.shape[1]), row),
                pl.BlockSpec((None,) + w.shape[1:], lambda i: (layer, 0, 0)),
                pl.BlockSpec((tm, d), row),
                pl.BlockSpec((1, N_MOD, d), lambda i: (i // per_b, 0, 0)),
                pl.BlockSpec((1, d), lambda i: (0, 0))]
    out_specs = [pl.BlockSpec((tm, d), row), pl.BlockSpec((tm, d), row)]
    out_shape = [jax.ShapeDtypeStruct((n, d), F32), jax.ShapeDtypeStruct((n, d), F32 if moe else BF16)]
    args = [ya, yd, w, x2, mod, g]
    if moe:
        in_specs += [pl.BlockSpec((d, LANES), lambda i: (0, 0))] * 2
        out_specs += [pl.BlockSpec((tm, LANES), row)] * 2
        out_shape += [jax.ShapeDtypeStruct((n, LANES), jnp.int32), jax.ShapeDtypeStruct((n, LANES), F32)]
        args += list(router_w)
    return pl.pallas_call(
        functools.partial(_outproj_kernel, moe=moe),
        grid=(n // tm,),
        in_specs=in_specs, out_specs=out_specs, out_shape=out_shape,
        compiler_params=_params(("arbitrary",)),
        name="outproj_moe" if moe else "outproj",
    )(*args)


def _swiglu_step(h, wg_ref, wu_ref, wd_ref):
    g = jnp.dot(h, wg_ref[...], preferred_element_type=F32)
    u = jnp.dot(h, wu_ref[...], preferred_element_type=F32)
    return jnp.dot((_silu(g) * u).astype(BF16), wd_ref[...], preferred_element_type=F32)


def _ffn_kernel(h_ref, wg_ref, wu_ref, wd_ref, x_ref, mod_ref, o_ref, acc_ref):
    j = pl.program_id(1)

    @pl.when(j == 0)
    def _():
        acc_ref[...] = jnp.zeros(acc_ref.shape, F32)

    acc_ref[...] += _swiglu_step(h_ref[...], wg_ref, wu_ref, wd_ref)

    @pl.when(j == pl.num_programs(1) - 1)
    def _():
        o_ref[...] = x_ref[...] + mod_ref[0, 5:6, :] * acc_ref[...]


def _dense_ffn(h, wg, wu, wd, li, x1, mod, seq):
    n, d = x1.shape
    f = wg.shape[2]
    tm, tf = TM_FFN, TF_FFN
    per_b = seq // tm
    return pl.pallas_call(
        _ffn_kernel,
        grid=(n // tm, f // tf),
        in_specs=[pl.BlockSpec((tm, d), lambda i, j: (i, 0)),
                  pl.BlockSpec((None, d, tf), lambda i, j: (li, 0, j)),
                  pl.BlockSpec((None, d, tf), lambda i, j: (li, 0, j)),
                  pl.BlockSpec((None, tf, d), lambda i, j: (li, j, 0)),
                  pl.BlockSpec((tm, d), lambda i, j: (i, 0)),
                  pl.BlockSpec((1, N_MOD, d), lambda i, j: (i // per_b, 0, 0))],
        out_specs=pl.BlockSpec((tm, d), lambda i, j: (i, 0)),
        out_shape=jax.ShapeDtypeStruct((n, d), F32),
        scratch_shapes=[pltpu.VMEM((tm, d), F32)],
        compiler_params=_params(("arbitrary", "arbitrary")),
        name="dense_ffn",
    )(h, wg, wu, wd, x1, mod)


def _moe_plan(idx, tm):
    n = idx.shape[0]
    flat_e = idx.reshape(-1)
    onehot = (flat_e[:, None] == jnp.arange(N_EXPERTS, dtype=jnp.int32)[None, :]).astype(jnp.int32)
    csum = jnp.cumsum(onehot, axis=0)
    rank = jnp.sum((csum - onehot) * onehot, axis=1)
    counts = csum[-1]
    padded = (counts + tm - 1) // tm * tm
    pad_ends = jnp.cumsum(padded)
    pad_starts = pad_ends - padded
    dest = (pad_starts[flat_e] + rank).astype(jnp.int32)
    rows = 2 * n + N_EXPERTS * tm
    flat_tok = jnp.arange(2 * n, dtype=jnp.int32) // 2
    buf_tok = jnp.zeros((rows,), jnp.int32).at[dest].set(flat_tok)
    n_tiles = (pad_ends[-1] // tm).astype(jnp.int32).reshape(1)
    tile_start = jnp.arange(rows // tm, dtype=jnp.int32) * tm
    tile_e = jnp.minimum(jnp.sum((pad_ends[None, :] <= tile_start[:, None]).astype(jnp.int32), axis=1),
                         N_EXPERTS - 1).astype(jnp.int32)
    return buf_tok, dest.reshape(n, 2), tile_e, n_tiles


def _row_gather(idx_of, src_hbm, dst_ref, sem, rows):
    def issue(r, carry):
        pltpu.make_async_copy(src_hbm.at[pl.ds(idx_of(r), 1)], dst_ref.at[pl.ds(r, 1)], sem).start()
        return carry

    lax.fori_loop(0, rows, issue, 0, unroll=8)


def _row_gather_wait(src_hbm, dst_ref, sem, rows):
    pltpu.make_async_copy(src_hbm.at[pl.ds(0, rows)], dst_ref, sem).wait()


def _moe_ffn_kernel(te_ref, nt_ref, tok_ref, h_hbm, wg_ref, wu_ref, wd_ref, o_ref, xbuf, xb16, acc_ref, sem, *, nf):
    i = pl.program_id(0)
    j = pl.program_id(1)
    tm = xb16.shape[0]
    n_tiles = nt_ref[0]
    slot = i % 2
    other = 1 - slot
    nxt = jnp.minimum(i + 1, n_tiles - 1)
    per_step = tm // nf

    def start_next(r):
        tok = tok_ref[nxt * tm + r]
        pltpu.make_async_copy(h_hbm.at[pl.ds(tok, 1)], xbuf.at[other, pl.ds(r, 1)], sem.at[other]).start()

    @pl.when(jnp.logical_and(i == 0, j == 0))
    def _():
        _row_gather(lambda r: tok_ref[r], h_hbm, xbuf.at[0], sem.at[0], tm)

    @pl.when(i < n_tiles)
    def _():
        @pl.when(j == 0)
        def _():
            _row_gather_wait(h_hbm, xbuf.at[slot], sem.at[slot], tm)
            xb16[...] = xbuf[slot].astype(BF16)
            acc_ref[...] = jnp.zeros(acc_ref.shape, F32)
            for r in range(per_step * nf, tm):
                start_next(r)

        for u in range(per_step):
            start_next(j * per_step + u)
        acc_ref[...] += _swiglu_step(xb16[...], wg_ref.at[0], wu_ref.at[0], wd_ref.at[0])

        @pl.when(j == nf - 1)
        def _():
            o_ref[...] = acc_ref[...].astype(o_ref.dtype)

        @pl.when(jnp.logical_and(j == nf - 1, i == n_tiles - 1))
        def _():
            _row_gather_wait(h_hbm, xbuf.at[other], sem.at[other], tm)

    @pl.when(jnp.logical_and(i >= nt_ref[0], j == pl.num_programs(1) - 1))
    def _():
        o_ref[...] = jnp.zeros(o_ref.shape, o_ref.dtype)


def _moe_ffn(h, buf_tok, wg, wu, wd, li, tile_e, n_tiles, tm):
    rows_total = buf_tok.shape[0]
    d = h.shape[1]
    f = wg.shape[3]
    tf = TF_FFN
    nf = f // tf

    def _fj(i, j, nt):
        return jnp.where(i < nt[0], j, nf - 1)

    def up_map(i, j, te, nt, tok):
        return (li, te[jnp.minimum(i, nt[0] - 1)], 0, _fj(i, j, nt))

    def down_map(i, j, te, nt, tok):
        return (li, te[jnp.minimum(i, nt[0] - 1)], _fj(i, j, nt), 0)

    return pl.pallas_call(
        functools.partial(_moe_ffn_kernel, nf=nf),
        grid_spec=pltpu.PrefetchScalarGridSpec(
            num_scalar_prefetch=3,
            grid=(rows_total // tm, nf),
            in_specs=[pl.BlockSpec(memory_space=pl.ANY),
                      pl.BlockSpec((None, 1, d, tf), up_map),
                      pl.BlockSpec((None, 1, d, tf), up_map),
                      pl.BlockSpec((None, 1, tf, d), down_map)],
            out_specs=pl.BlockSpec((tm, d), lambda i, j, te, nt, tok: (i, 0)),
            scratch_shapes=[pltpu.VMEM((2, tm, d), F32), pltpu.VMEM((tm, d), BF16),
                            pltpu.VMEM((tm, d), F32), pltpu.SemaphoreType.DMA((2,))]),
        out_shape=jax.ShapeDtypeStruct((rows_total, d), F32),
        compiler_params=_params(("arbitrary", "arbitrary")),
        name="moe_ffn",
    )(tile_e, n_tiles, buf_tok, h, wg, wu, wd)


def _combine_kernel(pos_ref, ys_hbm, wgt_ref, x_ref, mod_ref, fg_ref, o_ref, buf, sem, *, tm, final):
    i = pl.program_id(0)
    slot = i % 2

    def gather(step, to_slot):
        for k in range(2):
            _row_gather(lambda r: pos_ref[2 * (step * tm + r) + k], ys_hbm, buf.at[to_slot, k],
                        sem.at[to_slot], tm)

    @pl.when(i == 0)
    def _():
        gather(0, 0)

    @pl.when(i + 1 < pl.num_programs(0))
    def _():
        gather(i + 1, 1 - slot)

    for k in range(2):
        _row_gather_wait(ys_hbm, buf.at[slot, k], sem.at[slot], tm)

    wgt = wgt_ref[...]
    y = wgt[:, 0:1] * buf[slot, 0] + wgt[:, 1:2] * buf[slot, 1]
    out = x_ref[...] + mod_ref[0, 5:6, :] * y
    if final:
        out = out * lax.rsqrt(jnp.mean(out * out, axis=-1, keepdims=True) + EPS) * fg_ref[...]
    o_ref[...] = out


def _moe_combine(ys, pos, wgt, x1, mod, final_g, seq, final):
    n, d = x1.shape
    tm = TM_COMB
    per_b = seq // tm
    return pl.pallas_call(
        functools.partial(_combine_kernel, tm=tm, final=final),
        grid_spec=pltpu.PrefetchScalarGridSpec(
            num_scalar_prefetch=1,
            grid=(n // tm,),
            in_specs=[pl.BlockSpec(memory_space=pl.ANY),
                      pl.BlockSpec((tm, LANES), lambda i, p: (i, 0)),
                      pl.BlockSpec((tm, d), lambda i, p: (i, 0)),
                      pl.BlockSpec((1, N_MOD, d), lambda i, p: (i // per_b, 0, 0)),
                      pl.BlockSpec((1, d), lambda i, p: (0, 0))],
            out_specs=pl.BlockSpec((tm, d), lambda i, p: (i, 0)),
            scratch_shapes=[pltpu.VMEM((2, 2, tm, d), F32),
                            pltpu.SemaphoreType.DMA((2,))]),
        out_shape=jax.ShapeDtypeStruct((n, d), F32),
        compiler_params=_params(("arbitrary",)),
        name="moe_combine_final" if final else "moe_combine",
    )(pos.reshape(-1), ys, wgt, x1, mod, final_g)


def _final_norm_kernel(x_ref, g_ref, o_ref):
    x = x_ref[...]
    o_ref[...] = x * lax.rsqrt(jnp.mean(x * x, axis=-1, keepdims=True) + EPS) * g_ref[...]


def _final_norm(x2, g):
    n, d = x2.shape
    tm = TM_PROJ
    return pl.pallas_call(
        _final_norm_kernel,
        grid=(n // tm,),
        in_specs=[pl.BlockSpec((tm, d), lambda i: (i, 0)), pl.BlockSpec((1, d), lambda i: (0, 0))],
        out_specs=pl.BlockSpec((tm, d), lambda i: (i, 0)),
        out_shape=jax.ShapeDtypeStruct((n, d), F32),
        compiler_params=_params(("arbitrary",)),
        name="final_norm",
    )(x2, g)


def kernel(x, c, positions, ada_w, ada_b, norm_mix_g, norm_ffn_g, w_in, w_out, sink, lam_q1, lam_k1, lam_q2, lam_k2, subln_g, ffn_w_gate, ffn_w_up, ffn_w_down, router_w, moe_w_gate, moe_w_up, moe_w_down, final_g):
    batch, seq, d = x.shape
    depth = w_in.shape[0]
    n = batch * seq

    c_pad = jnp.zeros((8, d), F32).at[:batch].set(c)
    mod_all = _adaln(c_pad, ada_w, ada_b)
    cos_t, sin_t = _rope_tables(positions)
    w_in_p = _permute_in_proj(w_in).astype(BF16)
    w_vdt = _value_rows(w_in).astype(BF16)
    w_out_b = w_out.astype(BF16)
    ffn_w = [w.astype(BF16) for w in (ffn_w_gate, ffn_w_up, ffn_w_down)]
    moe_w = [w.astype(BF16) for w in (moe_w_gate, moe_w_up, moe_w_down)]
    final_g2 = final_g.reshape(1, d)

    x2 = x.reshape(n, d)
    for layer in range(depth):
        lambda_init = 0.8 - 0.6 * math.exp(-0.3 * layer)
        mod = mod_all[layer, :batch].reshape(batch, N_MOD, d)
        proj, vt = _inproj(x2, mod, norm_mix_g[layer].reshape(1, d), w_in_p, w_vdt, layer, cos_t, sin_t, seq)
        ya = _window_attention(proj, vt, sink[layer], batch, seq)
        lam_vecs = jnp.stack([lam_q1[layer], lam_k1[layer], lam_q2[layer], lam_k2[layer]])
        yd = _diff_attention(proj, vt, lam_vecs, subln_g[layer].reshape(1, LANES), batch, seq, lambda_init)
        i = layer // 2
        g_ffn = norm_ffn_g[layer].reshape(1, d)
        if layer % 2 == 0:
            x1, h = _outproj(ya, yd, w_out_b, layer, x2, mod, g_ffn, seq)
            x2 = _dense_ffn(h, *ffn_w, i, x1, mod, seq)
        else:
            rw = jnp.zeros((d, LANES), F32).at[:, :N_EXPERTS].set(router_w[i])
            rw_hi = rw.astype(BF16)
            rw_lo = (rw - rw_hi.astype(F32)).astype(BF16)
            x1, h, idx, wgt = _outproj(ya, yd, w_out_b, layer, x2, mod, g_ffn, seq, router_w=(rw_hi, rw_lo))
            buf_tok, pos, tile_e, n_tiles = _moe_plan(idx[:, :2], TM_MOE)
            ys = _moe_ffn(h, buf_tok, *moe_w, i, tile_e, n_tiles, TM_MOE)
            final = layer == depth - 1
            x2 = _moe_combine(ys, pos, wgt, x1, mod, final_g2, seq, final)
    if depth % 2 == 1:
        x2 = _final_norm(x2, final_g2)
    return x2.reshape(batch, seq, d)
```

```python
import functools
import math

import jax
import jax.numpy as jnp
from jax import lax
from jax.experimental import pallas as pl
from jax.experimental.pallas import tpu as pltpu

F32 = jnp.float32
BF16 = jnp.bfloat16

HEAD_DIM = 64
HALF_DIM = HEAD_DIM // 2
ROPE_THETA = 10000.0
EPS = 1e-6
SWA_Q_HEADS = 8
SWA_KV_HEADS = 2
WINDOW = 128
DIFF_HEADS = 4
N_EXPERTS = 8
N_MOD = 6
LANES = 128
VMEM_LIMIT = 56 * 1024 * 1024

QA_BLK, KA_BLK, QD_BLK, KD_BLK = 0, 4, 6, 10
ROPE_BLKS = 14
PROJ_COLS = ROPE_BLKS * LANES
VA_COL0 = 640
VD_COL0 = 1792
VT_WIN_BLK = 4
Q_SCALE = HEAD_DIM ** -0.5
LOG2E = math.log2(math.e)

TM_PROJ = 512
TQ_WIN = 256
TQ_DIFF = 512
TK_DIFF = 1024
TM_FFN = 1024
TF_FFN = 512
TM_MOE = 1024
TM_COMB = 256


def _permute_in_proj(w_in):
    depth, d, _ = w_in.shape
    qa, ka, qd, kd = 0, 512, 768, 1280

    def pairs(lo, n_pairs):
        t = w_in[:, :, lo:lo + n_pairs * LANES].reshape(depth, d, n_pairs, 2, 2, HALF_DIM)
        return t.transpose(0, 1, 2, 4, 3, 5).reshape(depth, d, n_pairs * LANES)

    def dup_halves(lo):
        t = w_in[:, :, lo:lo + SWA_KV_HEADS * HEAD_DIM].reshape(depth, d, SWA_KV_HEADS, 2, 1, HALF_DIM)
        return jnp.broadcast_to(t, (depth, d, SWA_KV_HEADS, 2, 2, HALF_DIM)).reshape(depth, d, SWA_KV_HEADS * LANES)

    return jnp.concatenate([pairs(qa, 4), dup_halves(ka), pairs(qd, DIFF_HEADS), pairs(kd, DIFF_HEADS)], axis=-1)


def _value_rows(w_in):
    depth, d, _ = w_in.shape
    vd = jnp.swapaxes(w_in[:, :, VD_COL0:], 1, 2)
    va = jnp.swapaxes(w_in[:, :, VA_COL0:VA_COL0 + SWA_KV_HEADS * HEAD_DIM], 1, 2)
    va = jnp.broadcast_to(va.reshape(depth, SWA_KV_HEADS, 1, HEAD_DIM, d), (depth, SWA_KV_HEADS, 2, HEAD_DIM, d))
    return jnp.concatenate([vd, va.reshape(depth, SWA_KV_HEADS * LANES, d)], axis=1)


def _first_of_pair_mask():
    lane = lax.broadcasted_iota(jnp.int32, (1, LANES), 1)
    return (lane // HALF_DIM) % 2 == 0


def _silu(g):
    return g / (1.0 + jnp.exp(-g))


def _params(sem):
    return pltpu.CompilerParams(dimension_semantics=sem, vmem_limit_bytes=VMEM_LIMIT)


def _adaln_kernel(c_ref, w_ref, b_ref, o_ref):
    a = _silu(c_ref[...])
    o_ref[0] = jnp.dot(a, w_ref[0], preferred_element_type=F32,
                       precision=lax.Precision.HIGHEST) + b_ref[0]


def _adaln(c_pad, ada_w, ada_b):
    depth, d, cols = ada_w.shape
    tn = 1536
    return pl.pallas_call(
        _adaln_kernel,
        grid=(depth, cols // tn),
        in_specs=[pl.BlockSpec((8, d), lambda l, j: (0, 0)),
                  pl.BlockSpec((1, d, tn), lambda l, j: (l, 0, j)),
                  pl.BlockSpec((1, 1, tn), lambda l, j: (l, 0, j))],
        out_specs=pl.BlockSpec((1, 8, tn), lambda l, j: (l, 0, j)),
        out_shape=jax.ShapeDtypeStruct((depth, 8, cols), F32),
        compiler_params=_params(("arbitrary", "arbitrary")),
        name="adaln",
    )(c_pad, ada_w, ada_b.reshape(depth, 1, cols))


def _rope_table_kernel(pos_ref, cos_ref, sin_ref):
    lane = lax.broadcasted_iota(jnp.int32, (1, LANES), 1)
    freq = (lane % HALF_DIM).astype(F32)
    inv = 1.0 / (ROPE_THETA ** (2.0 * freq / HEAD_DIM))
    ang = pos_ref[...].astype(F32) * inv
    sign = jnp.where(lane < HEAD_DIM, -1.0, 1.0)
    cos_ref[...] = jnp.cos(ang)
    sin_ref[...] = jnp.sin(ang) * sign


def _rope_tables(positions):
    n = positions.size
    tm = 1024
    pos = positions.reshape(n, 1)
    return pl.pallas_call(
        _rope_table_kernel,
        grid=(n // tm,),
        in_specs=[pl.BlockSpec((tm, 1), lambda i: (i, 0))],
        out_specs=[pl.BlockSpec((tm, LANES), lambda i: (i, 0))] * 2,
        out_shape=[jax.ShapeDtypeStruct((n, LANES), F32)] * 2,
        compiler_params=_params(("arbitrary",)),
        name="rope_tables",
    )(pos)


def _rms_mod(x, g, scale, shift):
    y = x * lax.rsqrt(jnp.mean(x * x, axis=-1, keepdims=True) + EPS) * g
    return y * (1.0 + scale) + shift


def _inproj_kernel(x_ref, mod_ref, g_ref, w_ref, wvt_ref, cos_ref, sin_ref, o_ref, vt_ref):
    h = _rms_mod(x_ref[...], g_ref[...], mod_ref[0, 1:2, :], mod_ref[0, 0:1, :]).astype(BF16)
    proj = jnp.dot(h, w_ref[...], preferred_element_type=F32)
    vt_ref[...] = lax.dot_general(wvt_ref[...], h, (((1,), (1,)), ((), ())),
                                  preferred_element_type=F32).astype(BF16)
    cos = cos_ref[...]
    sin = sin_ref[...]
    for blk in range(ROPE_BLKS):
        t = proj[:, blk * LANES:(blk + 1) * LANES]
        r = t * cos + pltpu.roll(t, HEAD_DIM, axis=1) * sin
        if blk < KA_BLK or QD_BLK <= blk < KD_BLK:
            r = r * (Q_SCALE * LOG2E)
        o_ref[:, blk * LANES:(blk + 1) * LANES] = r.astype(BF16)


def _inproj(x2, mod, g, w, wvt, layer, cos_t, sin_t, seq):
    n, d = x2.shape
    tm = TM_PROJ
    per_b = seq // tm
    vt_rows = wvt.shape[1]
    return pl.pallas_call(
        _inproj_kernel,
        grid=(n // tm,),
        in_specs=[pl.BlockSpec((tm, d), lambda i: (i, 0)),
                  pl.BlockSpec((1, N_MOD, d), lambda i: (i // per_b, 0, 0)),
                  pl.BlockSpec((1, d), lambda i: (0, 0)),
                  pl.BlockSpec((None, d, PROJ_COLS), lambda i: (layer, 0, 0)),
                  pl.BlockSpec((None, vt_rows, d), lambda i: (layer, 0, 0)),
                  pl.BlockSpec((tm, LANES), lambda i: (i, 0)),
                  pl.BlockSpec((tm, LANES), lambda i: (i, 0))],
        out_specs=[pl.BlockSpec((tm, PROJ_COLS), lambda i: (i, 0)),
                   pl.BlockSpec((vt_rows, tm), lambda i: (0, i))],
        out_shape=[jax.ShapeDtypeStruct((n, PROJ_COLS), BF16),
                   jax.ShapeDtypeStruct((vt_rows, n), BF16)],
        compiler_params=_params(("arbitrary",)),
        name="inproj",
    )(x2, mod, g, w, wvt, cos_t, sin_t)


def _win_kernel(sink_ref, q_ref, k_ref, vt_ref, o_ref, qt_sc, *, tq, seq):
    g = pl.program_id(1)
    i = pl.program_id(2)
    heads = SWA_Q_HEADS // SWA_KV_HEADS
    width = tq + 2 * WINDOW
    start = pl.multiple_of(jnp.clip(i * tq - WINDOW, 0, seq - width), LANES)
    first = _first_of_pair_mask()
    for blk in range(heads // 2):
        q = q_ref[:, blk * LANES:(blk + 1) * LANES].astype(F32)
        zero = jnp.zeros_like(q)
        qt_sc[:, (2 * blk) * tq:(2 * blk + 1) * tq] = jnp.where(first, q, zero).T.astype(BF16)
        qt_sc[:, (2 * blk + 1) * tq:(2 * blk + 2) * tq] = jnp.where(first, zero, q).T.astype(BF16)

    s = jnp.dot(k_ref[pl.ds(start, width), :], qt_sc[...], preferred_element_type=F32)
    kpos = start + lax.broadcasted_iota(jnp.int32, (width, tq), 0)
    qpos = i * tq + lax.broadcasted_iota(jnp.int32, (width, tq), 1)
    valid = jnp.abs(qpos - kpos) <= WINDOW
    s = jnp.concatenate([jnp.where(valid, s[:, c * tq:(c + 1) * tq], -1e30) for c in range(heads)], axis=1)
    col = lax.broadcasted_iota(jnp.int32, (1, heads * tq), 1)
    sk = jnp.zeros((1, heads * tq), F32)
    for c in range(heads):
        sk = jnp.where(col // tq == c, sink_ref[heads * g + c] * LOG2E, sk)
    m = jnp.maximum(jnp.max(s, axis=0, keepdims=True), sk)
    p = jnp.exp2(s - m)
    den = jnp.sum(p, axis=0, keepdims=True) + jnp.exp2(sk - m)
    o = jnp.dot(vt_ref[:, pl.ds(start, width)], p.astype(BF16), preferred_element_type=F32) / den
    lane = lax.broadcasted_iota(jnp.int32, (1, LANES), 1)
    for blk in range(heads // 2):
        oa = o[:, (2 * blk) * tq:(2 * blk + 1) * tq].T
        ob = o[:, (2 * blk + 1) * tq:(2 * blk + 2) * tq].T
        o_ref[:, blk * LANES:(blk + 1) * LANES] = jnp.where(lane < HEAD_DIM, oa, ob).astype(o_ref.dtype)


def _window_attention(proj, vt, sink, batch, seq):
    n = proj.shape[0]
    tq = TQ_WIN
    nq = seq // tq
    heads = SWA_Q_HEADS // SWA_KV_HEADS
    cols = heads // 2 * LANES
    kern = functools.partial(_win_kernel, tq=tq, seq=seq)
    return pl.pallas_call(
        kern,
        grid=(batch, SWA_KV_HEADS, nq),
        in_specs=[pl.BlockSpec(memory_space=pltpu.SMEM),
                  pl.BlockSpec((tq, cols), lambda b, g, i: (b * nq + i, g)),
                  pl.BlockSpec((seq, LANES), lambda b, g, i: (b, KA_BLK + g)),
                  pl.BlockSpec((LANES, seq), lambda b, g, i: (VT_WIN_BLK + g, b))],
        out_specs=pl.BlockSpec((tq, cols), lambda b, g, i: (b * nq + i, g)),
        out_shape=jax.ShapeDtypeStruct((n, SWA_KV_HEADS * cols), BF16),
        scratch_shapes=[pltpu.VMEM((LANES, heads * tq), BF16)],
        compiler_params=_params(("arbitrary", "arbitrary", "arbitrary")),
        name="window_attn",
    )(sink, proj, proj, vt)


def _diff_kernel(lam_ref, g_ref, q_ref, k_ref, vt_ref, o_ref, qt_sc, s_a, s_b, cm_a, cm_b, m_sc, l_sc, acc_sc,
                 *, tk, seq, lambda_init):
    tq = q_ref.shape[0]
    nk = seq // tk
    q = q_ref[...].astype(F32)
    first = _first_of_pair_mask()
    zero = jnp.zeros_like(q)
    qt_sc[:, :tq] = jnp.where(first, q, zero).T.astype(BF16)
    qt_sc[:, tq:] = jnp.where(first, zero, q).T.astype(BF16)

    m_sc[...] = jnp.full(m_sc.shape, -1e30, F32)
    l_sc[...] = jnp.zeros(l_sc.shape, F32)
    acc_sc[...] = jnp.zeros(acc_sc.shape, F32)

    def scores(t, s_ref, cm_ref):
        off = pl.multiple_of(t * tk, tk)
        s = jnp.dot(k_ref[pl.ds(off, tk), :], qt_sc[...], preferred_element_type=F32)
        s_ref[...] = s
        cm_ref[...] = jnp.max(s, axis=0, keepdims=True)

    def consume(t, s_ref, cm_ref):
        off = pl.multiple_of(t * tk, tk)
        m_prev = m_sc[...]
        m_new = jnp.maximum(m_prev, cm_ref[...])
        alpha = jnp.exp2(m_prev - m_new)
        p = jnp.exp2(s_ref[...] - m_new)
        l_sc[...] = alpha * l_sc[...] + jnp.sum(p, axis=0, keepdims=True)
        vb = vt_ref[:, pl.ds(off, tk)]
        acc_sc[...] = alpha * acc_sc[...] + jnp.dot(vb, p.astype(BF16), preferred_element_type=F32)
        m_sc[...] = m_new

    scores(0, s_a, cm_a)

    def body(u, carry):
        t = 2 * u
        scores(t + 1, s_b, cm_b)
        consume(t, s_a, cm_a)
        scores(t + 2, s_a, cm_a)
        consume(t + 1, s_b, cm_b)
        return carry

    lax.fori_loop(0, nk // 2 - 1, body, 0)
    scores(nk - 1, s_b, cm_b)
    consume(nk - 2, s_a, cm_a)
    consume(nk - 1, s_b, cm_b)

    lam1 = jnp.sum(lam_ref[0:1, :] * lam_ref[1:2, :], axis=-1, keepdims=True)
    lam2 = jnp.sum(lam_ref[2:3, :] * lam_ref[3:4, :], axis=-1, keepdims=True)
    lam = jnp.exp(lam1) - jnp.exp(lam2) + lambda_init
    o = acc_sc[...] / l_sc[...]
    o = o[:, :tq] - lam * o[:, tq:]
    y = o * lax.rsqrt(jnp.mean(o * o, axis=0, keepdims=True) + EPS)
    y = y.T * (g_ref[...] * (1.0 - lambda_init))
    o_ref[...] = y.astype(o_ref.dtype)


def _diff_attention(proj, vt, lam_vecs, subln_g, batch, seq, lambda_init):
    n = proj.shape[0]
    tq, tk = TQ_DIFF, TK_DIFF
    nq = seq // tq
    assert seq % (2 * tk) == 0 and seq % tq == 0, "the pipelined key loop consumes key blocks in pairs"
    kern = functools.partial(_diff_kernel, tk=tk, seq=seq, lambda_init=lambda_init)
    return pl.pallas_call(
        kern,
        grid=(batch, DIFF_HEADS, nq),
        in_specs=[pl.BlockSpec((4, HEAD_DIM), lambda b, h, i: (0, 0)),
                  pl.BlockSpec((1, LANES), lambda b, h, i: (0, 0)),
                  pl.BlockSpec((tq, LANES), lambda b, h, i: (b * nq + i, QD_BLK + h)),
                  pl.BlockSpec((seq, LANES), lambda b, h, i: (b, KD_BLK + h)),
                  pl.BlockSpec((LANES, seq), lambda b, h, i: (h, b))],
        out_specs=pl.BlockSpec((tq, LANES), lambda b, h, i: (b * nq + i, h)),
        out_shape=jax.ShapeDtypeStruct((n, DIFF_HEADS * LANES), BF16),
        scratch_shapes=[pltpu.VMEM((LANES, 2 * tq), BF16),
                        pltpu.VMEM((tk, 2 * tq), F32), pltpu.VMEM((tk, 2 * tq), F32),
                        pltpu.VMEM((1, 2 * tq), F32), pltpu.VMEM((1, 2 * tq), F32),
                        pltpu.VMEM((1, 2 * tq), F32), pltpu.VMEM((1, 2 * tq), F32),
                        pltpu.VMEM((LANES, 2 * tq), F32)],
        compiler_params=_params(("arbitrary", "arbitrary", "arbitrary")),
        name="diff_attn",
    )(lam_vecs, subln_g, proj, proj, vt)


def _route_top2(logits):
    lane = lax.broadcasted_iota(jnp.int32, logits.shape, 1)
    lg = jnp.where(lane < N_EXPERTS, logits, -jnp.inf)
    v1 = jnp.max(lg, axis=-1, keepdims=True)
    i1 = jnp.min(jnp.where(lg == v1, lane, LANES), axis=-1, keepdims=True)
    lg2 = jnp.where(lane == i1, -jnp.inf, lg)
    v2 = jnp.max(lg2, axis=-1, keepdims=True)
    i2 = jnp.min(jnp.where(lg2 == v2, lane, LANES), axis=-1, keepdims=True)
    e = jnp.exp(v2 - v1)
    w1 = 1.0 / (1.0 + e)
    w2 = e / (1.0 + e)
    idx = jnp.where(lane == 0, i1, jnp.where(lane == 1, i2, 0))
    wgt = jnp.where(lane == 0, w1, jnp.where(lane == 1, w2, 0.0))
    return idx, wgt


def _outproj_kernel(ya_ref, yd_ref, w_ref, x_ref, mod_ref, g_ref, *rest, moe):
    if moe:
        rwh_ref, rwl_ref, x1_ref, h_ref, idx_ref, wgt_ref = rest
    else:
        x1_ref, h_ref = rest
    half = ya_ref.shape[1]
    y = (jnp.dot(ya_ref[...], w_ref[:half, :], preferred_element_type=F32)
         + jnp.dot(yd_ref[...], w_ref[half:, :], preferred_element_type=F32))
    x1 = x_ref[...] + mod_ref[0, 2:3, :] * y
    x1_ref[...] = x1
    h = _rms_mod(x1, g_ref[...], mod_ref[0, 4:5, :], mod_ref[0, 3:4, :])
    h_ref[...] = h.astype(h_ref.dtype)
    if moe:
        h_hi = h.astype(BF16)
        h_lo = (h - h_hi.astype(F32)).astype(BF16)
        logits = (jnp.dot(h_hi, rwh_ref[...], preferred_element_type=F32)
                  + jnp.dot(h_hi, rwl_ref[...], preferred_element_type=F32)
                  + jnp.dot(h_lo, rwh_ref[...], preferred_element_type=F32))
        idx, wgt = _route_top2(logits)
        idx_ref[...] = idx
        wgt_ref[...] = wgt


def _outproj(ya, yd, w, layer, x2, mod, g, seq, router_w=None):
    n, d = x2.shape
    tm = TM_PROJ
    per_b = seq // tm
    moe = router_w is not None
    row = lambda i: (i, 0)
    in_specs = [pl.BlockSpec((tm, ya.shape[1]), row),
                pl.BlockSpec((tm, yd.shape[1]), row),
                pl.BlockSpec((None,) + w.shape[1:], lambda i: (layer, 0, 0)),
                pl.BlockSpec((tm, d), row),
                pl.BlockSpec((1, N_MOD, d), lambda i: (i // per_b, 0, 0)),
                pl.BlockSpec((1, d), lambda i: (0, 0))]
    out_specs = [pl.BlockSpec((tm, d), row), pl.BlockSpec((tm, d), row)]
    out_shape = [jax.ShapeDtypeStruct((n, d), F32), jax.ShapeDtypeStruct((n, d), F32 if moe else BF16)]
    args = [ya, yd, w, x2, mod, g]
    if moe:
        in_specs += [pl.BlockSpec((d, LANES), lambda i: (0, 0))] * 2
        out_specs += [pl.BlockSpec((tm, LANES), row)] * 2
        out_shape += [jax.ShapeDtypeStruct((n, LANES), jnp.int32), jax.ShapeDtypeStruct((n, LANES), F32)]
        args += list(router_w)
    return pl.pallas_call(
        functools.partial(_outproj_kernel, moe=moe),
        grid=(n // tm,),
        in_specs=in_specs, out_specs=out_specs, out_shape=out_shape,
        compiler_params=_params(("arbitrary",)),
        name="outproj_moe" if moe else "outproj",
    )(*args)


def _swiglu_step(h, wg_ref, wu_ref, wd_ref):
    g = jnp.dot(h, wg_ref[...], preferred_element_type=F32)
    u = jnp.dot(h, wu_ref[...], preferred_element_type=F32)
    return jnp.dot((_silu(g) * u).astype(BF16), wd_ref[...], preferred_element_type=F32)


def _ffn_kernel(h_ref, wg_ref, wu_ref, wd_ref, x_ref, mod_ref, o_ref, acc_ref):
    j = pl.program_id(1)

    @pl.when(j == 0)
    def _():
        acc_ref[...] = jnp.zeros(acc_ref.shape, F32)

    acc_ref[...] += _swiglu_step(h_ref[...], wg_ref, wu_ref, wd_ref)

    @pl.when(j == pl.num_programs(1) - 1)
    def _():
        o_ref[...] = x_ref[...] + mod_ref[0, 5:6, :] * acc_ref[...]


def _dense_ffn(h, wg, wu, wd, li, x1, mod, seq):
    n, d = x1.shape
    f = wg.shape[2]
    tm, tf = TM_FFN, TF_FFN
    per_b = seq // tm
    return pl.pallas_call(
        _ffn_kernel,
        grid=(n // tm, f // tf),
        in_specs=[pl.BlockSpec((tm, d), lambda i, j: (i, 0)),
                  pl.BlockSpec((None, d, tf), lambda i, j: (li, 0, j)),
                  pl.BlockSpec((None, d, tf), lambda i, j: (li, 0, j)),
                  pl.BlockSpec((None, tf, d), lambda i, j: (li, j, 0)),
                  pl.BlockSpec((tm, d), lambda i, j: (i, 0)),
                  pl.BlockSpec((1, N_MOD, d), lambda i, j: (i // per_b, 0, 0))],
        out_specs=pl.BlockSpec((tm, d), lambda i, j: (i, 0)),
        out_shape=jax.ShapeDtypeStruct((n, d), F32),
        scratch_shapes=[pltpu.VMEM((tm, d), F32)],
        compiler_params=_params(("arbitrary", "arbitrary")),
        name="dense_ffn",
    )(h, wg, wu, wd, x1, mod)


def _moe_plan(idx, tm):
    n = idx.shape[0]
    flat_e = idx.reshape(-1)
    onehot = (flat_e[:, None] == jnp.arange(N_EXPERTS, dtype=jnp.int32)[None, :]).astype(jnp.int32)
    csum = jnp.cumsum(onehot, axis=0)
    rank = jnp.sum((csum - onehot) * onehot, axis=1)
    counts = csum[-1]
    padded = (counts + tm - 1) // tm * tm
    pad_ends = jnp.cumsum(padded)
    pad_starts = pad_ends - padded
    dest = (pad_starts[flat_e] + rank).astype(jnp.int32)
    rows = 2 * n + N_EXPERTS * tm
    flat_tok = jnp.arange(2 * n, dtype=jnp.int32) // 2
    buf_tok = jnp.zeros((rows,), jnp.int32).at[dest].set(flat_tok)
    n_tiles = (pad_ends[-1] // tm).astype(jnp.int32).reshape(1)
    tile_start = jnp.arange(rows // tm, dtype=jnp.int32) * tm
    tile_e = jnp.minimum(jnp.sum((pad_ends[None, :] <= tile_start[:, None]).astype(jnp.int32), axis=1),
                         N_EXPERTS - 1).astype(jnp.int32)
    return buf_tok, dest.reshape(n, 2), tile_e, n_tiles


def _row_gather(idx_of, src_hbm, dst_ref, sem, rows):
    def issue(r, carry):
        pltpu.make_async_copy(src_hbm.at[pl.ds(idx_of(r), 1)], dst_ref.at[pl.ds(r, 1)], sem).start()
        return carry

    lax.fori_loop(0, rows, issue, 0, unroll=8)


def _row_gather_wait(src_hbm, dst_ref, sem, rows):
    pltpu.make_async_copy(src_hbm.at[pl.ds(0, rows)], dst_ref, sem).wait()


def _moe_ffn_kernel(te_ref, nt_ref, tok_ref, h_hbm, wg_ref, wu_ref, wd_ref, o_ref, xbuf, xb16, acc_ref, sem, *, nf):
    i = pl.program_id(0)
    j = pl.program_id(1)
    tm = xb16.shape[0]
    n_tiles = nt_ref[0]
    slot = i % 2
    other = 1 - slot
    nxt = jnp.minimum(i + 1, n_tiles - 1)
    per_step = tm // nf

    def start_next(r):
        tok = tok_ref[nxt * tm + r]
        pltpu.make_async_copy(h_hbm.at[pl.ds(tok, 1)], xbuf.at[other, pl.ds(r, 1)], sem.at[other]).start()

    @pl.when(jnp.logical_and(i == 0, j == 0))
    def _():
        _row_gather(lambda r: tok_ref[r], h_hbm, xbuf.at[0], sem.at[0], tm)

    @pl.when(i < n_tiles)
    def _():
        @pl.when(j == 0)
        def _():
            _row_gather_wait(h_hbm, xbuf.at[slot], sem.at[slot], tm)
            xb16[...] = xbuf[slot].astype(BF16)
            acc_ref[...] = jnp.zeros(acc_ref.shape, F32)
            for r in range(per_step * nf, tm):
                start_next(r)

        for u in range(per_step):
            start_next(j * per_step + u)
        acc_ref[...] += _swiglu_step(xb16[...], wg_ref.at[0], wu_ref.at[0], wd_ref.at[0])

        @pl.when(j == nf - 1)
        def _():
            o_ref[...] = acc_ref[...].astype(o_ref.dtype)

        @pl.when(jnp.logical_and(j == nf - 1, i == n_tiles - 1))
        def _():
            _row_gather_wait(h_hbm, xbuf.at[other], sem.at[other], tm)

    @pl.when(jnp.logical_and(i >= nt_ref[0], j == pl.num_programs(1) - 1))
    def _():
        o_ref[...] = jnp.zeros(o_ref.shape, o_ref.dtype)


def _moe_ffn(h, buf_tok, wg, wu, wd, li, tile_e, n_tiles, tm):
    rows_total = buf_tok.shape[0]
    d = h.shape[1]
    f = wg.shape[3]
    tf = TF_FFN
    nf = f // tf

    def _fj(i, j, nt):
        return jnp.where(i < nt[0], j, nf - 1)

    def up_map(i, j, te, nt, tok):
        return (li, te[jnp.minimum(i, nt[0] - 1)], 0, _fj(i, j, nt))

    def down_map(i, j, te, nt, tok):
        return (li, te[jnp.minimum(i, nt[0] - 1)], _fj(i, j, nt), 0)

    return pl.pallas_call(
        functools.partial(_moe_ffn_kernel, nf=nf),
        grid_spec=pltpu.PrefetchScalarGridSpec(
            num_scalar_prefetch=3,
            grid=(rows_total // tm, nf),
            in_specs=[pl.BlockSpec(memory_space=pl.ANY),
                      pl.BlockSpec((None, 1, d, tf), up_map),
                      pl.BlockSpec((None, 1, d, tf), up_map),
                      pl.BlockSpec((None, 1, tf, d), down_map)],
            out_specs=pl.BlockSpec((tm, d), lambda i, j, te, nt, tok: (i, 0)),
            scratch_shapes=[pltpu.VMEM((2, tm, d), F32), pltpu.VMEM((tm, d), BF16),
                            pltpu.VMEM((tm, d), F32), pltpu.SemaphoreType.DMA((2,))]),
        out_shape=jax.ShapeDtypeStruct((rows_total, d), F32),
        compiler_params=_params(("arbitrary", "arbitrary")),
        name="moe_ffn",
    )(tile_e, n_tiles, buf_tok, h, wg, wu, wd)


def _combine_kernel(pos_ref, ys_hbm, wgt_ref, x_ref, mod_ref, fg_ref, o_ref, buf, sem, *, tm, final):
    i = pl.program_id(0)
    slot = i % 2

    def gather(step, to_slot):
        for k in range(2):
            _row_gather(lambda r: pos_ref[2 * (step * tm + r) + k], ys_hbm, buf.at[to_slot, k],
                        sem.at[to_slot], tm)

    @pl.when(i == 0)
    def _():
        gather(0, 0)

    @pl.when(i + 1 < pl.num_programs(0))
    def _():
        gather(i + 1, 1 - slot)

    for k in range(2):
        _row_gather_wait(ys_hbm, buf.at[slot, k], sem.at[slot], tm)

    wgt = wgt_ref[...]
    y = wgt[:, 0:1] * buf[slot, 0] + wgt[:, 1:2] * buf[slot, 1]
    out = x_ref[...] + mod_ref[0, 5:6, :] * y
    if final:
        out = out * lax.rsqrt(jnp.mean(out * out, axis=-1, keepdims=True) + EPS) * fg_ref[...]
    o_ref[...] = out


def _moe_combine(ys, pos, wgt, x1, mod, final_g, seq, final):
    n, d = x1.shape
    tm = TM_COMB
    per_b = seq // tm
    return pl.pallas_call(
        functools.partial(_combine_kernel, tm=tm, final=final),
        grid_spec=pltpu.PrefetchScalarGridSpec(
            num_scalar_prefetch=1,
            grid=(n // tm,),
            in_specs=[pl.BlockSpec(memory_space=pl.ANY),
                      pl.BlockSpec((tm, LANES), lambda i, p: (i, 0)),
                      pl.BlockSpec((tm, d), lambda i, p: (i, 0)),
                      pl.BlockSpec((1, N_MOD, d), lambda i, p: (i // per_b, 0, 0)),
                      pl.BlockSpec((1, d), lambda i, p: (0, 0))],
            out_specs=pl.BlockSpec((tm, d), lambda i, p: (i, 0)),
            scratch_shapes=[pltpu.VMEM((2, 2, tm, d), F32),
                            pltpu.SemaphoreType.DMA((2,))]),
        out_shape=jax.ShapeDtypeStruct((n, d), F32),
        compiler_params=_params(("arbitrary",)),
        name="moe_combine_final" if final else "moe_combine",
    )(pos.reshape(-1), ys, wgt, x1, mod, final_g)


def _final_norm_kernel(x_ref, g_ref, o_ref):
    x = x_ref[...]
    o_ref[...] = x * lax.rsqrt(jnp.mean(x * x, axis=-1, keepdims=True) + EPS) * g_ref[...]


def _final_norm(x2, g):
    n, d = x2.shape
    tm = TM_PROJ
    return pl.pallas_call(
        _final_norm_kernel,
        grid=(n // tm,),
        in_specs=[pl.BlockSpec((tm, d), lambda i: (i, 0)), pl.BlockSpec((1, d), lambda i: (0, 0))],
        out_specs=pl.BlockSpec((tm, d), lambda i: (i, 0)),
        out_shape=jax.ShapeDtypeStruct((n, d), F32),
        compiler_params=_params(("arbitrary",)),
        name="final_norm",
    )(x2, g)


def kernel(x, c, positions, ada_w, ada_b, norm_mix_g, norm_ffn_g, w_in, w_out, sink, lam_q1, lam_k1, lam_q2, lam_k2, subln_g, ffn_w_gate, ffn_w_up, ffn_w_down, router_w, moe_w_gate, moe_w_up, moe_w_down, final_g):
    batch, seq, d = x.shape
    depth = w_in.shape[0]
    n = batch * seq

    c_pad = jnp.zeros((8, d), F32).at[:batch].set(c)
    mod_all = _adaln(c_pad, ada_w, ada_b)
    cos_t, sin_t = _rope_tables(positions)
    w_in_p = _permute_in_proj(w_in).astype(BF16)
    w_vdt = _value_rows(w_in).astype(BF16)
    w_out_b = w_out.astype(BF16)
    ffn_w = [w.astype(BF16) for w in (ffn_w_gate, ffn_w_up, ffn_w_down)]
    moe_w = [w.astype(BF16) for w in (moe_w_gate, moe_w_up, moe_w_down)]
    final_g2 = final_g.reshape(1, d)

    x2 = x.reshape(n, d)
    for layer in range(depth):
        lambda_init = 0.8 - 0.6 * math.exp(-0.3 * layer)
        mod = mod_all[layer, :batch].reshape(batch, N_MOD, d)
        proj, vt = _inproj(x2, mod, norm_mix_g[layer].reshape(1, d), w_in_p, w_vdt, layer, cos_t, sin_t, seq)
        ya = _window_attention(proj, vt, sink[layer], batch, seq)
        lam_vecs = jnp.stack([lam_q1[layer], lam_k1[layer], lam_q2[layer], lam_k2[layer]])
        yd = _diff_attention(proj, vt, lam_vecs, subln_g[layer].reshape(1, LANES), batch, seq, lambda_init)
        i = layer // 2
        g_ffn = norm_ffn_g[layer].reshape(1, d)
        if layer % 2 == 0:
            x1, h = _outproj(ya, yd, w_out_b, layer, x2, mod, g_ffn, seq)
            x2 = _dense_ffn(h, *ffn_w, i, x1, mod, seq)
        else:
            rw = jnp.zeros((d, LANES), F32).at[:, :N_EXPERTS].set(router_w[i])
            rw_hi = rw.astype(BF16)
            rw_lo = (rw - rw_hi.astype(F32)).astype(BF16)
            x1, h, idx, wgt = _outproj(ya, yd, w_out_b, layer, x2, mod, g_ffn, seq, router_w=(rw_hi, rw_lo))
            buf_tok, pos, tile_e, n_tiles = _moe_plan(idx[:, :2], TM_MOE)
            ys = _moe_ffn(h, buf_tok, *moe_w, i, tile_e, n_tiles, TM_MOE)
            final = layer == depth - 1
            x2 = _moe_combine(ys, pos, wgt, x1, mod, final_g2, seq, final)
    if depth % 2 == 1:
        x2 = _final_norm(x2, final_g2)
    return x2.reshape(batch, seq, d)
```

```python
import functools
import math

import jax
import jax.numpy as jnp
from jax import lax
from jax.experimental import pallas as pl
from jax.experimental.pallas import tpu as pltpu

F32 = jnp.float32
BF16 = jnp.bfloat16

HEAD_DIM = 64
HALF_DIM = HEAD_DIM // 2
ROPE_THETA = 10000.0
EPS = 1e-6
SWA_Q_HEADS = 8
SWA_KV_HEADS = 2
WINDOW = 128
DIFF_HEADS = 4
N_EXPERTS = 8
N_MOD = 6
LANES = 128
SUBROWS = 8
VMEM_LIMIT = 56 * 1024 * 1024

QA_BLK, KA_BLK, QD_BLK, KD_BLK = 0, 4, 6, 10
ROPE_BLKS = 14
PROJ_COLS = ROPE_BLKS * LANES
VA_COL0 = 640
VD_COL0 = 1792
VT_WIN_BLK = 4
Q_SCALE = HEAD_DIM ** -0.5
LOG2E = math.log2(math.e)

TM_PROJ = 512
TQ_WIN = 256
TQ_DIFF = 512
TK_DIFF = 1024
TM_FFN = 1024
TF_FFN = 512
TM_MOE = 1024
TM_COMB = 256


def _permute_in_proj(w_in):
    depth, d, _ = w_in.shape
    qa, ka, qd, kd = 0, 512, 768, 1280

    def pairs(lo, n_pairs):
        t = w_in[:, :, lo:lo + n_pairs * LANES].reshape(depth, d, n_pairs, 2, 2, HALF_DIM)
        return t.transpose(0, 1, 2, 4, 3, 5).reshape(depth, d, n_pairs * LANES)

    def dup_halves(lo):
        t = w_in[:, :, lo:lo + SWA_KV_HEADS * HEAD_DIM].reshape(depth, d, SWA_KV_HEADS, 2, 1, HALF_DIM)
        return jnp.broadcast_to(t, (depth, d, SWA_KV_HEADS, 2, 2, HALF_DIM)).reshape(depth, d, SWA_KV_HEADS * LANES)

    return jnp.concatenate([pairs(qa, 4), dup_halves(ka), pairs(qd, DIFF_HEADS), pairs(kd, DIFF_HEADS)], axis=-1)


def _value_rows(w_in):
    depth, d, _ = w_in.shape
    vd = jnp.swapaxes(w_in[:, :, VD_COL0:], 1, 2)
    va = jnp.swapaxes(w_in[:, :, VA_COL0:VA_COL0 + SWA_KV_HEADS * HEAD_DIM], 1, 2)
    va = jnp.broadcast_to(va.reshape(depth, SWA_KV_HEADS, 1, HEAD_DIM, d), (depth, SWA_KV_HEADS, 2, HEAD_DIM, d))
    return jnp.concatenate([vd, va.reshape(depth, SWA_KV_HEADS * LANES, d)], axis=1)


def _first_of_pair_mask():
    lane = lax.broadcasted_iota(jnp.int32, (1, LANES), 1)
    return (lane // HALF_DIM) % 2 == 0


def _silu(g):
    return g / (1.0 + jnp.exp(-g))


def _params(sem):
    return pltpu.CompilerParams(dimension_semantics=sem, vmem_limit_bytes=VMEM_LIMIT)


def _adaln_kernel(c_ref, w_ref, b_ref, o_ref):
    a = _silu(c_ref[...])
    o_ref[0] = jnp.dot(a, w_ref[0], preferred_element_type=F32,
                       precision=lax.Precision.HIGHEST) + b_ref[0]


def _adaln(c_pad, ada_w, ada_b):
    depth, d, cols = ada_w.shape
    tn = 1536
    return pl.pallas_call(
        _adaln_kernel,
        grid=(depth, cols // tn),
        in_specs=[pl.BlockSpec((8, d), lambda l, j: (0, 0)),
                  pl.BlockSpec((1, d, tn), lambda l, j: (l, 0, j)),
                  pl.BlockSpec((1, 1, tn), lambda l, j: (l, 0, j))],
        out_specs=pl.BlockSpec((1, 8, tn), lambda l, j: (l, 0, j)),
        out_shape=jax.ShapeDtypeStruct((depth, 8, cols), F32),
        compiler_params=_params(("arbitrary", "arbitrary")),
        name="adaln",
    )(c_pad, ada_w, ada_b.reshape(depth, 1, cols))


def _rope_table_kernel(pos_ref, cos_ref, sin_ref):
    lane = lax.broadcasted_iota(jnp.int32, (1, LANES), 1)
    freq = (lane % HALF_DIM).astype(F32)
    inv = 1.0 / (ROPE_THETA ** (2.0 * freq / HEAD_DIM))
    ang = pos_ref[...].astype(F32) * inv
    sign = jnp.where(lane < HEAD_DIM, -1.0, 1.0)
    cos_ref[...] = jnp.cos(ang)
    sin_ref[...] = jnp.sin(ang) * sign


def _rope_tables(positions):
    n = positions.size
    tm = 1024
    pos = positions.reshape(n, 1)
    return pl.pallas_call(
        _rope_table_kernel,
        grid=(n // tm,),
        in_specs=[pl.BlockSpec((tm, 1), lambda i: (i, 0))],
        out_specs=[pl.BlockSpec((tm, LANES), lambda i: (i, 0))] * 2,
        out_shape=[jax.ShapeDtypeStruct((n, LANES), F32)] * 2,
        compiler_params=_params(("arbitrary",)),
        name="rope_tables",
    )(pos)


def _rms_mod(x, g, scale, shift):
    y = x * lax.rsqrt(jnp.mean(x * x, axis=-1, keepdims=True) + EPS) * g
    return y * (1.0 + scale) + shift


def _inproj_kernel(x_ref, mod_ref, g_ref, w_ref, wvt_ref, cos_ref, sin_ref, o_ref, vt_ref):
    h = _rms_mod(x_ref[...], g_ref[...], mod_ref[0, 1:2, :], mod_ref[0, 0:1, :]).astype(BF16)
    proj = jnp.dot(h, w_ref[...], preferred_element_type=F32)
    vt_ref[...] = lax.dot_general(wvt_ref[...], h, (((1,), (1,)), ((), ())),
                                  preferred_element_type=F32).astype(BF16)
    cos = cos_ref[...]
    sin = sin_ref[...]
    for blk in range(ROPE_BLKS):
        t = proj[:, blk * LANES:(blk + 1) * LANES]
        r = t * cos + pltpu.roll(t, HEAD_DIM, axis=1) * sin
        if blk < KA_BLK or QD_BLK <= blk < KD_BLK:
            r = r * (Q_SCALE * LOG2E)
        o_ref[:, blk * LANES:(blk + 1) * LANES] = r.astype(BF16)


def _inproj(x2, mod, g, w, wvt, layer, cos_t, sin_t, seq):
    n, d = x2.shape
    tm = TM_PROJ
    per_b = seq // tm
    vt_rows = wvt.shape[1]
    return pl.pallas_call(
        _inproj_kernel,
        grid=(n // tm,),
        in_specs=[pl.BlockSpec((tm, d), lambda i: (i, 0)),
                  pl.BlockSpec((1, N_MOD, d), lambda i: (i // per_b, 0, 0)),
                  pl.BlockSpec((1, d), lambda i: (0, 0)),
                  pl.BlockSpec((None, d, PROJ_COLS), lambda i: (layer, 0, 0)),
                  pl.BlockSpec((None, vt_rows, d), lambda i: (layer, 0, 0)),
                  pl.BlockSpec((tm, LANES), lambda i: (i, 0)),
                  pl.BlockSpec((tm, LANES), lambda i: (i, 0))],
        out_specs=[pl.BlockSpec((tm, PROJ_COLS), lambda i: (i, 0)),
                   pl.BlockSpec((vt_rows, tm), lambda i: (0, i))],
        out_shape=[jax.ShapeDtypeStruct((n, PROJ_COLS), BF16),
                   jax.ShapeDtypeStruct((vt_rows, n), BF16)],
        compiler_params=_params(("arbitrary",)),
        name="inproj",
    )(x2, mod, g, w, wvt, cos_t, sin_t)


def _win_kernel(sink_ref, q_ref, k_ref, vt_ref, o_ref, qt_sc, *, tq, seq):
    g = pl.program_id(1)
    i = pl.program_id(2)
    heads = SWA_Q_HEADS // SWA_KV_HEADS
    width = tq + 2 * WINDOW
    start = pl.multiple_of(jnp.clip(i * tq - WINDOW, 0, seq - width), LANES)
    first = _first_of_pair_mask()
    for blk in range(heads // 2):
        q = q_ref[:, blk * LANES:(blk + 1) * LANES].astype(F32)
        zero = jnp.zeros_like(q)
        qt_sc[:, (2 * blk) * tq:(2 * blk + 1) * tq] = jnp.where(first, q, zero).T.astype(BF16)
        qt_sc[:, (2 * blk + 1) * tq:(2 * blk + 2) * tq] = jnp.where(first, zero, q).T.astype(BF16)

    s = jnp.dot(k_ref[pl.ds(start, width), :], qt_sc[...], preferred_element_type=F32)
    kpos = start + lax.broadcasted_iota(jnp.int32, (width, tq), 0)
    qpos = i * tq + lax.broadcasted_iota(jnp.int32, (width, tq), 1)
    valid = jnp.abs(qpos - kpos) <= WINDOW
    s = jnp.concatenate([jnp.where(valid, s[:, c * tq:(c + 1) * tq], -1e30) for c in range(heads)], axis=1)
    col = lax.broadcasted_iota(jnp.int32, (1, heads * tq), 1)
    sk = jnp.zeros((1, heads * tq), F32)
    for c in range(heads):
        sk = jnp.where(col // tq == c, sink_ref[heads * g + c] * LOG2E, sk)
    m = jnp.maximum(jnp.max(s, axis=0, keepdims=True), sk)
    p = jnp.exp2(s - m)
    den = jnp.sum(p, axis=0, keepdims=True) + jnp.exp2(sk - m)
    o = jnp.dot(vt_ref[:, pl.ds(start, width)], p.astype(BF16), preferred_element_type=F32) / den
    lane = lax.broadcasted_iota(jnp.int32, (1, LANES), 1)
    for blk in range(heads // 2):
        oa = o[:, (2 * blk) * tq:(2 * blk + 1) * tq].T
        ob = o[:, (2 * blk + 1) * tq:(2 * blk + 2) * tq].T
        o_ref[:, blk * LANES:(blk + 1) * LANES] = jnp.where(lane < HEAD_DIM, oa, ob).astype(o_ref.dtype)


def _window_attention(proj, vt, sink, batch, seq):
    n = proj.shape[0]
    tq = TQ_WIN
    nq = seq // tq
    heads = SWA_Q_HEADS // SWA_KV_HEADS
    cols = heads // 2 * LANES
    kern = functools.partial(_win_kernel, tq=tq, seq=seq)
    return pl.pallas_call(
        kern,
        grid=(batch, SWA_KV_HEADS, nq),
        in_specs=[pl.BlockSpec(memory_space=pltpu.SMEM),
                  pl.BlockSpec((tq, cols), lambda b, g, i: (b * nq + i, g)),
                  pl.BlockSpec((seq, LANES), lambda b, g, i: (b, KA_BLK + g)),
                  pl.BlockSpec((LANES, seq), lambda b, g, i: (VT_WIN_BLK + g, b))],
        out_specs=pl.BlockSpec((tq, cols), lambda b, g, i: (b * nq + i, g)),
        out_shape=jax.ShapeDtypeStruct((n, SWA_KV_HEADS * cols), BF16),
        scratch_shapes=[pltpu.VMEM((LANES, heads * tq), BF16)],
        compiler_params=_params(("arbitrary", "arbitrary", "arbitrary")),
        name="window_attn",
    )(sink, proj, proj, vt)


def _diff_kernel(lam_ref, g_ref, q_ref, k_ref, vt_ref, o_ref, qt_sc, s_a, s_b, cm_a, cm_b, m_sc, l_sc, acc_sc,
                 *, tk, seq, lambda_init):
    tq = q_ref.shape[0]
    nk = seq // tk
    q = q_ref[...].astype(F32)
    first = _first_of_pair_mask()
    zero = jnp.zeros_like(q)
    qt_sc[:, :tq] = jnp.where(first, q, zero).T.astype(BF16)
    qt_sc[:, tq:] = jnp.where(first, zero, q).T.astype(BF16)

    m_sc[...] = jnp.full(m_sc.shape, -1e30, F32)
    l_sc[...] = jnp.zeros(l_sc.shape, F32)
    acc_sc[...] = jnp.zeros(acc_sc.shape, F32)

    def scores(t, s_ref, cm_ref):
        off = pl.multiple_of(t * tk, tk)
        s = jnp.dot(k_ref[pl.ds(off, tk), :], qt_sc[...], preferred_element_type=F32)
        s_ref[...] = s
        cm_ref[...] = jnp.max(s, axis=0, keepdims=True)

    def consume(t, s_ref, cm_ref):
        off = pl.multiple_of(t * tk, tk)
        m_prev = m_sc[...]
        m_new = jnp.maximum(m_prev, cm_ref[...])
        alpha = jnp.exp2(m_prev - m_new)
        p = jnp.exp2(s_ref[...] - m_new)
        l_sc[...] = alpha * l_sc[...] + jnp.sum(p, axis=0, keepdims=True)
        vb = vt_ref[:, pl.ds(off, tk)]
        acc_sc[...] = alpha * acc_sc[...] + jnp.dot(vb, p.astype(BF16), preferred_element_type=F32)
        m_sc[...] = m_new

    scores(0, s_a, cm_a)

    def body(u, carry):
        t = 2 * u
        scores(t + 1, s_b, cm_b)
        consume(t, s_a, cm_a)
        scores(t + 2, s_a, cm_a)
        consume(t + 1, s_b, cm_b)
        return carry

    lax.fori_loop(0, nk // 2 - 1, body, 0)
    scores(nk - 1, s_b, cm_b)
    consume(nk - 2, s_a, cm_a)
    consume(nk - 1, s_b, cm_b)

    lam1 = jnp.sum(lam_ref[0:1, :] * lam_ref[1:2, :], axis=-1, keepdims=True)
    lam2 = jnp.sum(lam_ref[2:3, :] * lam_ref[3:4, :], axis=-1, keepdims=True)
    lam = jnp.exp(lam1) - jnp.exp(lam2) + lambda_init
    o = acc_sc[...] / l_sc[...]
    o = o[:, :tq] - lam * o[:, tq:]
    y = o * lax.rsqrt(jnp.mean(o * o, axis=0, keepdims=True) + EPS)
    y = y.T * (g_ref[...] * (1.0 - lambda_init))
    o_ref[...] = y.astype(o_ref.dtype)


def _diff_attention(proj, vt, lam_vecs, subln_g, batch, seq, lambda_init):
    n = proj.shape[0]
    tq, tk = TQ_DIFF, TK_DIFF
    nq = seq // tq
    assert seq % (2 * tk) == 0 and seq % tq == 0, "the pipelined key loop consumes key blocks in pairs"
    kern = functools.partial(_diff_kernel, tk=tk, seq=seq, lambda_init=lambda_init)
    return pl.pallas_call(
        kern,
        grid=(batch, DIFF_HEADS, nq),
        in_specs=[pl.BlockSpec((4, HEAD_DIM), lambda b, h, i: (0, 0)),
                  pl.BlockSpec((1, LANES), lambda b, h, i: (0, 0)),
                  pl.BlockSpec((tq, LANES), lambda b, h, i: (b * nq + i, QD_BLK + h)),
                  pl.BlockSpec((seq, LANES), lambda b, h, i: (b, KD_BLK + h)),
                  pl.BlockSpec((LANES, seq), lambda b, h, i: (h, b))],
        out_specs=pl.BlockSpec((tq, LANES), lambda b, h, i: (b * nq + i, h)),
        out_shape=jax.ShapeDtypeStruct((n, DIFF_HEADS * LANES), BF16),
        scratch_shapes=[pltpu.VMEM((LANES, 2 * tq), BF16),
                        pltpu.VMEM((tk, 2 * tq), F32), pltpu.VMEM((tk, 2 * tq), F32),
                        pltpu.VMEM((1, 2 * tq), F32), pltpu.VMEM((1, 2 * tq), F32),
                        pltpu.VMEM((1, 2 * tq), F32), pltpu.VMEM((1, 2 * tq), F32),
                        pltpu.VMEM((LANES, 2 * tq), F32)],
        compiler_params=_params(("arbitrary", "arbitrary", "arbitrary")),
        name="diff_attn",
    )(lam_vecs, subln_g, proj, proj, vt)


def _route_top2(logits):
    lane = lax.broadcasted_iota(jnp.int32, logits.shape, 1)
    lg = jnp.where(lane < N_EXPERTS, logits, -jnp.inf)
    v1 = jnp.max(lg, axis=-1, keepdims=True)
    i1 = jnp.min(jnp.where(lg == v1, lane, LANES), axis=-1, keepdims=True)
    lg2 = jnp.where(lane == i1, -jnp.inf, lg)
    v2 = jnp.max(lg2, axis=-1, keepdims=True)
    i2 = jnp.min(jnp.where(lg2 == v2, lane, LANES), axis=-1, keepdims=True)
    e = jnp.exp(v2 - v1)
    w1 = 1.0 / (1.0 + e)
    w2 = e / (1.0 + e)
    idx = jnp.where(lane == 0, i1, jnp.where(lane == 1, i2, 0))
    wgt = jnp.where(lane == 0, w1, jnp.where(lane == 1, w2, 0.0))
    return idx, wgt


def _outproj_kernel(ya_ref, yd_ref, w_ref, x_ref, mod_ref, g_ref, *rest, moe):
    if moe:
        rwh_ref, rwl_ref, x1_ref, h_ref, idx_ref, wgt_ref = rest
    else:
        x1_ref, h_ref = rest
    half = ya_ref.shape[1]
    y = (jnp.dot(ya_ref[...], w_ref[:half, :], preferred_element_type=F32)
         + jnp.dot(yd_ref[...], w_ref[half:, :], preferred_element_type=F32))
    x1 = x_ref[...] + mod_ref[0, 2:3, :] * y
    x1_ref[...] = x1
    h = _rms_mod(x1, g_ref[...], mod_ref[0, 4:5, :], mod_ref[0, 3:4, :])
    if moe:
        for c in range(SUBROWS):
            h_ref[pl.ds(c, h.shape[0], stride=SUBROWS), :] = h[:, c * LANES:(c + 1) * LANES]
    else:
        h_ref[...] = h.astype(h_ref.dtype)
    if moe:
        h_hi = h.astype(BF16)
        h_lo = (h - h_hi.astype(F32)).astype(BF16)
        logits = (jnp.dot(h_hi, rwh_ref[...], preferred_element_type=F32)
                  + jnp.dot(h_hi, rwl_ref[...], preferred_element_type=F32)
                  + jnp.dot(h_lo, rwh_ref[...], preferred_element_type=F32))
        idx, wgt = _route_top2(logits)
        idx_ref[...] = idx
        wgt_ref[...] = wgt


def _outproj(ya, yd, w, layer, x2, mod, g, seq, router_w=None):
    n, d = x2.shape
    tm = TM_PROJ
    per_b = seq // tm
    moe = router_w is not None
    row = lambda i: (i, 0)
    in_specs = [pl.BlockSpec((tm, ya.shape[1]), row),
                pl.BlockSpec((tm, yd.shape[1]), row),
                pl.BlockSpec((None,) + w.shape[1:], lambda i: (layer, 0, 0)),
                pl.BlockSpec((tm, d), row),
                pl.BlockSpec((1, N_MOD, d), lambda i: (i // per_b, 0, 0)),
                pl.BlockSpec((1, d), lambda i: (0, 0))]
    out_specs = [pl.BlockSpec((tm, d), row), pl.BlockSpec(_to_subrows((tm, d)) if moe else (tm, d), row)]
    out_shape = [jax.ShapeDtypeStruct((n, d), F32),
                 jax.ShapeDtypeStruct(_to_subrows((n, d)), F32) if moe else jax.ShapeDtypeStruct((n, d), BF16)]
    args = [ya, yd, w, x2, mod, g]
    if moe:
        in_specs += [pl.BlockSpec((d, LANES), lambda i: (0, 0))] * 2
        out_specs += [pl.BlockSpec((tm, LANES), row)] * 2
        out_shape += [jax.ShapeDtypeStruct((n, LANES), jnp.int32), jax.ShapeDtypeStruct((n, LANES), F32)]
        args += list(router_w)
    return pl.pallas_call(
        functools.partial(_outproj_kernel, moe=moe),
        grid=(n // tm,),
        in_specs=in_specs, out_specs=out_specs, out_shape=out_shape,
        compiler_params=_params(("arbitrary",)),
        name="outproj_moe" if moe else "outproj",
    )(*args)


def _swiglu_step(h, wg_ref, wu_ref, wd_ref):
    g = jnp.dot(h, wg_ref[...], preferred_element_type=F32)
    u = jnp.dot(h, wu_ref[...], preferred_element_type=F32)
    return jnp.dot((_silu(g) * u).astype(BF16), wd_ref[...], preferred_element_type=F32)


def _ffn_kernel(h_ref, wg_ref, wu_ref, wd_ref, x_ref, mod_ref, o_ref, acc_ref):
    j = pl.program_id(1)

    @pl.when(j == 0)
    def _():
        acc_ref[...] = jnp.zeros(acc_ref.shape, F32)

    acc_ref[...] += _swiglu_step(h_ref[...], wg_ref, wu_ref, wd_ref)

    @pl.when(j == pl.num_programs(1) - 1)
    def _():
        o_ref[...] = x_ref[...] + mod_ref[0, 5:6, :] * acc_ref[...]


def _dense_ffn(h, wg, wu, wd, li, x1, mod, seq):
    n, d = x1.shape
    f = wg.shape[2]
    tm, tf = TM_FFN, TF_FFN
    per_b = seq // tm
    return pl.pallas_call(
        _ffn_kernel,
        grid=(n // tm, f // tf),
        in_specs=[pl.BlockSpec((tm, d), lambda i, j: (i, 0)),
                  pl.BlockSpec((None, d, tf), lambda i, j: (li, 0, j)),
                  pl.BlockSpec((None, d, tf), lambda i, j: (li, 0, j)),
                  pl.BlockSpec((None, tf, d), lambda i, j: (li, j, 0)),
                  pl.BlockSpec((tm, d), lambda i, j: (i, 0)),
                  pl.BlockSpec((1, N_MOD, d), lambda i, j: (i // per_b, 0, 0))],
        out_specs=pl.BlockSpec((tm, d), lambda i, j: (i, 0)),
        out_shape=jax.ShapeDtypeStruct((n, d), F32),
        scratch_shapes=[pltpu.VMEM((tm, d), F32)],
        compiler_params=_params(("arbitrary", "arbitrary")),
        name="dense_ffn",
    )(h, wg, wu, wd, x1, mod)


def _moe_plan(idx, tm):
    n = idx.shape[0]
    flat_e = idx.reshape(-1)
    onehot = (flat_e[:, None] == jnp.arange(N_EXPERTS, dtype=jnp.int32)[None, :]).astype(jnp.int32)
    csum = jnp.cumsum(onehot, axis=0)
    rank = jnp.sum((csum - onehot) * onehot, axis=1)
    counts = csum[-1]
    padded = (counts + tm - 1) // tm * tm
    pad_ends = jnp.cumsum(padded)
    pad_starts = pad_ends - padded
    dest = (pad_starts[flat_e] + rank).astype(jnp.int32)
    rows = 2 * n + N_EXPERTS * tm
    flat_tok = jnp.arange(2 * n, dtype=jnp.int32) // 2
    buf_tok = jnp.zeros((rows,), jnp.int32).at[dest].set(flat_tok)
    n_tiles = (pad_ends[-1] // tm).astype(jnp.int32).reshape(1)
    tile_start = jnp.arange(rows // tm, dtype=jnp.int32) * tm
    tile_e = jnp.minimum(jnp.sum((pad_ends[None, :] <= tile_start[:, None]).astype(jnp.int32), axis=1),
                         N_EXPERTS - 1).astype(jnp.int32)
    return buf_tok, dest.reshape(n, 2), tile_e, n_tiles


def _to_subrows(ref_or_shape):
    rows, d = ref_or_shape
    assert d == SUBROWS * LANES
    return (rows * SUBROWS, LANES)


def _row_copy(src_hbm, src_row, dst_ref, dst_row, sem):
    s = pl.multiple_of(src_row * SUBROWS, SUBROWS)
    t = pl.multiple_of(dst_row * SUBROWS, SUBROWS)
    return pltpu.make_async_copy(src_hbm.at[pl.ds(s, SUBROWS)], dst_ref.at[pl.ds(t, SUBROWS)], sem)


def _row_gather(idx_of, src_hbm, dst_ref, sem, rows):
    def issue(r, carry):
        _row_copy(src_hbm, idx_of(r), dst_ref, r, sem).start()
        return carry

    lax.fori_loop(0, rows, issue, 0, unroll=8)


def _row_gather_wait(src_hbm, dst_ref, sem, rows):
    pltpu.make_async_copy(src_hbm.at[pl.ds(0, rows * SUBROWS)], dst_ref, sem).wait()


def _moe_ffn_kernel(te_ref, nt_ref, tok_ref, h_hbm, wg_ref, wu_ref, wd_ref, o_ref, xbuf, xb16, acc_ref, sem, *, nf):
    i = pl.program_id(0)
    j = pl.program_id(1)
    tm = xb16.shape[0]
    n_tiles = nt_ref[0]
    slot = i % 2
    other = 1 - slot
    nxt = jnp.minimum(i + 1, n_tiles - 1)
    per_step = tm // nf

    def start_next(r):
        tok = tok_ref[nxt * tm + r]
        _row_copy(h_hbm, tok, xbuf.at[other], r, sem.at[other]).start()

    @pl.when(jnp.logical_and(i == 0, j == 0))
    def _():
        _row_gather(lambda r: tok_ref[r], h_hbm, xbuf.at[0], sem.at[0], tm)

    @pl.when(i < n_tiles)
    def _():
        @pl.when(j == 0)
        def _():
            _row_gather_wait(h_hbm, xbuf.at[slot], sem.at[slot], tm)
            for c in range(SUBROWS):
                xb16[:, c * LANES:(c + 1) * LANES] = xbuf[slot, pl.ds(c, tm, stride=SUBROWS), :].astype(BF16)
            acc_ref[...] = jnp.zeros(acc_ref.shape, F32)
            for r in range(per_step * nf, tm):
                start_next(r)

        for u in range(per_step):
            start_next(j * per_step + u)
        acc_ref[...] += _swiglu_step(xb16[...], wg_ref.at[0], wu_ref.at[0], wd_ref.at[0])

        @pl.when(j == nf - 1)
        def _():
            for c in range(SUBROWS):
                o_ref[pl.ds(c, tm, stride=SUBROWS), :] = acc_ref[:, c * LANES:(c + 1) * LANES]

        @pl.when(jnp.logical_and(j == nf - 1, i == n_tiles - 1))
        def _():
            _row_gather_wait(h_hbm, xbuf.at[other], sem.at[other], tm)

    @pl.when(jnp.logical_and(i >= nt_ref[0], j == pl.num_programs(1) - 1))
    def _():
        o_ref[...] = jnp.zeros(o_ref.shape, o_ref.dtype)


def _moe_ffn(h, buf_tok, wg, wu, wd, li, tile_e, n_tiles, tm):
    rows_total = buf_tok.shape[0]
    d = wg.shape[2]
    f = wg.shape[3]
    tf = TF_FFN
    nf = f // tf

    def _fj(i, j, nt):
        return jnp.where(i < nt[0], j, nf - 1)

    def up_map(i, j, te, nt, tok):
        return (li, te[jnp.minimum(i, nt[0] - 1)], 0, _fj(i, j, nt))

    def down_map(i, j, te, nt, tok):
        return (li, te[jnp.minimum(i, nt[0] - 1)], _fj(i, j, nt), 0)

    return pl.pallas_call(
        functools.partial(_moe_ffn_kernel, nf=nf),
        grid_spec=pltpu.PrefetchScalarGridSpec(
            num_scalar_prefetch=3,
            grid=(rows_total // tm, nf),
            in_specs=[pl.BlockSpec(memory_space=pl.ANY),
                      pl.BlockSpec((None, 1, d, tf), up_map),
                      pl.BlockSpec((None, 1, d, tf), up_map),
                      pl.BlockSpec((None, 1, tf, d), down_map)],
            out_specs=pl.BlockSpec(_to_subrows((tm, d)), lambda i, j, te, nt, tok: (i, 0)),
            scratch_shapes=[pltpu.VMEM((2,) + _to_subrows((tm, d)), F32), pltpu.VMEM((tm, d), BF16),
                            pltpu.VMEM((tm, d), F32), pltpu.SemaphoreType.DMA((2,))]),
        out_shape=jax.ShapeDtypeStruct(_to_subrows((rows_total, d)), F32),
        compiler_params=_params(("arbitrary", "arbitrary")),
        name="moe_ffn",
    )(tile_e, n_tiles, buf_tok, h, wg, wu, wd)


def _combine_kernel(pos_ref, ys_hbm, wgt_ref, x_ref, mod_ref, fg_ref, o_ref, buf, sem, *, tm, final):
    i = pl.program_id(0)
    slot = i % 2

    def gather(step, to_slot):
        for k in range(2):
            _row_gather(lambda r: pos_ref[2 * (step * tm + r) + k], ys_hbm, buf.at[to_slot, k],
                        sem.at[to_slot], tm)

    @pl.when(i == 0)
    def _():
        gather(0, 0)

    @pl.when(i + 1 < pl.num_programs(0))
    def _():
        gather(i + 1, 1 - slot)

    for k in range(2):
        _row_gather_wait(ys_hbm, buf.at[slot, k], sem.at[slot], tm)

    wgt = wgt_ref[...]
    w0, w1 = wgt[:, 0:1], wgt[:, 1:2]
    y = jnp.concatenate([w0 * buf[slot, 0, pl.ds(c, tm, stride=SUBROWS), :]
                         + w1 * buf[slot, 1, pl.ds(c, tm, stride=SUBROWS), :] for c in range(SUBROWS)], axis=1)
    out = x_ref[...] + mod_ref[0, 5:6, :] * y
    if final:
        out = out * lax.rsqrt(jnp.mean(out * out, axis=-1, keepdims=True) + EPS) * fg_ref[...]
    o_ref[...] = out


def _moe_combine(ys, pos, wgt, x1, mod, final_g, seq, final):
    n, d = x1.shape
    tm = TM_COMB
    per_b = seq // tm
    return pl.pallas_call(
        functools.partial(_combine_kernel, tm=tm, final=final),
        grid_spec=pltpu.PrefetchScalarGridSpec(
            num_scalar_prefetch=1,
            grid=(n // tm,),
            in_specs=[pl.BlockSpec(memory_space=pl.ANY),
                      pl.BlockSpec((tm, LANES), lambda i, p: (i, 0)),
                      pl.BlockSpec((tm, d), lambda i, p: (i, 0)),
                      pl.BlockSpec((1, N_MOD, d), lambda i, p: (i // per_b, 0, 0)),
                      pl.BlockSpec((1, d), lambda i, p: (0, 0))],
            out_specs=pl.BlockSpec((tm, d), lambda i, p: (i, 0)),
            scratch_shapes=[pltpu.VMEM((2, 2) + _to_subrows((tm, d)), F32),
                            pltpu.SemaphoreType.DMA((2,))]),
        out_shape=jax.ShapeDtypeStruct((n, d), F32),
        compiler_params=_params(("arbitrary",)),
        name="moe_combine_final" if final else "moe_combine",
    )(pos.reshape(-1), ys, wgt, x1, mod, final_g)


def _final_norm_kernel(x_ref, g_ref, o_ref):
    x = x_ref[...]
    o_ref[...] = x * lax.rsqrt(jnp.mean(x * x, axis=-1, keepdims=True) + EPS) * g_ref[...]


def _final_norm(x2, g):
    n, d = x2.shape
    tm = TM_PROJ
    return pl.pallas_call(
        _final_norm_kernel,
        grid=(n // tm,),
        in_specs=[pl.BlockSpec((tm, d), lambda i: (i, 0)), pl.BlockSpec((1, d), lambda i: (0, 0))],
        out_specs=pl.BlockSpec((tm, d), lambda i: (i, 0)),
        out_shape=jax.ShapeDtypeStruct((n, d), F32),
        compiler_params=_params(("arbitrary",)),
        name="final_norm",
    )(x2, g)


def kernel(x, c, positions, ada_w, ada_b, norm_mix_g, norm_ffn_g, w_in, w_out, sink, lam_q1, lam_k1, lam_q2, lam_k2, subln_g, ffn_w_gate, ffn_w_up, ffn_w_down, router_w, moe_w_gate, moe_w_up, moe_w_down, final_g):
    batch, seq, d = x.shape
    depth = w_in.shape[0]
    n = batch * seq

    c_pad = jnp.zeros((8, d), F32).at[:batch].set(c)
    mod_all = _adaln(c_pad, ada_w, ada_b)
    cos_t, sin_t = _rope_tables(positions)
    w_in_p = _permute_in_proj(w_in).astype(BF16)
    w_vdt = _value_rows(w_in).astype(BF16)
    w_out_b = w_out.astype(BF16)
    ffn_w = [w.astype(BF16) for w in (ffn_w_gate, ffn_w_up, ffn_w_down)]
    moe_w = [w.astype(BF16) for w in (moe_w_gate, moe_w_up, moe_w_down)]
    final_g2 = final_g.reshape(1, d)

    x2 = x.reshape(n, d)
    for layer in range(depth):
        lambda_init = 0.8 - 0.6 * math.exp(-0.3 * layer)
        mod = mod_all[layer, :batch].reshape(batch, N_MOD, d)
        proj, vt = _inproj(x2, mod, norm_mix_g[layer].reshape(1, d), w_in_p, w_vdt, layer, cos_t, sin_t, seq)
        ya = _window_attention(proj, vt, sink[layer], batch, seq)
        lam_vecs = jnp.stack([lam_q1[layer], lam_k1[layer], lam_q2[layer], lam_k2[layer]])
        yd = _diff_attention(proj, vt, lam_vecs, subln_g[layer].reshape(1, LANES), batch, seq, lambda_init)
        i = layer // 2
        g_ffn = norm_ffn_g[layer].reshape(1, d)
        if layer % 2 == 0:
            x1, h = _outproj(ya, yd, w_out_b, layer, x2, mod, g_ffn, seq)
            x2 = _dense_ffn(h, *ffn_w, i, x1, mod, seq)
        else:
            rw = jnp.zeros((d, LANES), F32).at[:, :N_EXPERTS].set(router_w[i])
            rw_hi = rw.astype(BF16)
            rw_lo = (rw - rw_hi.astype(F32)).astype(BF16)
            x1, h, idx, wgt = _outproj(ya, yd, w_out_b, layer, x2, mod, g_ffn, seq, router_w=(rw_hi, rw_lo))
            buf_tok, pos, tile_e, n_tiles = _moe_plan(idx[:, :2], TM_MOE)
            ys = _moe_ffn(h, buf_tok, *moe_w, i, tile_e, n_tiles, TM_MOE)
            final = layer == depth - 1
            x2 = _moe_combine(ys, pos, wgt, x1, mod, final_g2, seq, final)
    if depth % 2 == 1:
        x2 = _final_norm(x2, final_g2)
    return x2.reshape(batch, seq, d)
```

```python
import functools
import math

import jax
import jax.numpy as jnp
from jax import lax
from jax.experimental import pallas as pl
from jax.experimental.pallas import tpu as pltpu

F32 = jnp.float32
BF16 = jnp.bfloat16

HEAD_DIM = 64
HALF_DIM = HEAD_DIM // 2
ROPE_THETA = 10000.0
EPS = 1e-6
SWA_Q_HEADS = 8
SWA_KV_HEADS = 2
WINDOW = 128
DIFF_HEADS = 4
N_EXPERTS = 8
N_MOD = 6
LANES = 128
SUBROWS = 8
VMEM_LIMIT = 56 * 1024 * 1024

QA_BLK, KA_BLK, QD_BLK, KD_BLK = 0, 4, 6, 10
ROPE_BLKS = 14
PROJ_COLS = ROPE_BLKS * LANES
VA_COL0 = 640
VD_COL0 = 1792
VT_WIN_BLK = 4
Q_SCALE = HEAD_DIM ** -0.5
LOG2E = math.log2(math.e)

TM_PROJ = 512
TQ_WIN = 256
TQ_DIFF = 512
TK_DIFF = 1024
TM_FFN = 1024
TF_FFN = 512
TM_MOE = 1024
TM_COMB = 256


def _permute_in_proj(w_in):
    depth, d, _ = w_in.shape
    qa, ka, qd, kd = 0, 512, 768, 1280

    def pairs(lo, n_pairs):
        t = w_in[:, :, lo:lo + n_pairs * LANES].reshape(depth, d, n_pairs, 2, 2, HALF_DIM)
        return t.transpose(0, 1, 2, 4, 3, 5).reshape(depth, d, n_pairs * LANES)

    def dup_halves(lo):
        t = w_in[:, :, lo:lo + SWA_KV_HEADS * HEAD_DIM].reshape(depth, d, SWA_KV_HEADS, 2, 1, HALF_DIM)
        return jnp.broadcast_to(t, (depth, d, SWA_KV_HEADS, 2, 2, HALF_DIM)).reshape(depth, d, SWA_KV_HEADS * LANES)

    return jnp.concatenate([pairs(qa, 4), dup_halves(ka), pairs(qd, DIFF_HEADS), pairs(kd, DIFF_HEADS)], axis=-1)


def _value_rows(w_in):
    depth, d, _ = w_in.shape
    vd = jnp.swapaxes(w_in[:, :, VD_COL0:], 1, 2)
    va = jnp.swapaxes(w_in[:, :, VA_COL0:VA_COL0 + SWA_KV_HEADS * HEAD_DIM], 1, 2)
    va = jnp.broadcast_to(va.reshape(depth, SWA_KV_HEADS, 1, HEAD_DIM, d), (depth, SWA_KV_HEADS, 2, HEAD_DIM, d))
    return jnp.concatenate([vd, va.reshape(depth, SWA_KV_HEADS * LANES, d)], axis=1)


def _first_of_pair_mask():
    lane = lax.broadcasted_iota(jnp.int32, (1, LANES), 1)
    return (lane // HALF_DIM) % 2 == 0


def _silu(g):
    return g / (1.0 + jnp.exp(-g))


def _params(sem):
    return pltpu.CompilerParams(dimension_semantics=sem, vmem_limit_bytes=VMEM_LIMIT)


def _adaln_kernel(c_ref, w_ref, b_ref, o_ref):
    a = _silu(c_ref[...])
    o_ref[0] = jnp.dot(a, w_ref[0], preferred_element_type=F32,
                       precision=lax.Precision.HIGHEST) + b_ref[0]


def _adaln(c_pad, ada_w, ada_b):
    depth, d, cols = ada_w.shape
    tn = 1536
    return pl.pallas_call(
        _adaln_kernel,
        grid=(depth, cols // tn),
        in_specs=[pl.BlockSpec((8, d), lambda l, j: (0, 0)),
                  pl.BlockSpec((1, d, tn), lambda l, j: (l, 0, j)),
                  pl.BlockSpec((1, 1, tn), lambda l, j: (l, 0, j))],
        out_specs=pl.BlockSpec((1, 8, tn), lambda l, j: (l, 0, j)),
        out_shape=jax.ShapeDtypeStruct((depth, 8, cols), F32),
        compiler_params=_params(("arbitrary", "arbitrary")),
        name="adaln",
    )(c_pad, ada_w, ada_b.reshape(depth, 1, cols))


def _rope_table_kernel(pos_ref, cos_ref, sin_ref):
    lane = lax.broadcasted_iota(jnp.int32, (1, LANES), 1)
    freq = (lane % HALF_DIM).astype(F32)
    inv = 1.0 / (ROPE_THETA ** (2.0 * freq / HEAD_DIM))
    ang = pos_ref[...].astype(F32) * inv
    sign = jnp.where(lane < HEAD_DIM, -1.0, 1.0)
    cos_ref[...] = jnp.cos(ang)
    sin_ref[...] = jnp.sin(ang) * sign


def _rope_tables(positions):
    n = positions.size
    tm = 1024
    pos = positions.reshape(n, 1)
    return pl.pallas_call(
        _rope_table_kernel,
        grid=(n // tm,),
        in_specs=[pl.BlockSpec((tm, 1), lambda i: (i, 0))],
        out_specs=[pl.BlockSpec((tm, LANES), lambda i: (i, 0))] * 2,
        out_shape=[jax.ShapeDtypeStruct((n, LANES), F32)] * 2,
        compiler_params=_params(("arbitrary",)),
        name="rope_tables",
    )(pos)


def _rms_mod(x, g, scale, shift):
    y = x * lax.rsqrt(jnp.mean(x * x, axis=-1, keepdims=True) + EPS) * g
    return y * (1.0 + scale) + shift


def _inproj_kernel(x_ref, mod_ref, g_ref, w_ref, wvt_ref, cos_ref, sin_ref, o_ref, vt_ref):
    h = _rms_mod(x_ref[...], g_ref[...], mod_ref[0, 1:2, :], mod_ref[0, 0:1, :]).astype(BF16)
    proj = jnp.dot(h, w_ref[...], preferred_element_type=F32)
    vt_ref[...] = lax.dot_general(wvt_ref[...], h, (((1,), (1,)), ((), ())),
                                  preferred_element_type=F32).astype(BF16)
    cos = cos_ref[...]
    sin = sin_ref[...]
    for blk in range(ROPE_BLKS):
        t = proj[:, blk * LANES:(blk + 1) * LANES]
        r = t * cos + pltpu.roll(t, HEAD_DIM, axis=1) * sin
        if blk < KA_BLK or QD_BLK <= blk < KD_BLK:
            r = r * (Q_SCALE * LOG2E)
        o_ref[:, blk * LANES:(blk + 1) * LANES] = r.astype(BF16)


def _inproj(x2, mod, g, w, wvt, layer, cos_t, sin_t, seq):
    n, d = x2.shape
    tm = TM_PROJ
    per_b = seq // tm
    vt_rows = wvt.shape[1]
    return pl.pallas_call(
        _inproj_kernel,
        grid=(n // tm,),
        in_specs=[pl.BlockSpec((tm, d), lambda i: (i, 0)),
                  pl.BlockSpec((1, N_MOD, d), lambda i: (i // per_b, 0, 0)),
                  pl.BlockSpec((1, d), lambda i: (0, 0)),
                  pl.BlockSpec((None, d, PROJ_COLS), lambda i: (layer, 0, 0)),
                  pl.BlockSpec((None, vt_rows, d), lambda i: (layer, 0, 0)),
                  pl.BlockSpec((tm, LANES), lambda i: (i, 0)),
                  pl.BlockSpec((tm, LANES), lambda i: (i, 0))],
        out_specs=[pl.BlockSpec((tm, PROJ_COLS), lambda i: (i, 0)),
                   pl.BlockSpec((vt_rows, tm), lambda i: (0, i))],
        out_shape=[jax.ShapeDtypeStruct((n, PROJ_COLS), BF16),
                   jax.ShapeDtypeStruct((vt_rows, n), BF16)],
        compiler_params=_params(("arbitrary",)),
        name="inproj",
    )(x2, mod, g, w, wvt, cos_t, sin_t)


def _win_kernel(sink_ref, q_ref, k_ref, vt_ref, o_ref, qt_sc, *, tq, seq):
    g = pl.program_id(1)
    i = pl.program_id(2)
    heads = SWA_Q_HEADS // SWA_KV_HEADS
    width = tq + 2 * WINDOW
    start = pl.multiple_of(jnp.clip(i * tq - WINDOW, 0, seq - width), LANES)
    first = _first_of_pair_mask()
    for blk in range(heads // 2):
        q = q_ref[:, blk * LANES:(blk + 1) * LANES].astype(F32)
        zero = jnp.zeros_like(q)
        qt_sc[:, (2 * blk) * tq:(2 * blk + 1) * tq] = jnp.where(first, q, zero).T.astype(BF16)
        qt_sc[:, (2 * blk + 1) * tq:(2 * blk + 2) * tq] = jnp.where(first, zero, q).T.astype(BF16)

    s = jnp.dot(k_ref[pl.ds(start, width), :], qt_sc[...], preferred_element_type=F32)
    kpos = start + lax.broadcasted_iota(jnp.int32, (width, tq), 0)
    qpos = i * tq + lax.broadcasted_iota(jnp.int32, (width, tq), 1)
    valid = jnp.abs(qpos - kpos) <= WINDOW
    s = jnp.concatenate([jnp.where(valid, s[:, c * tq:(c + 1) * tq], -1e30) for c in range(heads)], axis=1)
    col = lax.broadcasted_iota(jnp.int32, (1, heads * tq), 1)
    sk = jnp.zeros((1, heads * tq), F32)
    for c in range(heads):
        sk = jnp.where(col // tq == c, sink_ref[heads * g + c] * LOG2E, sk)
    m = jnp.maximum(jnp.max(s, axis=0, keepdims=True), sk)
    p = jnp.exp2(s - m)
    den = jnp.sum(p, axis=0, keepdims=True) + jnp.exp2(sk - m)
    o = jnp.dot(vt_ref[:, pl.ds(start, width)], p.astype(BF16), preferred_element_type=F32) / den
    lane = lax.broadcasted_iota(jnp.int32, (1, LANES), 1)
    for blk in range(heads // 2):
        oa = o[:, (2 * blk) * tq:(2 * blk + 1) * tq].T
        ob = o[:, (2 * blk + 1) * tq:(2 * blk + 2) * tq].T
        o_ref[:, blk * LANES:(blk + 1) * LANES] = jnp.where(lane < HEAD_DIM, oa, ob).astype(o_ref.dtype)


def _window_attention(proj, vt, sink, batch, seq):
    n = proj.shape[0]
    tq = TQ_WIN
    nq = seq // tq
    heads = SWA_Q_HEADS // SWA_KV_HEADS
    cols = heads // 2 * LANES
    kern = functools.partial(_win_kernel, tq=tq, seq=seq)
    return pl.pallas_call(
        kern,
        grid=(batch, SWA_KV_HEADS, nq),
        in_specs=[pl.BlockSpec(memory_space=pltpu.SMEM),
                  pl.BlockSpec((tq, cols), lambda b, g, i: (b * nq + i, g)),
                  pl.BlockSpec((seq, LANES), lambda b, g, i: (b, KA_BLK + g)),
                  pl.BlockSpec((LANES, seq), lambda b, g, i: (VT_WIN_BLK + g, b))],
        out_specs=pl.BlockSpec((tq, cols), lambda b, g, i: (b * nq + i, g)),
        out_shape=jax.ShapeDtypeStruct((n, SWA_KV_HEADS * cols), BF16),
        scratch_shapes=[pltpu.VMEM((LANES, heads * tq), BF16)],
        compiler_params=_params(("arbitrary", "arbitrary", "arbitrary")),
        name="window_attn",
    )(sink, proj, proj, vt)


def _diff_kernel(lam_ref, g_ref, q_ref, k_ref, vt_ref, o_ref, qt_sc, s_a, s_b, cm_a, cm_b, m_sc, l_sc, acc_sc,
                 *, tk, seq, lambda_init):
    tq = q_ref.shape[0]
    nk = seq // tk
    q = q_ref[...].astype(F32)
    first = _first_of_pair_mask()
    zero = jnp.zeros_like(q)
    qt_sc[:, :tq] = jnp.where(first, q, zero).T.astype(BF16)
    qt_sc[:, tq:] = jnp.where(first, zero, q).T.astype(BF16)

    m_sc[...] = jnp.full(m_sc.shape, -1e30, F32)
    l_sc[...] = jnp.zeros(l_sc.shape, F32)
    acc_sc[...] = jnp.zeros(acc_sc.shape, F32)

    def scores(t, s_ref, cm_ref):
        off = pl.multiple_of(t * tk, tk)
        s = jnp.dot(k_ref[pl.ds(off, tk), :], qt_sc[...], preferred_element_type=F32)
        s_ref[...] = s
        cm_ref[...] = jnp.max(s, axis=0, keepdims=True)

    def consume(t, s_ref, cm_ref):
        off = pl.multiple_of(t * tk, tk)
        m_prev = m_sc[...]
        m_new = jnp.maximum(m_prev, cm_ref[...])
        alpha = jnp.exp2(m_prev - m_new)
        p = jnp.exp2(s_ref[...] - m_new)
        l_sc[...] = alpha * l_sc[...] + jnp.sum(p, axis=0, keepdims=True)
        vb = vt_ref[:, pl.ds(off, tk)]
        acc_sc[...] = alpha * acc_sc[...] + jnp.dot(vb, p.astype(BF16), preferred_element_type=F32)
        m_sc[...] = m_new

    scores(0, s_a, cm_a)

    def body(u, carry):
        t = 2 * u
        scores(t + 1, s_b, cm_b)
        consume(t, s_a, cm_a)
        scores(t + 2, s_a, cm_a)
        consume(t + 1, s_b, cm_b)
        return carry

    lax.fori_loop(0, nk // 2 - 1, body, 0)
    scores(nk - 1, s_b, cm_b)
    consume(nk - 2, s_a, cm_a)
    consume(nk - 1, s_b, cm_b)

    lam1 = jnp.sum(lam_ref[0:1, :] * lam_ref[1:2, :], axis=-1, keepdims=True)
    lam2 = jnp.sum(lam_ref[2:3, :] * lam_ref[3:4, :], axis=-1, keepdims=True)
    lam = jnp.exp(lam1) - jnp.exp(lam2) + lambda_init
    o = acc_sc[...] / l_sc[...]
    o = o[:, :tq] - lam * o[:, tq:]
    y = o * lax.rsqrt(jnp.mean(o * o, axis=0, keepdims=True) + EPS)
    y = y.T * (g_ref[...] * (1.0 - lambda_init))
    o_ref[...] = y.astype(o_ref.dtype)


def _diff_attention(proj, vt, lam_vecs, subln_g, batch, seq, lambda_init):
    n = proj.shape[0]
    tq, tk = TQ_DIFF, TK_DIFF
    nq = seq // tq
    assert seq % (2 * tk) == 0 and seq % tq == 0, "the pipelined key loop consumes key blocks in pairs"
    kern = functools.partial(_diff_kernel, tk=tk, seq=seq, lambda_init=lambda_init)
    return pl.pallas_call(
        kern,
        grid=(batch, DIFF_HEADS, nq),
        in_specs=[pl.BlockSpec((4, HEAD_DIM), lambda b, h, i: (0, 0)),
                  pl.BlockSpec((1, LANES), lambda b, h, i: (0, 0)),
                  pl.BlockSpec((tq, LANES), lambda b, h, i: (b * nq + i, QD_BLK + h)),
                  pl.BlockSpec((seq, LANES), lambda b, h, i: (b, KD_BLK + h)),
                  pl.BlockSpec((LANES, seq), lambda b, h, i: (h, b))],
        out_specs=pl.BlockSpec((tq, LANES), lambda b, h, i: (b * nq + i, h)),
        out_shape=jax.ShapeDtypeStruct((n, DIFF_HEADS * LANES), BF16),
        scratch_shapes=[pltpu.VMEM((LANES, 2 * tq), BF16),
                        pltpu.VMEM((tk, 2 * tq), F32), pltpu.VMEM((tk, 2 * tq), F32),
                        pltpu.VMEM((1, 2 * tq), F32), pltpu.VMEM((1, 2 * tq), F32),
                        pltpu.VMEM((1, 2 * tq), F32), pltpu.VMEM((1, 2 * tq), F32),
                        pltpu.VMEM((LANES, 2 * tq), F32)],
        compiler_params=_params(("arbitrary", "arbitrary", "arbitrary")),
        name="diff_attn",
    )(lam_vecs, subln_g, proj, proj, vt)


def _route_top2(logits):
    lane = lax.broadcasted_iota(jnp.int32, logits.shape, 1)
    lg = jnp.where(lane < N_EXPERTS, logits, -jnp.inf)
    v1 = jnp.max(lg, axis=-1, keepdims=True)
    i1 = jnp.min(jnp.where(lg == v1, lane, LANES), axis=-1, keepdims=True)
    lg2 = jnp.where(lane == i1, -jnp.inf, lg)
    v2 = jnp.max(lg2, axis=-1, keepdims=True)
    i2 = jnp.min(jnp.where(lg2 == v2, lane, LANES), axis=-1, keepdims=True)
    e = jnp.exp(v2 - v1)
    w1 = 1.0 / (1.0 + e)
    w2 = e / (1.0 + e)
    idx = jnp.where(lane == 0, i1, jnp.where(lane == 1, i2, 0))
    wgt = jnp.where(lane == 0, w1, jnp.where(lane == 1, w2, 0.0))
    return idx, wgt


def _outproj_kernel(ya_ref, yd_ref, w_ref, x_ref, mod_ref, g_ref, *rest, moe):
    if moe:
        rwh_ref, rwl_ref, x1_ref, h_ref, idx_ref, wgt_ref = rest
    else:
        x1_ref, h_ref = rest
    half = ya_ref.shape[1]
    y = (jnp.dot(ya_ref[...], w_ref[:half, :], preferred_element_type=F32)
         + jnp.dot(yd_ref[...], w_ref[half:, :], preferred_element_type=F32))
    x1 = x_ref[...] + mod_ref[0, 2:3, :] * y
    x1_ref[...] = x1
    h = _rms_mod(x1, g_ref[...], mod_ref[0, 4:5, :], mod_ref[0, 3:4, :])
    if moe:
        for c in range(SUBROWS):
            h_ref[pl.ds(c, h.shape[0], stride=SUBROWS), :] = h[:, c * LANES:(c + 1) * LANES]
    else:
        h_ref[...] = h.astype(h_ref.dtype)
    if moe:
        h_hi = h.astype(BF16)
        h_lo = (h - h_hi.astype(F32)).astype(BF16)
        logits = (jnp.dot(h_hi, rwh_ref[...], preferred_element_type=F32)
                  + jnp.dot(h_hi, rwl_ref[...], preferred_element_type=F32)
                  + jnp.dot(h_lo, rwh_ref[...], preferred_element_type=F32))
        idx, wgt = _route_top2(logits)
        idx_ref[...] = idx
        wgt_ref[...] = wgt


def _outproj(ya, yd, w, layer, x2, mod, g, seq, router_w=None):
    n, d = x2.shape
    tm = TM_PROJ
    per_b = seq // tm
    moe = router_w is not None
    row = lambda i: (i, 0)
    in_specs = [pl.BlockSpec((tm, ya.shape[1]), row),
                pl.BlockSpec((tm, yd.shape[1]), row),
                pl.BlockSpec((None,) + w.shape[1:], lambda i: (layer, 0, 0)),
                pl.BlockSpec((tm, d), row),
                pl.BlockSpec((1, N_MOD, d), lambda i: (i // per_b, 0, 0)),
                pl.BlockSpec((1, d), lambda i: (0, 0))]
    out_specs = [pl.BlockSpec((tm, d), row), pl.BlockSpec(_to_subrows((tm, d)) if moe else (tm, d), row)]
    out_shape = [jax.ShapeDtypeStruct((n, d), F32),
                 jax.ShapeDtypeStruct(_to_subrows((n, d)), F32) if moe else jax.ShapeDtypeStruct((n, d), BF16)]
    args = [ya, yd, w, x2, mod, g]
    if moe:
        in_specs += [pl.BlockSpec((d, LANES), lambda i: (0, 0))] * 2
        out_specs += [pl.BlockSpec((tm, LANES), row)] * 2
        out_shape += [jax.ShapeDtypeStruct((n, LANES), jnp.int32), jax.ShapeDtypeStruct((n, LANES), F32)]
        args += list(router_w)
    return pl.pallas_call(
        functools.partial(_outproj_kernel, moe=moe),
        grid=(n // tm,),
        in_specs=in_specs, out_specs=out_specs, out_shape=out_shape,
        compiler_params=_params(("arbitrary",)),
        name="outproj_moe" if moe else "outproj",
    )(*args)


def _swiglu_step(h, wg_ref, wu_ref, wd_ref):
    g = jnp.dot(h, wg_ref[...].astype(BF16), preferred_element_type=F32)
    u = jnp.dot(h, wu_ref[...].astype(BF16), preferred_element_type=F32)
    return jnp.dot((_silu(g) * u).astype(BF16), wd_ref[...].astype(BF16), preferred_element_type=F32)


def _ffn_kernel(h_ref, wg_ref, wu_ref, wd_ref, x_ref, mod_ref, o_ref, acc_ref):
    j = pl.program_id(1)

    @pl.when(j == 0)
    def _():
        acc_ref[...] = jnp.zeros(acc_ref.shape, F32)

    acc_ref[...] += _swiglu_step(h_ref[...], wg_ref, wu_ref, wd_ref)

    @pl.when(j == pl.num_programs(1) - 1)
    def _():
        o_ref[...] = x_ref[...] + mod_ref[0, 5:6, :] * acc_ref[...]


def _dense_ffn(h, wg, wu, wd, li, x1, mod, seq):
    n, d = x1.shape
    f = wg.shape[2]
    tm, tf = TM_FFN, TF_FFN
    per_b = seq // tm
    return pl.pallas_call(
        _ffn_kernel,
        grid=(n // tm, f // tf),
        in_specs=[pl.BlockSpec((tm, d), lambda i, j: (i, 0)),
                  pl.BlockSpec((None, d, tf), lambda i, j: (li, 0, j)),
                  pl.BlockSpec((None, d, tf), lambda i, j: (li, 0, j)),
                  pl.BlockSpec((None, tf, d), lambda i, j: (li, j, 0)),
                  pl.BlockSpec((tm, d), lambda i, j: (i, 0)),
                  pl.BlockSpec((1, N_MOD, d), lambda i, j: (i // per_b, 0, 0))],
        out_specs=pl.BlockSpec((tm, d), lambda i, j: (i, 0)),
        out_shape=jax.ShapeDtypeStruct((n, d), F32),
        scratch_shapes=[pltpu.VMEM((tm, d), F32)],
        compiler_params=_params(("arbitrary", "arbitrary")),
        name="dense_ffn",
    )(h, wg, wu, wd, x1, mod)


def _moe_plan(idx, tm):
    n = idx.shape[0]
    flat_e = idx.reshape(-1)
    onehot = (flat_e[:, None] == jnp.arange(N_EXPERTS, dtype=jnp.int32)[None, :]).astype(jnp.int32)
    csum = jnp.cumsum(onehot, axis=0)
    rank = jnp.sum((csum - onehot) * onehot, axis=1)
    counts = csum[-1]
    padded = (counts + tm - 1) // tm * tm
    pad_ends = jnp.cumsum(padded)
    pad_starts = pad_ends - padded
    dest = (pad_starts[flat_e] + rank).astype(jnp.int32)
    rows = 2 * n + N_EXPERTS * tm
    flat_tok = jnp.arange(2 * n, dtype=jnp.int32) // 2
    buf_tok = jnp.zeros((rows,), jnp.int32).at[dest].set(flat_tok)
    n_tiles = (pad_ends[-1] // tm).astype(jnp.int32).reshape(1)
    tile_start = jnp.arange(rows // tm, dtype=jnp.int32) * tm
    tile_e = jnp.minimum(jnp.sum((pad_ends[None, :] <= tile_start[:, None]).astype(jnp.int32), axis=1),
                         N_EXPERTS - 1).astype(jnp.int32)
    return buf_tok, dest.reshape(n, 2), tile_e, n_tiles


def _to_subrows(ref_or_shape):
    rows, d = ref_or_shape
    assert d == SUBROWS * LANES
    return (rows * SUBROWS, LANES)


def _row_copy(src_hbm, src_row, dst_ref, dst_row, sem):
    s = pl.multiple_of(src_row * SUBROWS, SUBROWS)
    t = pl.multiple_of(dst_row * SUBROWS, SUBROWS)
    return pltpu.make_async_copy(src_hbm.at[pl.ds(s, SUBROWS)], dst_ref.at[pl.ds(t, SUBROWS)], sem)


def _row_gather(idx_of, src_hbm, dst_ref, sem, rows):
    def issue(r, carry):
        _row_copy(src_hbm, idx_of(r), dst_ref, r, sem).start()
        return carry

    lax.fori_loop(0, rows, issue, 0, unroll=8)


def _row_gather_wait(src_hbm, dst_ref, sem, rows):
    pltpu.make_async_copy(src_hbm.at[pl.ds(0, rows * SUBROWS)], dst_ref, sem).wait()


def _moe_ffn_kernel(te_ref, nt_ref, tok_ref, h_hbm, wg_ref, wu_ref, wd_ref, o_ref, xbuf, xb16, acc_ref, sem, *, nf):
    i = pl.program_id(0)
    j = pl.program_id(1)
    tm = xb16.shape[0]
    n_tiles = nt_ref[0]
    slot = i % 2
    other = 1 - slot
    nxt = jnp.minimum(i + 1, n_tiles - 1)
    per_step = tm // nf

    def start_next(r):
        tok = tok_ref[nxt * tm + r]
        _row_copy(h_hbm, tok, xbuf.at[other], r, sem.at[other]).start()

    @pl.when(jnp.logical_and(i == 0, j == 0))
    def _():
        _row_gather(lambda r: tok_ref[r], h_hbm, xbuf.at[0], sem.at[0], tm)

    @pl.when(i < n_tiles)
    def _():
        @pl.when(j == 0)
        def _():
            _row_gather_wait(h_hbm, xbuf.at[slot], sem.at[slot], tm)
            for c in range(SUBROWS):
                xb16[:, c * LANES:(c + 1) * LANES] = xbuf[slot, pl.ds(c, tm, stride=SUBROWS), :].astype(BF16)
            acc_ref[...] = jnp.zeros(acc_ref.shape, F32)
            for r in range(per_step * nf, tm):
                start_next(r)

        for u in range(per_step):
            start_next(j * per_step + u)
        acc_ref[...] += _swiglu_step(xb16[...], wg_ref.at[0], wu_ref.at[0], wd_ref.at[0])

        @pl.when(j == nf - 1)
        def _():
            for c in range(SUBROWS):
                o_ref[pl.ds(c, tm, stride=SUBROWS), :] = acc_ref[:, c * LANES:(c + 1) * LANES]

        @pl.when(jnp.logical_and(j == nf - 1, i == n_tiles - 1))
        def _():
            _row_gather_wait(h_hbm, xbuf.at[other], sem.at[other], tm)

    @pl.when(jnp.logical_and(i >= nt_ref[0], j == pl.num_programs(1) - 1))
    def _():
        o_ref[...] = jnp.zeros(o_ref.shape, o_ref.dtype)


def _moe_ffn(h, buf_tok, wg, wu, wd, li, tile_e, n_tiles, tm):
    rows_total = buf_tok.shape[0]
    d = wg.shape[2]
    f = wg.shape[3]
    tf = TF_FFN
    nf = f // tf

    def _fj(i, j, nt):
        return jnp.where(i < nt[0], j, nf - 1)

    def up_map(i, j, te, nt, tok):
        return (li, te[jnp.minimum(i, nt[0] - 1)], 0, _fj(i, j, nt))

    def down_map(i, j, te, nt, tok):
        return (li, te[jnp.minimum(i, nt[0] - 1)], _fj(i, j, nt), 0)

    return pl.pallas_call(
        functools.partial(_moe_ffn_kernel, nf=nf),
        grid_spec=pltpu.PrefetchScalarGridSpec(
            num_scalar_prefetch=3,
            grid=(rows_total // tm, nf),
            in_specs=[pl.BlockSpec(memory_space=pl.ANY),
                      pl.BlockSpec((None, 1, d, tf), up_map),
                      pl.BlockSpec((None, 1, d, tf), up_map),
                      pl.BlockSpec((None, 1, tf, d), down_map)],
            out_specs=pl.BlockSpec(_to_subrows((tm, d)), lambda i, j, te, nt, tok: (i, 0)),
            scratch_shapes=[pltpu.VMEM((2,) + _to_subrows((tm, d)), F32), pltpu.VMEM((tm, d), BF16),
                            pltpu.VMEM((tm, d), F32), pltpu.SemaphoreType.DMA((2,))]),
        out_shape=jax.ShapeDtypeStruct(_to_subrows((rows_total, d)), F32),
        compiler_params=_params(("arbitrary", "arbitrary")),
        name="moe_ffn",
    )(tile_e, n_tiles, buf_tok, h, wg, wu, wd)


def _combine_kernel(pos_ref, ys_hbm, wgt_ref, x_ref, mod_ref, fg_ref, o_ref, buf, sem, *, tm, final):
    i = pl.program_id(0)
    slot = i % 2

    def gather(step, to_slot):
        for k in range(2):
            _row_gather(lambda r: pos_ref[2 * (step * tm + r) + k], ys_hbm, buf.at[to_slot, k],
                        sem.at[to_slot], tm)

    @pl.when(i == 0)
    def _():
        gather(0, 0)

    @pl.when(i + 1 < pl.num_programs(0))
    def _():
        gather(i + 1, 1 - slot)

    for k in range(2):
        _row_gather_wait(ys_hbm, buf.at[slot, k], sem.at[slot], tm)

    wgt = wgt_ref[...]
    w0, w1 = wgt[:, 0:1], wgt[:, 1:2]
    y = jnp.concatenate([w0 * buf[slot, 0, pl.ds(c, tm, stride=SUBROWS), :]
                         + w1 * buf[slot, 1, pl.ds(c, tm, stride=SUBROWS), :] for c in range(SUBROWS)], axis=1)
    out = x_ref[...] + mod_ref[0, 5:6, :] * y
    if final:
        out = out * lax.rsqrt(jnp.mean(out * out, axis=-1, keepdims=True) + EPS) * fg_ref[...]
    o_ref[...] = out


def _moe_combine(ys, pos, wgt, x1, mod, final_g, seq, final):
    n, d = x1.shape
    tm = TM_COMB
    per_b = seq // tm
    return pl.pallas_call(
        functools.partial(_combine_kernel, tm=tm, final=final),
        grid_spec=pltpu.PrefetchScalarGridSpec(
            num_scalar_prefetch=1,
            grid=(n // tm,),
            in_specs=[pl.BlockSpec(memory_space=pl.ANY),
                      pl.BlockSpec((tm, LANES), lambda i, p: (i, 0)),
                      pl.BlockSpec((tm, d), lambda i, p: (i, 0)),
                      pl.BlockSpec((1, N_MOD, d), lambda i, p: (i // per_b, 0, 0)),
                      pl.BlockSpec((1, d), lambda i, p: (0, 0))],
            out_specs=pl.BlockSpec((tm, d), lambda i, p: (i, 0)),
            scratch_shapes=[pltpu.VMEM((2, 2) + _to_subrows((tm, d)), F32),
                            pltpu.SemaphoreType.DMA((2,))]),
        out_shape=jax.ShapeDtypeStruct((n, d), F32),
        compiler_params=_params(("arbitrary",)),
        name="moe_combine_final" if final else "moe_combine",
    )(pos.reshape(-1), ys, wgt, x1, mod, final_g)


def _final_norm_kernel(x_ref, g_ref, o_ref):
    x = x_ref[...]
    o_ref[...] = x * lax.rsqrt(jnp.mean(x * x, axis=-1, keepdims=True) + EPS) * g_ref[...]


def _final_norm(x2, g):
    n, d = x2.shape
    tm = TM_PROJ
    return pl.pallas_call(
        _final_norm_kernel,
        grid=(n // tm,),
        in_specs=[pl.BlockSpec((tm, d), lambda i: (i, 0)), pl.BlockSpec((1, d), lambda i: (0, 0))],
        out_specs=pl.BlockSpec((tm, d), lambda i: (i, 0)),
        out_shape=jax.ShapeDtypeStruct((n, d), F32),
        compiler_params=_params(("arbitrary",)),
        name="final_norm",
    )(x2, g)


def kernel(x, c, positions, ada_w, ada_b, norm_mix_g, norm_ffn_g, w_in, w_out, sink, lam_q1, lam_k1, lam_q2, lam_k2, subln_g, ffn_w_gate, ffn_w_up, ffn_w_down, router_w, moe_w_gate, moe_w_up, moe_w_down, final_g):
    batch, seq, d = x.shape
    depth = w_in.shape[0]
    n = batch * seq

    c_pad = jnp.zeros((8, d), F32).at[:batch].set(c)
    mod_all = _adaln(c_pad, ada_w, ada_b)
    cos_t, sin_t = _rope_tables(positions)
    w_in_p = _permute_in_proj(w_in).astype(BF16)
    w_vdt = _value_rows(w_in).astype(BF16)
    w_out_b = w_out.astype(BF16)
    ffn_w = [w.astype(BF16) for w in (ffn_w_gate, ffn_w_up, ffn_w_down)]
    moe_w = [moe_w_gate, moe_w_up, moe_w_down]
    final_g2 = final_g.reshape(1, d)

    x2 = x.reshape(n, d)
    for layer in range(depth):
        lambda_init = 0.8 - 0.6 * math.exp(-0.3 * layer)
        mod = mod_all[layer, :batch].reshape(batch, N_MOD, d)
        proj, vt = _inproj(x2, mod, norm_mix_g[layer].reshape(1, d), w_in_p, w_vdt, layer, cos_t, sin_t, seq)
        ya = _window_attention(proj, vt, sink[layer], batch, seq)
        lam_vecs = jnp.stack([lam_q1[layer], lam_k1[layer], lam_q2[layer], lam_k2[layer]])
        yd = _diff_attention(proj, vt, lam_vecs, subln_g[layer].reshape(1, LANES), batch, seq, lambda_init)
        i = layer // 2
        g_ffn = norm_ffn_g[layer].reshape(1, d)
        if layer % 2 == 0:
            x1, h = _outproj(ya, yd, w_out_b, layer, x2, mod, g_ffn, seq)
            x2 = _dense_ffn(h, *ffn_w, i, x1, mod, seq)
        else:
            rw = jnp.zeros((d, LANES), F32).at[:, :N_EXPERTS].set(router_w[i])
            rw_hi = rw.astype(BF16)
            rw_lo = (rw - rw_hi.astype(F32)).astype(BF16)
            x1, h, idx, wgt = _outproj(ya, yd, w_out_b, layer, x2, mod, g_ffn, seq, router_w=(rw_hi, rw_lo))
            buf_tok, pos, tile_e, n_tiles = _moe_plan(idx[:, :2], TM_MOE)
            ys = _moe_ffn(h, buf_tok, *moe_w, i, tile_e, n_tiles, TM_MOE)
            final = layer == depth - 1
            x2 = _moe_combine(ys, pos, wgt, x1, mod, final_g2, seq, final)
    if depth % 2 == 1:
        x2 = _final_norm(x2, final_g2)
    return x2.reshape(batch, seq, d)
```

```python
import functools
import math

import jax
import jax.numpy as jnp
from jax import lax
from jax.experimental import pallas as pl
from jax.experimental.pallas import tpu as pltpu

F32 = jnp.float32
BF16 = jnp.bfloat16

HEAD_DIM = 64
HALF_DIM = HEAD_DIM // 2
ROPE_THETA = 10000.0
EPS = 1e-6
SWA_Q_HEADS = 8
SWA_KV_HEADS = 2
WINDOW = 128
DIFF_HEADS = 4
N_EXPERTS = 8
N_MOD = 6
LANES = 128
SUBROWS = 8
VMEM_LIMIT = 56 * 1024 * 1024

QA_BLK, KA_BLK, QD_BLK, KD_BLK = 0, 4, 6, 10
ROPE_BLKS = 14
PROJ_COLS = ROPE_BLKS * LANES
VA_COL0 = 640
VD_COL0 = 1792
VT_WIN_BLK = 4
Q_SCALE = HEAD_DIM ** -0.5
LOG2E = math.log2(math.e)

TM_PROJ = 512
TQ_WIN = 256
TQ_DIFF = 512
TK_DIFF = 1024
TM_FFN = 1024
TF_FFN = 512
TM_MOE = 1024
TM_COMB = 256


def _permute_in_proj(w_in):
    depth, d, _ = w_in.shape
    qa, ka, qd, kd = 0, 512, 768, 1280

    def pairs(lo, n_pairs):
        t = w_in[:, :, lo:lo + n_pairs * LANES].reshape(depth, d, n_pairs, 2, 2, HALF_DIM)
        return t.transpose(0, 1, 2, 4, 3, 5).reshape(depth, d, n_pairs * LANES)

    def dup_halves(lo):
        t = w_in[:, :, lo:lo + SWA_KV_HEADS * HEAD_DIM].reshape(depth, d, SWA_KV_HEADS, 2, 1, HALF_DIM)
        return jnp.broadcast_to(t, (depth, d, SWA_KV_HEADS, 2, 2, HALF_DIM)).reshape(depth, d, SWA_KV_HEADS * LANES)

    return jnp.concatenate([pairs(qa, 4), dup_halves(ka), pairs(qd, DIFF_HEADS), pairs(kd, DIFF_HEADS)], axis=-1)


def _value_rows(w_in):
    depth, d, _ = w_in.shape
    vd = jnp.swapaxes(w_in[:, :, VD_COL0:], 1, 2)
    va = jnp.swapaxes(w_in[:, :, VA_COL0:VA_COL0 + SWA_KV_HEADS * HEAD_DIM], 1, 2)
    va = jnp.broadcast_to(va.reshape(depth, SWA_KV_HEADS, 1, HEAD_DIM, d), (depth, SWA_KV_HEADS, 2, HEAD_DIM, d))
    return jnp.concatenate([vd, va.reshape(depth, SWA_KV_HEADS * LANES, d)], axis=1)


def _first_of_pair_mask():
    lane = lax.broadcasted_iota(jnp.int32, (1, LANES), 1)
    return (lane // HALF_DIM) % 2 == 0


def _silu(g):
    return g / (1.0 + jnp.exp(-g))


def _params(sem):
    return pltpu.CompilerParams(dimension_semantics=sem, vmem_limit_bytes=VMEM_LIMIT)


def _adaln_kernel(c_ref, w_ref, b_ref, o_ref):
    a = _silu(c_ref[...])
    o_ref[0] = jnp.dot(a, w_ref[0], preferred_element_type=F32,
                       precision=lax.Precision.HIGHEST) + b_ref[0]


def _adaln(c_pad, ada_w, ada_b):
    depth, d, cols = ada_w.shape
    tn = 1536
    return pl.pallas_call(
        _adaln_kernel,
        grid=(depth, cols // tn),
        in_specs=[pl.BlockSpec((8, d), lambda l, j: (0, 0)),
                  pl.BlockSpec((1, d, tn), lambda l, j: (l, 0, j)),
                  pl.BlockSpec((1, 1, tn), lambda l, j: (l, 0, j))],
        out_specs=pl.BlockSpec((1, 8, tn), lambda l, j: (l, 0, j)),
        out_shape=jax.ShapeDtypeStruct((depth, 8, cols), F32),
        compiler_params=_params(("arbitrary", "arbitrary")),
        name="adaln",
    )(c_pad, ada_w, ada_b.reshape(depth, 1, cols))


def _rope_table_kernel(pos_ref, cos_ref, sin_ref):
    lane = lax.broadcasted_iota(jnp.int32, (1, LANES), 1)
    freq = (lane % HALF_DIM).astype(F32)
    inv = 1.0 / (ROPE_THETA ** (2.0 * freq / HEAD_DIM))
    ang = pos_ref[...].astype(F32) * inv
    sign = jnp.where(lane < HEAD_DIM, -1.0, 1.0)
    cos_ref[...] = jnp.cos(ang)
    sin_ref[...] = jnp.sin(ang) * sign


def _rope_tables(positions):
    n = positions.size
    tm = 1024
    pos = positions.reshape(n, 1)
    return pl.pallas_call(
        _rope_table_kernel,
        grid=(n // tm,),
        in_specs=[pl.BlockSpec((tm, 1), lambda i: (i, 0))],
        out_specs=[pl.BlockSpec((tm, LANES), lambda i: (i, 0))] * 2,
        out_shape=[jax.ShapeDtypeStruct((n, LANES), F32)] * 2,
        compiler_params=_params(("arbitrary",)),
        name="rope_tables",
    )(pos)


def _rms_mod(x, g, scale, shift):
    y = x * lax.rsqrt(jnp.mean(x * x, axis=-1, keepdims=True) + EPS) * g
    return y * (1.0 + scale) + shift


def _inproj_kernel(x_ref, mod_ref, g_ref, w_ref, wvt_ref, cos_ref, sin_ref, o_ref, vt_ref):
    h = _rms_mod(x_ref[...], g_ref[...], mod_ref[0, 1:2, :], mod_ref[0, 0:1, :]).astype(BF16)
    proj = jnp.dot(h, w_ref[...], preferred_element_type=F32)
    vt_ref[...] = lax.dot_general(wvt_ref[...], h, (((1,), (1,)), ((), ())),
                                  preferred_element_type=F32).astype(BF16)
    cos = cos_ref[...]
    sin = sin_ref[...]
    for blk in range(ROPE_BLKS):
        t = proj[:, blk * LANES:(blk + 1) * LANES]
        r = t * cos + pltpu.roll(t, HEAD_DIM, axis=1) * sin
        if blk < KA_BLK or QD_BLK <= blk < KD_BLK:
            r = r * (Q_SCALE * LOG2E)
        o_ref[:, blk * LANES:(blk + 1) * LANES] = r.astype(BF16)


def _inproj(x2, mod, g, w, wvt, layer, cos_t, sin_t, seq):
    n, d = x2.shape
    tm = TM_PROJ
    per_b = seq // tm
    vt_rows = wvt.shape[1]
    return pl.pallas_call(
        _inproj_kernel,
        grid=(n // tm,),
        in_specs=[pl.BlockSpec((tm, d), lambda i: (i, 0)),
                  pl.BlockSpec((1, N_MOD, d), lambda i: (i // per_b, 0, 0)),
                  pl.BlockSpec((1, d), lambda i: (0, 0)),
                  pl.BlockSpec((None, d, PROJ_COLS), lambda i: (layer, 0, 0)),
                  pl.BlockSpec((None, vt_rows, d), lambda i: (layer, 0, 0)),
                  pl.BlockSpec((tm, LANES), lambda i: (i, 0)),
                  pl.BlockSpec((tm, LANES), lambda i: (i, 0))],
        out_specs=[pl.BlockSpec((tm, PROJ_COLS), lambda i: (i, 0)),
                   pl.BlockSpec((vt_rows, tm), lambda i: (0, i))],
        out_shape=[jax.ShapeDtypeStruct((n, PROJ_COLS), BF16),
                   jax.ShapeDtypeStruct((vt_rows, n), BF16)],
        compiler_params=_params(("arbitrary",)),
        name="inproj",
    )(x2, mod, g, w, wvt, cos_t, sin_t)


def _win_kernel(sink_ref, q_ref, k_ref, vt_ref, o_ref, qt_sc, *, tq, seq):
    g = pl.program_id(1)
    i = pl.program_id(2)
    heads = SWA_Q_HEADS // SWA_KV_HEADS
    width = tq + 2 * WINDOW
    start = pl.multiple_of(jnp.clip(i * tq - WINDOW, 0, seq - width), LANES)
    first = _first_of_pair_mask()
    for blk in range(heads // 2):
        q = q_ref[:, blk * LANES:(blk + 1) * LANES].astype(F32)
        zero = jnp.zeros_like(q)
        qt_sc[:, (2 * blk) * tq:(2 * blk + 1) * tq] = jnp.where(first, q, zero).T.astype(BF16)
        qt_sc[:, (2 * blk + 1) * tq:(2 * blk + 2) * tq] = jnp.where(first, zero, q).T.astype(BF16)

    s = jnp.dot(k_ref[pl.ds(start, width), :], qt_sc[...], preferred_element_type=F32)
    kpos = start + lax.broadcasted_iota(jnp.int32, (width, tq), 0)
    qpos = i * tq + lax.broadcasted_iota(jnp.int32, (width, tq), 1)
    valid = jnp.abs(qpos - kpos) <= WINDOW
    s = jnp.concatenate([jnp.where(valid, s[:, c * tq:(c + 1) * tq], -1e30) for c in range(heads)], axis=1)
    col = lax.broadcasted_iota(jnp.int32, (1, heads * tq), 1)
    sk = jnp.zeros((1, heads * tq), F32)
    for c in range(heads):
        sk = jnp.where(col // tq == c, sink_ref[heads * g + c] * LOG2E, sk)
    m = jnp.maximum(jnp.max(s, axis=0, keepdims=True), sk)
    p = jnp.exp2(s - m)
    den = jnp.sum(p, axis=0, keepdims=True) + jnp.exp2(sk - m)
    o = jnp.dot(vt_ref[:, pl.ds(start, width)], p.astype(BF16), preferred_element_type=F32) / den
    lane = lax.broadcasted_iota(jnp.int32, (1, LANES), 1)
    for blk in range(heads // 2):
        oa = o[:, (2 * blk) * tq:(2 * blk + 1) * tq].T
        ob = o[:, (2 * blk + 1) * tq:(2 * blk + 2) * tq].T
        o_ref[:, blk * LANES:(blk + 1) * LANES] = jnp.where(lane < HEAD_DIM, oa, ob).astype(o_ref.dtype)


def _window_attention(proj, vt, sink, batch, seq):
    n = proj.shape[0]
    tq = TQ_WIN
    nq = seq // tq
    heads = SWA_Q_HEADS // SWA_KV_HEADS
    cols = heads // 2 * LANES
    kern = functools.partial(_win_kernel, tq=tq, seq=seq)
    return pl.pallas_call(
        kern,
        grid=(batch, SWA_KV_HEADS, nq),
        in_specs=[pl.BlockSpec(memory_space=pltpu.SMEM),
                  pl.BlockSpec((tq, cols), lambda b, g, i: (b * nq + i, g)),
                  pl.BlockSpec((seq, LANES), lambda b, g, i: (b, KA_BLK + g)),
                  pl.BlockSpec((LANES, seq), lambda b, g, i: (VT_WIN_BLK + g, b))],
        out_specs=pl.BlockSpec((tq, cols), lambda b, g, i: (b * nq + i, g)),
        out_shape=jax.ShapeDtypeStruct((n, SWA_KV_HEADS * cols), BF16),
        scratch_shapes=[pltpu.VMEM((LANES, heads * tq), BF16)],
        compiler_params=_params(("arbitrary", "arbitrary", "arbitrary")),
        name="window_attn",
    )(sink, proj, proj, vt)


def _diff_kernel(lam_ref, g_ref, q_ref, k_ref, vt_ref, o_ref, qt_sc, s_a, s_b, cm_a, cm_b, m_sc, l_sc, acc_sc,
                 *, tk, seq, lambda_init):
    tq = q_ref.shape[0]
    nk = seq // tk
    q = q_ref[...].astype(F32)
    first = _first_of_pair_mask()
    zero = jnp.zeros_like(q)
    qt_sc[:, :tq] = jnp.where(first, q, zero).T.astype(BF16)
    qt_sc[:, tq:] = jnp.where(first, zero, q).T.astype(BF16)

    m_sc[...] = jnp.full(m_sc.shape, -1e30, F32)
    l_sc[...] = jnp.zeros(l_sc.shape, F32)
    acc_sc[...] = jnp.zeros(acc_sc.shape, F32)

    def scores(t, s_ref, cm_ref):
        off = pl.multiple_of(t * tk, tk)
        s = jnp.dot(k_ref[pl.ds(off, tk), :], qt_sc[...], preferred_element_type=F32)
        s_ref[...] = s
        cm_ref[...] = jnp.max(s, axis=0, keepdims=True)

    def consume(t, s_ref, cm_ref):
        off = pl.multiple_of(t * tk, tk)
        m_prev = m_sc[...]
        m_new = jnp.maximum(m_prev, cm_ref[...])
        alpha = jnp.exp2(m_prev - m_new)
        p = jnp.exp2(s_ref[...] - m_new)
        l_sc[...] = alpha * l_sc[...] + jnp.sum(p, axis=0, keepdims=True)
        vb = vt_ref[:, pl.ds(off, tk)]
        acc_sc[...] = alpha * acc_sc[...] + jnp.dot(vb, p.astype(BF16), preferred_element_type=F32)
        m_sc[...] = m_new

    scores(0, s_a, cm_a)

    def body(u, carry):
        t = 2 * u
        scores(t + 1, s_b, cm_b)
        consume(t, s_a, cm_a)
        scores(t + 2, s_a, cm_a)
        consume(t + 1, s_b, cm_b)
        return carry

    lax.fori_loop(0, nk // 2 - 1, body, 0)
    scores(nk - 1, s_b, cm_b)
    consume(nk - 2, s_a, cm_a)
    consume(nk - 1, s_b, cm_b)

    lam1 = jnp.sum(lam_ref[0:1, :] * lam_ref[1:2, :], axis=-1, keepdims=True)
    lam2 = jnp.sum(lam_ref[2:3, :] * lam_ref[3:4, :], axis=-1, keepdims=True)
    lam = jnp.exp(lam1) - jnp.exp(lam2) + lambda_init
    o = acc_sc[...] / l_sc[...]
    o = o[:, :tq] - lam * o[:, tq:]
    y = o * lax.rsqrt(jnp.mean(o * o, axis=0, keepdims=True) + EPS)
    y = y.T * (g_ref[...] * (1.0 - lambda_init))
    o_ref[...] = y.astype(o_ref.dtype)


def _diff_attention(proj, vt, lam_vecs, subln_g, batch, seq, lambda_init):
    n = proj.shape[0]
    tq, tk = TQ_DIFF, TK_DIFF
    nq = seq // tq
    assert seq % (2 * tk) == 0 and seq % tq == 0, "the pipelined key loop consumes key blocks in pairs"
    kern = functools.partial(_diff_kernel, tk=tk, seq=seq, lambda_init=lambda_init)
    return pl.pallas_call(
        kern,
        grid=(batch, DIFF_HEADS, nq),
        in_specs=[pl.BlockSpec((4, HEAD_DIM), lambda b, h, i: (0, 0)),
                  pl.BlockSpec((1, LANES), lambda b, h, i: (0, 0)),
                  pl.BlockSpec((tq, LANES), lambda b, h, i: (b * nq + i, QD_BLK + h)),
                  pl.BlockSpec((seq, LANES), lambda b, h, i: (b, KD_BLK + h)),
                  pl.BlockSpec((LANES, seq), lambda b, h, i: (h, b))],
        out_specs=pl.BlockSpec((tq, LANES), lambda b, h, i: (b * nq + i, h)),
        out_shape=jax.ShapeDtypeStruct((n, DIFF_HEADS * LANES), BF16),
        scratch_shapes=[pltpu.VMEM((LANES, 2 * tq), BF16),
                        pltpu.VMEM((tk, 2 * tq), F32), pltpu.VMEM((tk, 2 * tq), F32),
                        pltpu.VMEM((1, 2 * tq), F32), pltpu.VMEM((1, 2 * tq), F32),
                        pltpu.VMEM((1, 2 * tq), F32), pltpu.VMEM((1, 2 * tq), F32),
                        pltpu.VMEM((LANES, 2 * tq), F32)],
        compiler_params=_params(("arbitrary", "arbitrary", "arbitrary")),
        name="diff_attn",
    )(lam_vecs, subln_g, proj, proj, vt)


def _route_top2(logits):
    lane = lax.broadcasted_iota(jnp.int32, logits.shape, 1)
    lg = jnp.where(lane < N_EXPERTS, logits, -jnp.inf)
    v1 = jnp.max(lg, axis=-1, keepdims=True)
    i1 = jnp.min(jnp.where(lg == v1, lane, LANES), axis=-1, keepdims=True)
    lg2 = jnp.where(lane == i1, -jnp.inf, lg)
    v2 = jnp.max(lg2, axis=-1, keepdims=True)
    i2 = jnp.min(jnp.where(lg2 == v2, lane, LANES), axis=-1, keepdims=True)
    e = jnp.exp(v2 - v1)
    w1 = 1.0 / (1.0 + e)
    w2 = e / (1.0 + e)
    idx = jnp.where(lane == 0, i1, jnp.where(lane == 1, i2, 0))
    wgt = jnp.where(lane == 0, w1, jnp.where(lane == 1, w2, 0.0))
    return idx, wgt


def _outproj_kernel(ya_ref, yd_ref, w_ref, x_ref, mod_ref, g_ref, *rest, moe):
    if moe:
        rwh_ref, rwl_ref, x1_ref, h_ref, idx_ref, wgt_ref = rest
    else:
        x1_ref, h_ref = rest
    half = ya_ref.shape[1]
    y = (jnp.dot(ya_ref[...], w_ref[:half, :], preferred_element_type=F32)
         + jnp.dot(yd_ref[...], w_ref[half:, :], preferred_element_type=F32))
    x1 = x_ref[...] + mod_ref[0, 2:3, :] * y
    x1_ref[...] = x1
    h = _rms_mod(x1, g_ref[...], mod_ref[0, 4:5, :], mod_ref[0, 3:4, :])
    if moe:
        for c in range(SUBROWS):
            h_ref[pl.ds(c, h.shape[0], stride=SUBROWS), :] = h[:, c * LANES:(c + 1) * LANES]
    else:
        h_ref[...] = h.astype(h_ref.dtype)
    if moe:
        h_hi = h.astype(BF16)
        h_lo = (h - h_hi.astype(F32)).astype(BF16)
        logits = (jnp.dot(h_hi, rwh_ref[...], preferred_element_type=F32)
                  + jnp.dot(h_hi, rwl_ref[...], preferred_element_type=F32)
                  + jnp.dot(h_lo, rwh_ref[...], preferred_element_type=F32))
        idx, wgt = _route_top2(logits)
        idx_ref[...] = idx
        wgt_ref[...] = wgt


def _outproj(ya, yd, w, layer, x2, mod, g, seq, router_w=None):
    n, d = x2.shape
    tm = TM_PROJ
    per_b = seq // tm
    moe = router_w is not None
    row = lambda i: (i, 0)
    in_specs = [pl.BlockSpec((tm, ya.shape[1]), row),
                pl.BlockSpec((tm, yd.shape[1]), row),
                pl.BlockSpec((None,) + w.shape[1:], lambda i: (layer, 0, 0)),
                pl.BlockSpec((tm, d), row),
                pl.BlockSpec((1, N_MOD, d), lambda i: (i // per_b, 0, 0)),
                pl.BlockSpec((1, d), lambda i: (0, 0))]
    out_specs = [pl.BlockSpec((tm, d), row), pl.BlockSpec(_to_subrows((tm, d)) if moe else (tm, d), row)]
    out_shape = [jax.ShapeDtypeStruct((n, d), F32),
                 jax.ShapeDtypeStruct(_to_subrows((n, d)), F32) if moe else jax.ShapeDtypeStruct((n, d), BF16)]
    args = [ya, yd, w, x2, mod, g]
    if moe:
        in_specs += [pl.BlockSpec((d, LANES), lambda i: (0, 0))] * 2
        out_specs += [pl.BlockSpec((tm, LANES), row)] * 2
        out_shape += [jax.ShapeDtypeStruct((n, LANES), jnp.int32), jax.ShapeDtypeStruct((n, LANES), F32)]
        args += list(router_w)
    return pl.pallas_call(
        functools.partial(_outproj_kernel, moe=moe),
        grid=(n // tm,),
        in_specs=in_specs, out_specs=out_specs, out_shape=out_shape,
        compiler_params=_params(("arbitrary",)),
        name="outproj_moe" if moe else "outproj",
    )(*args)


def _swiglu_step(h, wg_ref, wu_ref, wd_ref):
    g = jnp.dot(h, wg_ref[...].astype(BF16), preferred_element_type=F32)
    u = jnp.dot(h, wu_ref[...].astype(BF16), preferred_element_type=F32)
    return jnp.dot((_silu(g) * u).astype(BF16), wd_ref[...].astype(BF16), preferred_element_type=F32)


def _ffn_kernel(h_ref, wg_ref, wu_ref, wd_ref, x_ref, mod_ref, o_ref, acc_ref):
    j = pl.program_id(1)

    @pl.when(j == 0)
    def _():
        acc_ref[...] = jnp.zeros(acc_ref.shape, F32)

    acc_ref[...] += _swiglu_step(h_ref[...], wg_ref, wu_ref, wd_ref)

    @pl.when(j == pl.num_programs(1) - 1)
    def _():
        o_ref[...] = x_ref[...] + mod_ref[0, 5:6, :] * acc_ref[...]


def _dense_ffn(h, wg, wu, wd, li, x1, mod, seq):
    n, d = x1.shape
    f = wg.shape[2]
    tm, tf = TM_FFN, TF_FFN
    per_b = seq // tm
    return pl.pallas_call(
        _ffn_kernel,
        grid=(n // tm, f // tf),
        in_specs=[pl.BlockSpec((tm, d), lambda i, j: (i, 0)),
                  pl.BlockSpec((None, d, tf), lambda i, j: (li, 0, j)),
                  pl.BlockSpec((None, d, tf), lambda i, j: (li, 0, j)),
                  pl.BlockSpec((None, tf, d), lambda i, j: (li, j, 0)),
                  pl.BlockSpec((tm, d), lambda i, j: (i, 0)),
                  pl.BlockSpec((1, N_MOD, d), lambda i, j: (i // per_b, 0, 0))],
        out_specs=pl.BlockSpec((tm, d), lambda i, j: (i, 0)),
        out_shape=jax.ShapeDtypeStruct((n, d), F32),
        scratch_shapes=[pltpu.VMEM((tm, d), F32)],
        compiler_params=_params(("arbitrary", "arbitrary")),
        name="dense_ffn",
    )(h, wg, wu, wd, x1, mod)


def _moe_plan(idx, tm):
    n = idx.shape[0]
    flat_e = idx.reshape(-1)
    onehot = (flat_e[:, None] == jnp.arange(N_EXPERTS, dtype=jnp.int32)[None, :]).astype(jnp.int32)
    csum = jnp.cumsum(onehot, axis=0)
    rank = jnp.sum((csum - onehot) * onehot, axis=1)
    counts = csum[-1]
    padded = (counts + tm - 1) // tm * tm
    pad_ends = jnp.cumsum(padded)
    pad_starts = pad_ends - padded
    dest = (pad_starts[flat_e] + rank).astype(jnp.int32)
    rows = 2 * n + N_EXPERTS * tm
    n_tiles = (pad_ends[-1] // tm).astype(jnp.int32).reshape(1)
    tile_start = jnp.arange(rows // tm, dtype=jnp.int32) * tm
    tile_e = jnp.minimum(jnp.sum((pad_ends[None, :] <= tile_start[:, None]).astype(jnp.int32), axis=1),
                         N_EXPERTS - 1).astype(jnp.int32)
    return dest, pad_ends.astype(jnp.int32), tile_e, n_tiles, rows


def _to_subrows(ref_or_shape):
    rows, d = ref_or_shape
    assert d == SUBROWS * LANES
    return (rows * SUBROWS, LANES)


def _row_copy(src_hbm, src_row, dst_ref, dst_row, sem):
    s = pl.multiple_of(src_row * SUBROWS, SUBROWS)
    t = pl.multiple_of(dst_row * SUBROWS, SUBROWS)
    return pltpu.make_async_copy(src_hbm.at[pl.ds(s, SUBROWS)], dst_ref.at[pl.ds(t, SUBROWS)], sem)


def _row_gather(idx_of, src_hbm, dst_ref, sem, rows):
    def issue(r, carry):
        _row_copy(src_hbm, idx_of(r), dst_ref, r, sem).start()
        return carry

    lax.fori_loop(0, rows, issue, 0, unroll=8)


def _row_gather_wait(src_hbm, dst_ref, sem, rows):
    pltpu.make_async_copy(src_hbm.at[pl.ds(0, rows * SUBROWS)], dst_ref, sem).wait()


def _moe_scatter_kernel(dest_ref, pe_ref, nt_ref, h_ref, xs_hbm, zeros, sem, *, tb, tm, n_tiles_max):
    i = pl.program_id(0)
    zero_sem, row_sem = sem.at[0], sem.at[1]
    tile_rows = tm * SUBROWS

    def zero_tile(t):
        start = pl.multiple_of(t * tile_rows, tile_rows)
        return pltpu.make_async_copy(zeros, xs_hbm.at[pl.ds(start, tile_rows)], zero_sem)

    @pl.when(i == 0)
    def _():
        zeros[...] = jnp.zeros(zeros.shape, zeros.dtype)
        for e in range(N_EXPERTS):
            last_tile = jnp.maximum(pe_ref[e] // tm - 1, 0)
            zero_tile(last_tile).start()
            zero_tile(last_tile).wait()
        for t in range(n_tiles_max):
            @pl.when(t >= nt_ref[0])
            def _():
                zero_tile(t).start()
                zero_tile(t).wait()

    base = i * tb

    def issue(r, carry):
        src = h_ref.at[pl.ds(pl.multiple_of(r * SUBROWS, SUBROWS), SUBROWS)]
        for k in range(2):
            d = dest_ref[2 * (base + r) + k]
            pltpu.make_async_copy(src, xs_hbm.at[pl.ds(pl.multiple_of(d * SUBROWS, SUBROWS), SUBROWS)],
                                  row_sem).start()
        return carry

    lax.fori_loop(0, tb, issue, 0, unroll=4)
    for k in range(2):
        pltpu.make_async_copy(h_ref, xs_hbm.at[pl.ds(0, tb * SUBROWS)], row_sem).wait()


def _moe_scatter(h_sub, dest, pad_ends, n_tiles, rows_total, tm):
    n_sub = h_sub.shape[0]
    tb = TM_COMB
    d = SUBROWS * LANES
    n_tiles_max = rows_total // tm
    return pl.pallas_call(
        functools.partial(_moe_scatter_kernel, tb=tb, tm=tm, n_tiles_max=n_tiles_max),
        grid_spec=pltpu.PrefetchScalarGridSpec(
            num_scalar_prefetch=3,
            grid=(n_sub // (tb * SUBROWS),),
            in_specs=[pl.BlockSpec(_to_subrows((tb, d)), lambda i, de, pe, nt: (i, 0))],
            out_specs=pl.BlockSpec(memory_space=pl.ANY),
            scratch_shapes=[pltpu.VMEM(_to_subrows((tm, d)), F32), pltpu.SemaphoreType.DMA((2,))]),
        out_shape=jax.ShapeDtypeStruct(_to_subrows((rows_total, d)), F32),
        compiler_params=_params(("arbitrary",)),
        name="moe_scatter",
    )(dest, pad_ends, n_tiles, h_sub)


def _moe_ffn_kernel(te_ref, nt_ref, xs_ref, wg_ref, wu_ref, wd_ref, o_ref, xb16, acc_ref, *, nf):
    i = pl.program_id(0)
    j = pl.program_id(1)
    tm = xb16.shape[0]

    @pl.when(i < nt_ref[0])
    def _():
        @pl.when(j == 0)
        def _():
            for c in range(SUBROWS):
                xb16[:, c * LANES:(c + 1) * LANES] = xs_ref[pl.ds(c, tm, stride=SUBROWS), :].astype(BF16)
            acc_ref[...] = jnp.zeros(acc_ref.shape, F32)

        acc_ref[...] += _swiglu_step(xb16[...], wg_ref.at[0], wu_ref.at[0], wd_ref.at[0])

        @pl.when(j == nf - 1)
        def _():
            for c in range(SUBROWS):
                o_ref[pl.ds(c, tm, stride=SUBROWS), :] = acc_ref[:, c * LANES:(c + 1) * LANES]

    @pl.when(jnp.logical_and(i >= nt_ref[0], j == nf - 1))
    def _():
        o_ref[...] = jnp.zeros(o_ref.shape, o_ref.dtype)


def _moe_ffn(xs, wg, wu, wd, li, tile_e, n_tiles, tm):
    d = wg.shape[2]
    f = wg.shape[3]
    rows_total = xs.shape[0] // SUBROWS
    tf = TF_FFN
    nf = f // tf

    def row_map(i, j, te, nt):
        return (jnp.minimum(i, nt[0] - 1), 0)

    def _fj(i, j, nt):
        return jnp.where(i < nt[0], j, nf - 1)

    def up_map(i, j, te, nt):
        return (li, te[jnp.minimum(i, nt[0] - 1)], 0, _fj(i, j, nt))

    def down_map(i, j, te, nt):
        return (li, te[jnp.minimum(i, nt[0] - 1)], _fj(i, j, nt), 0)

    return pl.pallas_call(
        functools.partial(_moe_ffn_kernel, nf=nf),
        grid_spec=pltpu.PrefetchScalarGridSpec(
            num_scalar_prefetch=2,
            grid=(rows_total // tm, nf),
            in_specs=[pl.BlockSpec(_to_subrows((tm, d)), row_map),
                      pl.BlockSpec((None, 1, d, tf), up_map),
                      pl.BlockSpec((None, 1, d, tf), up_map),
                      pl.BlockSpec((None, 1, tf, d), down_map)],
            out_specs=pl.BlockSpec(_to_subrows((tm, d)), lambda i, j, te, nt: (i, 0)),
            scratch_shapes=[pltpu.VMEM((tm, d), BF16), pltpu.VMEM((tm, d), F32)]),
        out_shape=jax.ShapeDtypeStruct(_to_subrows((rows_total, d)), F32),
        compiler_params=_params(("arbitrary", "arbitrary")),
        name="moe_ffn",
    )(tile_e, n_tiles, xs, wg, wu, wd)


def _combine_kernel(pos_ref, ys_hbm, wgt_ref, x_ref, mod_ref, fg_ref, o_ref, buf, sem, *, tm, final):
    i = pl.program_id(0)
    slot = i % 2

    def gather(step, to_slot):
        for k in range(2):
            _row_gather(lambda r: pos_ref[2 * (step * tm + r) + k], ys_hbm, buf.at[to_slot, k],
                        sem.at[to_slot], tm)

    @pl.when(i == 0)
    def _():
        gather(0, 0)

    @pl.when(i + 1 < pl.num_programs(0))
    def _():
        gather(i + 1, 1 - slot)

    for k in range(2):
        _row_gather_wait(ys_hbm, buf.at[slot, k], sem.at[slot], tm)

    wgt = wgt_ref[...]
    w0, w1 = wgt[:, 0:1], wgt[:, 1:2]
    y = jnp.concatenate([w0 * buf[slot, 0, pl.ds(c, tm, stride=SUBROWS), :]
                         + w1 * buf[slot, 1, pl.ds(c, tm, stride=SUBROWS), :] for c in range(SUBROWS)], axis=1)
    out = x_ref[...] + mod_ref[0, 5:6, :] * y
    if final:
        out = out * lax.rsqrt(jnp.mean(out * out, axis=-1, keepdims=True) + EPS) * fg_ref[...]
    o_ref[...] = out


def _moe_combine(ys, pos, wgt, x1, mod, final_g, seq, final):
    n, d = x1.shape
    tm = TM_COMB
    per_b = seq // tm
    return pl.pallas_call(
        functools.partial(_combine_kernel, tm=tm, final=final),
        grid_spec=pltpu.PrefetchScalarGridSpec(
            num_scalar_prefetch=1,
            grid=(n // tm,),
            in_specs=[pl.BlockSpec(memory_space=pl.ANY),
                      pl.BlockSpec((tm, LANES), lambda i, p: (i, 0)),
                      pl.BlockSpec((tm, d), lambda i, p: (i, 0)),
                      pl.BlockSpec((1, N_MOD, d), lambda i, p: (i // per_b, 0, 0)),
                      pl.BlockSpec((1, d), lambda i, p: (0, 0))],
            out_specs=pl.BlockSpec((tm, d), lambda i, p: (i, 0)),
            scratch_shapes=[pltpu.VMEM((2, 2) + _to_subrows((tm, d)), F32),
                            pltpu.SemaphoreType.DMA((2,))]),
        out_shape=jax.ShapeDtypeStruct((n, d), F32),
        compiler_params=_params(("arbitrary",)),
        name="moe_combine_final" if final else "moe_combine",
    )(pos.reshape(-1), ys, wgt, x1, mod, final_g)


def _final_norm_kernel(x_ref, g_ref, o_ref):
    x = x_ref[...]
    o_ref[...] = x * lax.rsqrt(jnp.mean(x * x, axis=-1, keepdims=True) + EPS) * g_ref[...]


def _final_norm(x2, g):
    n, d = x2.shape
    tm = TM_PROJ
    return pl.pallas_call(
        _final_norm_kernel,
        grid=(n // tm,),
        in_specs=[pl.BlockSpec((tm, d), lambda i: (i, 0)), pl.BlockSpec((1, d), lambda i: (0, 0))],
        out_specs=pl.BlockSpec((tm, d), lambda i: (i, 0)),
        out_shape=jax.ShapeDtypeStruct((n, d), F32),
        compiler_params=_params(("arbitrary",)),
        name="final_norm",
    )(x2, g)


def kernel(x, c, positions, ada_w, ada_b, norm_mix_g, norm_ffn_g, w_in, w_out, sink, lam_q1, lam_k1, lam_q2, lam_k2, subln_g, ffn_w_gate, ffn_w_up, ffn_w_down, router_w, moe_w_gate, moe_w_up, moe_w_down, final_g):
    batch, seq, d = x.shape
    depth = w_in.shape[0]
    n = batch * seq

    c_pad = jnp.zeros((8, d), F32).at[:batch].set(c)
    mod_all = _adaln(c_pad, ada_w, ada_b)
    cos_t, sin_t = _rope_tables(positions)
    w_in_p = _permute_in_proj(w_in).astype(BF16)
    w_vdt = _value_rows(w_in).astype(BF16)
    w_out_b = w_out.astype(BF16)
    ffn_w = [w.astype(BF16) for w in (ffn_w_gate, ffn_w_up, ffn_w_down)]
    moe_w = [moe_w_gate, moe_w_up, moe_w_down]
    final_g2 = final_g.reshape(1, d)

    x2 = x.reshape(n, d)
    for layer in range(depth):
        lambda_init = 0.8 - 0.6 * math.exp(-0.3 * layer)
        mod = mod_all[layer, :batch].reshape(batch, N_MOD, d)
        proj, vt = _inproj(x2, mod, norm_mix_g[layer].reshape(1, d), w_in_p, w_vdt, layer, cos_t, sin_t, seq)
        ya = _window_attention(proj, vt, sink[layer], batch, seq)
        lam_vecs = jnp.stack([lam_q1[layer], lam_k1[layer], lam_q2[layer], lam_k2[layer]])
        yd = _diff_attention(proj, vt, lam_vecs, subln_g[layer].reshape(1, LANES), batch, seq, lambda_init)
        i = layer // 2
        g_ffn = norm_ffn_g[layer].reshape(1, d)
        if layer % 2 == 0:
            x1, h = _outproj(ya, yd, w_out_b, layer, x2, mod, g_ffn, seq)
            x2 = _dense_ffn(h, *ffn_w, i, x1, mod, seq)
        else:
            rw = jnp.zeros((d, LANES), F32).at[:, :N_EXPERTS].set(router_w[i])
            rw_hi = rw.astype(BF16)
            rw_lo = (rw - rw_hi.astype(F32)).astype(BF16)
            x1, h, idx, wgt = _outproj(ya, yd, w_out_b, layer, x2, mod, g_ffn, seq, router_w=(rw_hi, rw_lo))
            dest, pad_ends, tile_e, n_tiles, rows_total = _moe_plan(idx[:, :2], TM_MOE)
            pos = dest.reshape(n, 2)
            xs = _moe_scatter(h, dest, pad_ends, n_tiles, rows_total, TM_MOE)
            ys = _moe_ffn(xs, *moe_w, i, tile_e, n_tiles, TM_MOE)
            final = layer == depth - 1
            x2 = _moe_combine(ys, pos, wgt, x1, mod, final_g2, seq, final)
    if depth % 2 == 1:
        x2 = _final_norm(x2, final_g2)
    return x2.reshape(batch, seq, d)
```

```python
import functools
import math

import jax
import jax.numpy as jnp
from jax import lax
from jax.experimental import pallas as pl
from jax.experimental.pallas import tpu as pltpu

F32 = jnp.float32
BF16 = jnp.bfloat16

HEAD_DIM = 64
HALF_DIM = HEAD_DIM // 2
ROPE_THETA = 10000.0
EPS = 1e-6
SWA_Q_HEADS = 8
SWA_KV_HEADS = 2
WINDOW = 128
DIFF_HEADS = 4
N_EXPERTS = 8
N_MOD = 6
LANES = 128
SUBROWS = 8
VMEM_LIMIT = 56 * 1024 * 1024

QA_BLK, KA_BLK, QD_BLK, KD_BLK = 0, 4, 6, 10
ROPE_BLKS = 14
PROJ_COLS = ROPE_BLKS * LANES
VA_COL0 = 640
VD_COL0 = 1792
VT_WIN_BLK = 4
Q_SCALE = HEAD_DIM ** -0.5
LOG2E = math.log2(math.e)

TM_PROJ = 512
TQ_WIN = 256
TQ_DIFF = 512
TK_DIFF = 1024
TM_FFN = 1024
TF_FFN = 512
TM_MOE = 1024
TM_COMB = 256


def _permute_in_proj(w_in):
    depth, d, _ = w_in.shape
    qa, ka, qd, kd = 0, 512, 768, 1280

    def pairs(lo, n_pairs):
        t = w_in[:, :, lo:lo + n_pairs * LANES].reshape(depth, d, n_pairs, 2, 2, HALF_DIM)
        return t.transpose(0, 1, 2, 4, 3, 5).reshape(depth, d, n_pairs * LANES)

    def dup_halves(lo):
        t = w_in[:, :, lo:lo + SWA_KV_HEADS * HEAD_DIM].reshape(depth, d, SWA_KV_HEADS, 2, 1, HALF_DIM)
        return jnp.broadcast_to(t, (depth, d, SWA_KV_HEADS, 2, 2, HALF_DIM)).reshape(depth, d, SWA_KV_HEADS * LANES)

    return jnp.concatenate([pairs(qa, 4), dup_halves(ka), pairs(qd, DIFF_HEADS), pairs(kd, DIFF_HEADS)], axis=-1)


def _value_rows(w_in):
    depth, d, _ = w_in.shape
    vd = jnp.swapaxes(w_in[:, :, VD_COL0:], 1, 2)
    va = jnp.swapaxes(w_in[:, :, VA_COL0:VA_COL0 + SWA_KV_HEADS * HEAD_DIM], 1, 2)
    va = jnp.broadcast_to(va.reshape(depth, SWA_KV_HEADS, 1, HEAD_DIM, d), (depth, SWA_KV_HEADS, 2, HEAD_DIM, d))
    return jnp.concatenate([vd, va.reshape(depth, SWA_KV_HEADS * LANES, d)], axis=1)


def _first_of_pair_mask():
    lane = lax.broadcasted_iota(jnp.int32, (1, LANES), 1)
    return (lane // HALF_DIM) % 2 == 0


def _silu(g):
    return g / (1.0 + jnp.exp(-g))


def _params(sem):
    return pltpu.CompilerParams(dimension_semantics=sem, vmem_limit_bytes=VMEM_LIMIT)


def _adaln_kernel(c_ref, w_ref, b_ref, o_ref):
    a = _silu(c_ref[...])
    o_ref[0] = jnp.dot(a, w_ref[0], preferred_element_type=F32,
                       precision=lax.Precision.HIGHEST) + b_ref[0]


def _adaln(c_pad, ada_w, ada_b):
    depth, d, cols = ada_w.shape
    tn = 1536
    return pl.pallas_call(
        _adaln_kernel,
        grid=(depth, cols // tn),
        in_specs=[pl.BlockSpec((8, d), lambda l, j: (0, 0)),
                  pl.BlockSpec((1, d, tn), lambda l, j: (l, 0, j)),
                  pl.BlockSpec((1, 1, tn), lambda l, j: (l, 0, j))],
        out_specs=pl.BlockSpec((1, 8, tn), lambda l, j: (l, 0, j)),
        out_shape=jax.ShapeDtypeStruct((depth, 8, cols), F32),
        compiler_params=_params(("arbitrary", "arbitrary")),
        name="adaln",
    )(c_pad, ada_w, ada_b.reshape(depth, 1, cols))


def _rope_table_kernel(pos_ref, cos_ref, sin_ref):
    lane = lax.broadcasted_iota(jnp.int32, (1, LANES), 1)
    freq = (lane % HALF_DIM).astype(F32)
    inv = 1.0 / (ROPE_THETA ** (2.0 * freq / HEAD_DIM))
    ang = pos_ref[...].astype(F32) * inv
    sign = jnp.where(lane < HEAD_DIM, -1.0, 1.0)
    cos_ref[...] = jnp.cos(ang)
    sin_ref[...] = jnp.sin(ang) * sign


def _rope_tables(positions):
    n = positions.size
    tm = 1024
    pos = positions.reshape(n, 1)
    return pl.pallas_call(
        _rope_table_kernel,
        grid=(n // tm,),
        in_specs=[pl.BlockSpec((tm, 1), lambda i: (i, 0))],
        out_specs=[pl.BlockSpec((tm, LANES), lambda i: (i, 0))] * 2,
        out_shape=[jax.ShapeDtypeStruct((n, LANES), F32)] * 2,
        compiler_params=_params(("arbitrary",)),
        name="rope_tables",
    )(pos)


def _rms_mod(x, g, scale, shift):
    y = x * lax.rsqrt(jnp.mean(x * x, axis=-1, keepdims=True) + EPS) * g
    return y * (1.0 + scale) + shift


def _inproj_kernel(x_ref, mod_ref, g_ref, w_ref, wvt_ref, cos_ref, sin_ref, o_ref, vt_ref):
    h = _rms_mod(x_ref[...], g_ref[...], mod_ref[0, 1:2, :], mod_ref[0, 0:1, :]).astype(BF16)
    proj = jnp.dot(h, w_ref[...], preferred_element_type=F32)
    vt_ref[...] = lax.dot_general(wvt_ref[...], h, (((1,), (1,)), ((), ())),
                                  preferred_element_type=F32).astype(BF16)
    cos = cos_ref[...]
    sin = sin_ref[...]
    for blk in range(ROPE_BLKS):
        t = proj[:, blk * LANES:(blk + 1) * LANES]
        r = t * cos + pltpu.roll(t, HEAD_DIM, axis=1) * sin
        if blk < KA_BLK or QD_BLK <= blk < KD_BLK:
            r = r * (Q_SCALE * LOG2E)
        o_ref[:, blk * LANES:(blk + 1) * LANES] = r.astype(BF16)


def _inproj(x2, mod, g, w, wvt, layer, cos_t, sin_t, seq):
    n, d = x2.shape
    tm = TM_PROJ
    per_b = seq // tm
    vt_rows = wvt.shape[1]
    return pl.pallas_call(
        _inproj_kernel,
        grid=(n // tm,),
        in_specs=[pl.BlockSpec((tm, d), lambda i: (i, 0)),
                  pl.BlockSpec((1, N_MOD, d), lambda i: (i // per_b, 0, 0)),
                  pl.BlockSpec((1, d), lambda i: (0, 0)),
                  pl.BlockSpec((None, d, PROJ_COLS), lambda i: (layer, 0, 0)),
                  pl.BlockSpec((None, vt_rows, d), lambda i: (layer, 0, 0)),
                  pl.BlockSpec((tm, LANES), lambda i: (i, 0)),
                  pl.BlockSpec((tm, LANES), lambda i: (i, 0))],
        out_specs=[pl.BlockSpec((tm, PROJ_COLS), lambda i: (i, 0)),
                   pl.BlockSpec((vt_rows, tm), lambda i: (0, i))],
        out_shape=[jax.ShapeDtypeStruct((n, PROJ_COLS), BF16),
                   jax.ShapeDtypeStruct((vt_rows, n), BF16)],
        compiler_params=_params(("arbitrary",)),
        name="inproj",
    )(x2, mod, g, w, wvt, cos_t, sin_t)


def _win_kernel(sink_ref, q_ref, k_ref, vt_ref, o_ref, qt_sc, *, tq, seq):
    i = pl.program_id(1)
    heads = SWA_Q_HEADS // SWA_KV_HEADS
    width = tq + 2 * WINDOW
    start = pl.multiple_of(jnp.clip(i * tq - WINDOW, 0, seq - width), LANES)
    first = _first_of_pair_mask()
    kpos = start + lax.broadcasted_iota(jnp.int32, (width, tq), 0)
    qpos = i * tq + lax.broadcasted_iota(jnp.int32, (width, tq), 1)
    valid = jnp.abs(qpos - kpos) <= WINDOW
    col = lax.broadcasted_iota(jnp.int32, (1, heads * tq), 1)
    lane = lax.broadcasted_iota(jnp.int32, (1, LANES), 1)
    for g in range(SWA_KV_HEADS):
        for blk in range(heads // 2):
            qb = (heads // 2) * g + blk
            q = q_ref[:, qb * LANES:(qb + 1) * LANES].astype(F32)
            zero = jnp.zeros_like(q)
            qt_sc[g, :, (2 * blk) * tq:(2 * blk + 1) * tq] = jnp.where(first, q, zero).T.astype(BF16)
            qt_sc[g, :, (2 * blk + 1) * tq:(2 * blk + 2) * tq] = jnp.where(first, zero, q).T.astype(BF16)

        s = jnp.dot(k_ref[pl.ds(start, width), g * LANES:(g + 1) * LANES], qt_sc[g],
                    preferred_element_type=F32)
        s = jnp.concatenate([jnp.where(valid, s[:, c * tq:(c + 1) * tq], -1e30) for c in range(heads)], axis=1)
        sk = jnp.zeros((1, heads * tq), F32)
        for c in range(heads):
            sk = jnp.where(col // tq == c, sink_ref[heads * g + c] * LOG2E, sk)
        m = jnp.maximum(jnp.max(s, axis=0, keepdims=True), sk)
        p = jnp.exp2(s - m)
        den = jnp.sum(p, axis=0, keepdims=True) + jnp.exp2(sk - m)
        o = jnp.dot(vt_ref[g * LANES:(g + 1) * LANES, pl.ds(start, width)], p.astype(BF16),
                    preferred_element_type=F32) / den
        for blk in range(heads // 2):
            qb = (heads // 2) * g + blk
            oa = o[:, (2 * blk) * tq:(2 * blk + 1) * tq].T
            ob = o[:, (2 * blk + 1) * tq:(2 * blk + 2) * tq].T
            o_ref[:, qb * LANES:(qb + 1) * LANES] = jnp.where(lane < HEAD_DIM, oa, ob).astype(o_ref.dtype)


def _window_attention(proj, vt, sink, batch, seq):
    n = proj.shape[0]
    tq = TQ_WIN
    nq = seq // tq
    heads = SWA_Q_HEADS // SWA_KV_HEADS
    q_cols = SWA_Q_HEADS // 2 * LANES
    kv_cols = SWA_KV_HEADS * LANES
    assert QA_BLK == 0 and KA_BLK * LANES % kv_cols == 0 and VT_WIN_BLK * LANES % kv_cols == 0
    kern = functools.partial(_win_kernel, tq=tq, seq=seq)
    return pl.pallas_call(
        kern,
        grid=(batch, nq),
        in_specs=[pl.BlockSpec(memory_space=pltpu.SMEM),
                  pl.BlockSpec((tq, q_cols), lambda b, i: (b * nq + i, 0)),
                  pl.BlockSpec((seq, kv_cols), lambda b, i: (b, KA_BLK * LANES // kv_cols)),
                  pl.BlockSpec((kv_cols, seq), lambda b, i: (VT_WIN_BLK * LANES // kv_cols, b))],
        out_specs=pl.BlockSpec((tq, q_cols), lambda b, i: (b * nq + i, 0)),
        out_shape=jax.ShapeDtypeStruct((n, q_cols), BF16),
        scratch_shapes=[pltpu.VMEM((SWA_KV_HEADS, LANES, heads * tq), BF16)],
        compiler_params=_params(("arbitrary", "arbitrary")),
        name="window_attn",
    )(sink, proj, proj, vt)


def _diff_kernel(lam_ref, g_ref, q_ref, k_ref, vt_ref, o_ref, qt_sc, s_a, s_b, cm_a, cm_b, m_sc, l_sc, acc_sc,
                 *, tk, seq, lambda_init):
    tq = q_ref.shape[0]
    nk = seq // tk
    q = q_ref[...].astype(F32)
    first = _first_of_pair_mask()
    zero = jnp.zeros_like(q)
    qt_sc[:, :tq] = jnp.where(first, q, zero).T.astype(BF16)
    qt_sc[:, tq:] = jnp.where(first, zero, q).T.astype(BF16)

    m_sc[...] = jnp.full(m_sc.shape, -1e30, F32)
    l_sc[...] = jnp.zeros(l_sc.shape, F32)
    acc_sc[...] = jnp.zeros(acc_sc.shape, F32)

    def scores(t, s_ref, cm_ref):
        off = pl.multiple_of(t * tk, tk)
        s = jnp.dot(k_ref[pl.ds(off, tk), :], qt_sc[...], preferred_element_type=F32)
        s_ref[...] = s
        cm_ref[...] = jnp.max(s, axis=0, keepdims=True)

    def consume(t, s_ref, cm_ref):
        off = pl.multiple_of(t * tk, tk)
        m_prev = m_sc[...]
        m_new = jnp.maximum(m_prev, cm_ref[...])
        alpha = jnp.exp2(m_prev - m_new)
        p = jnp.exp2(s_ref[...] - m_new)
        l_sc[...] = alpha * l_sc[...] + jnp.sum(p, axis=0, keepdims=True)
        vb = vt_ref[:, pl.ds(off, tk)]
        acc_sc[...] = alpha * acc_sc[...] + jnp.dot(vb, p.astype(BF16), preferred_element_type=F32)
        m_sc[...] = m_new

    scores(0, s_a, cm_a)

    def body(u, carry):
        t = 2 * u
        scores(t + 1, s_b, cm_b)
        consume(t, s_a, cm_a)
        scores(t + 2, s_a, cm_a)
        consume(t + 1, s_b, cm_b)
        return carry

    lax.fori_loop(0, nk // 2 - 1, body, 0)
    scores(nk - 1, s_b, cm_b)
    consume(nk - 2, s_a, cm_a)
    consume(nk - 1, s_b, cm_b)

    lam1 = jnp.sum(lam_ref[0:1, :] * lam_ref[1:2, :], axis=-1, keepdims=True)
    lam2 = jnp.sum(lam_ref[2:3, :] * lam_ref[3:4, :], axis=-1, keepdims=True)
    lam = jnp.exp(lam1) - jnp.exp(lam2) + lambda_init
    o = acc_sc[...] / l_sc[...]
    o = o[:, :tq] - lam * o[:, tq:]
    y = o * lax.rsqrt(jnp.mean(o * o, axis=0, keepdims=True) + EPS)
    y = y.T * (g_ref[...] * (1.0 - lambda_init))
    o_ref[...] = y.astype(o_ref.dtype)


def _diff_attention(proj, vt, lam_vecs, subln_g, batch, seq, lambda_init):
    n = proj.shape[0]
    tq, tk = TQ_DIFF, TK_DIFF
    nq = seq // tq
    assert seq % (2 * tk) == 0 and seq % tq == 0, "the pipelined key loop consumes key blocks in pairs"
    kern = functools.partial(_diff_kernel, tk=tk, seq=seq, lambda_init=lambda_init)
    return pl.pallas_call(
        kern,
        grid=(batch, DIFF_HEADS, nq),
        in_specs=[pl.BlockSpec((4, HEAD_DIM), lambda b, h, i: (0, 0)),
                  pl.BlockSpec((1, LANES), lambda b, h, i: (0, 0)),
                  pl.BlockSpec((tq, LANES), lambda b, h, i: (b * nq + i, QD_BLK + h)),
                  pl.BlockSpec((seq, LANES), lambda b, h, i: (b, KD_BLK + h)),
                  pl.BlockSpec((LANES, seq), lambda b, h, i: (h, b))],
        out_specs=pl.BlockSpec((tq, LANES), lambda b, h, i: (b * nq + i, h)),
        out_shape=jax.ShapeDtypeStruct((n, DIFF_HEADS * LANES), BF16),
        scratch_shapes=[pltpu.VMEM((LANES, 2 * tq), BF16),
                        pltpu.VMEM((tk, 2 * tq), F32), pltpu.VMEM((tk, 2 * tq), F32),
                        pltpu.VMEM((1, 2 * tq), F32), pltpu.VMEM((1, 2 * tq), F32),
                        pltpu.VMEM((1, 2 * tq), F32), pltpu.VMEM((1, 2 * tq), F32),
                        pltpu.VMEM((LANES, 2 * tq), F32)],
        compiler_params=_params(("arbitrary", "arbitrary", "arbitrary")),
        name="diff_attn",
    )(lam_vecs, subln_g, proj, proj, vt)


def _route_top2(logits):
    lane = lax.broadcasted_iota(jnp.int32, logits.shape, 1)
    lg = jnp.where(lane < N_EXPERTS, logits, -jnp.inf)
    v1 = jnp.max(lg, axis=-1, keepdims=True)
    i1 = jnp.min(jnp.where(lg == v1, lane, LANES), axis=-1, keepdims=True)
    lg2 = jnp.where(lane == i1, -jnp.inf, lg)
    v2 = jnp.max(lg2, axis=-1, keepdims=True)
    i2 = jnp.min(jnp.where(lg2 == v2, lane, LANES), axis=-1, keepdims=True)
    e = jnp.exp(v2 - v1)
    w1 = 1.0 / (1.0 + e)
    w2 = e / (1.0 + e)
    idx = jnp.where(lane == 0, i1, jnp.where(lane == 1, i2, 0))
    wgt = jnp.where(lane == 0, w1, jnp.where(lane == 1, w2, 0.0))
    return idx, wgt


def _outproj_kernel(ya_ref, yd_ref, w_ref, x_ref, mod_ref, g_ref, *rest, moe):
    if moe:
        rwh_ref, rwl_ref, x1_ref, h_ref, idx_ref, wgt_ref = rest
    else:
        x1_ref, h_ref = rest
    half = ya_ref.shape[1]
    y = (jnp.dot(ya_ref[...], w_ref[:half, :], preferred_element_type=F32)
         + jnp.dot(yd_ref[...], w_ref[half:, :], preferred_element_type=F32))
    x1 = x_ref[...] + mod_ref[0, 2:3, :] * y
    x1_ref[...] = x1
    h = _rms_mod(x1, g_ref[...], mod_ref[0, 4:5, :], mod_ref[0, 3:4, :])
    if moe:
        for c in range(SUBROWS):
            h_ref[pl.ds(c, h.shape[0], stride=SUBROWS), :] = h[:, c * LANES:(c + 1) * LANES]
    else:
        h_ref[...] = h.astype(h_ref.dtype)
    if moe:
        h_hi = h.astype(BF16)
        h_lo = (h - h_hi.astype(F32)).astype(BF16)
        logits = (jnp.dot(h_hi, rwh_ref[...], preferred_element_type=F32)
                  + jnp.dot(h_hi, rwl_ref[...], preferred_element_type=F32)
                  + jnp.dot(h_lo, rwh_ref[...], preferred_element_type=F32))
        idx, wgt = _route_top2(logits)
        idx_ref[...] = idx
        wgt_ref[...] = wgt


def _outproj(ya, yd, w, layer, x2, mod, g, seq, router_w=None):
    n, d = x2.shape
    tm = TM_PROJ
    per_b = seq // tm
    moe = router_w is not None
    row = lambda i: (i, 0)
    in_specs = [pl.BlockSpec((tm, ya.shape[1]), row),
                pl.BlockSpec((tm, yd.shape[1]), row),
                pl.BlockSpec((None,) + w.shape[1:], lambda i: (layer, 0, 0)),
                pl.BlockSpec((tm, d), row),
                pl.BlockSpec((1, N_MOD, d), lambda i: (i // per_b, 0, 0)),
                pl.BlockSpec((1, d), lambda i: (0, 0))]
    out_specs = [pl.BlockSpec((tm, d), row), pl.BlockSpec(_to_subrows((tm, d)) if moe else (tm, d), row)]
    out_shape = [jax.ShapeDtypeStruct((n, d), F32),
                 jax.ShapeDtypeStruct(_to_subrows((n, d)), F32) if moe else jax.ShapeDtypeStruct((n, d), BF16)]
    args = [ya, yd, w, x2, mod, g]
    if moe:
        in_specs += [pl.BlockSpec((d, LANES), lambda i: (0, 0))] * 2
        out_specs += [pl.BlockSpec((tm, LANES), row)] * 2
        out_shape += [jax.ShapeDtypeStruct((n, LANES), jnp.int32), jax.ShapeDtypeStruct((n, LANES), F32)]
        args += list(router_w)
    return pl.pallas_call(
        functools.partial(_outproj_kernel, moe=moe),
        grid=(n // tm,),
        in_specs=in_specs, out_specs=out_specs, out_shape=out_shape,
        compiler_params=_params(("arbitrary",)),
        name="outproj_moe" if moe else "outproj",
    )(*args)


def _swiglu_step(h, wg_ref, wu_ref, wd_ref):
    g = jnp.dot(h, wg_ref[...].astype(BF16), preferred_element_type=F32)
    u = jnp.dot(h, wu_ref[...].astype(BF16), preferred_element_type=F32)
    return jnp.dot((_silu(g) * u).astype(BF16), wd_ref[...].astype(BF16), preferred_element_type=F32)


def _ffn_kernel(h_ref, wg_ref, wu_ref, wd_ref, x_ref, mod_ref, o_ref, acc_ref):
    j = pl.program_id(1)

    @pl.when(j == 0)
    def _():
        acc_ref[...] = jnp.zeros(acc_ref.shape, F32)

    acc_ref[...] += _swiglu_step(h_ref[...], wg_ref, wu_ref, wd_ref)

    @pl.when(j == pl.num_programs(1) - 1)
    def _():
        o_ref[...] = x_ref[...] + mod_ref[0, 5:6, :] * acc_ref[...]


def _dense_ffn(h, wg, wu, wd, li, x1, mod, seq):
    n, d = x1.shape
    f = wg.shape[2]
    tm, tf = TM_FFN, TF_FFN
    per_b = seq // tm
    return pl.pallas_call(
        _ffn_kernel,
        grid=(n // tm, f // tf),
        in_specs=[pl.BlockSpec((tm, d), lambda i, j: (i, 0)),
                  pl.BlockSpec((None, d, tf), lambda i, j: (li, 0, j)),
                  pl.BlockSpec((None, d, tf), lambda i, j: (li, 0, j)),
                  pl.BlockSpec((None, tf, d), lambda i, j: (li, j, 0)),
                  pl.BlockSpec((tm, d), lambda i, j: (i, 0)),
                  pl.BlockSpec((1, N_MOD, d), lambda i, j: (i // per_b, 0, 0))],
        out_specs=pl.BlockSpec((tm, d), lambda i, j: (i, 0)),
        out_shape=jax.ShapeDtypeStruct((n, d), F32),
        scratch_shapes=[pltpu.VMEM((tm, d), F32)],
        compiler_params=_params(("arbitrary", "arbitrary")),
        name="dense_ffn",
    )(h, wg, wu, wd, x1, mod)


def _moe_plan(idx, tm):
    n = idx.shape[0]
    flat_e = idx.reshape(-1)
    onehot = (flat_e[:, None] == jnp.arange(N_EXPERTS, dtype=jnp.int32)[None, :]).astype(jnp.int32)
    csum = jnp.cumsum(onehot, axis=0)
    rank = jnp.sum((csum - onehot) * onehot, axis=1)
    counts = csum[-1]
    padded = (counts + tm - 1) // tm * tm
    pad_ends = jnp.cumsum(padded)
    pad_starts = pad_ends - padded
    dest = (pad_starts[flat_e] + rank).astype(jnp.int32)
    rows = 2 * n + N_EXPERTS * tm
    n_tiles = (pad_ends[-1] // tm).astype(jnp.int32).reshape(1)
    tile_start = jnp.arange(rows // tm, dtype=jnp.int32) * tm
    tile_e = jnp.minimum(jnp.sum((pad_ends[None, :] <= tile_start[:, None]).astype(jnp.int32), axis=1),
                         N_EXPERTS - 1).astype(jnp.int32)
    return dest, pad_ends.astype(jnp.int32), tile_e, n_tiles, rows


def _to_subrows(ref_or_shape):
    rows, d = ref_or_shape
    assert d == SUBROWS * LANES
    return (rows * SUBROWS, LANES)


def _row_copy(src_hbm, src_row, dst_ref, dst_row, sem):
    s = pl.multiple_of(src_row * SUBROWS, SUBROWS)
    t = pl.multiple_of(dst_row * SUBROWS, SUBROWS)
    return pltpu.make_async_copy(src_hbm.at[pl.ds(s, SUBROWS)], dst_ref.at[pl.ds(t, SUBROWS)], sem)


def _row_gather(idx_of, src_hbm, dst_ref, sem, rows):
    def issue(r, carry):
        _row_copy(src_hbm, idx_of(r), dst_ref, r, sem).start()
        return carry

    lax.fori_loop(0, rows, issue, 0, unroll=8)


def _row_gather_wait(src_hbm, dst_ref, sem, rows):
    pltpu.make_async_copy(src_hbm.at[pl.ds(0, rows * SUBROWS)], dst_ref, sem).wait()


def _moe_scatter_kernel(dest_ref, pe_ref, nt_ref, h_ref, xs_hbm, zeros, sem, *, tb, tm, n_tiles_max):
    i = pl.program_id(0)
    zero_sem, row_sem = sem.at[0], sem.at[1]
    tile_rows = tm * SUBROWS

    def zero_tile(t):
        start = pl.multiple_of(t * tile_rows, tile_rows)
        return pltpu.make_async_copy(zeros, xs_hbm.at[pl.ds(start, tile_rows)], zero_sem)

    @pl.when(i == 0)
    def _():
        zeros[...] = jnp.zeros(zeros.shape, zeros.dtype)
        for e in range(N_EXPERTS):
            last_tile = jnp.maximum(pe_ref[e] // tm - 1, 0)
            zero_tile(last_tile).start()
            zero_tile(last_tile).wait()
        for t in range(n_tiles_max):
            @pl.when(t >= nt_ref[0])
            def _():
                zero_tile(t).start()
                zero_tile(t).wait()

    base = i * tb

    def issue(r, carry):
        src = h_ref.at[pl.ds(pl.multiple_of(r * SUBROWS, SUBROWS), SUBROWS)]
        for k in range(2):
            d = dest_ref[2 * (base + r) + k]
            pltpu.make_async_copy(src, xs_hbm.at[pl.ds(pl.multiple_of(d * SUBROWS, SUBROWS), SUBROWS)],
                                  row_sem).start()
        return carry

    lax.fori_loop(0, tb, issue, 0, unroll=4)
    for k in range(2):
        pltpu.make_async_copy(h_ref, xs_hbm.at[pl.ds(0, tb * SUBROWS)], row_sem).wait()


def _moe_scatter(h_sub, dest, pad_ends, n_tiles, rows_total, tm):
    n_sub = h_sub.shape[0]
    tb = TM_COMB
    d = SUBROWS * LANES
    n_tiles_max = rows_total // tm
    return pl.pallas_call(
        functools.partial(_moe_scatter_kernel, tb=tb, tm=tm, n_tiles_max=n_tiles_max),
        grid_spec=pltpu.PrefetchScalarGridSpec(
            num_scalar_prefetch=3,
            grid=(n_sub // (tb * SUBROWS),),
            in_specs=[pl.BlockSpec(_to_subrows((tb, d)), lambda i, de, pe, nt: (i, 0))],
            out_specs=pl.BlockSpec(memory_space=pl.ANY),
            scratch_shapes=[pltpu.VMEM(_to_subrows((tm, d)), F32), pltpu.SemaphoreType.DMA((2,))]),
        out_shape=jax.ShapeDtypeStruct(_to_subrows((rows_total, d)), F32),
        compiler_params=_params(("arbitrary",)),
        name="moe_scatter",
    )(dest, pad_ends, n_tiles, h_sub)


def _moe_ffn_kernel(te_ref, nt_ref, xs_ref, wg_ref, wu_ref, wd_ref, o_ref, xb16, acc_ref, *, nf):
    i = pl.program_id(0)
    j = pl.program_id(1)
    tm = xb16.shape[0]

    @pl.when(i < nt_ref[0])
    def _():
        @pl.when(j == 0)
        def _():
            for c in range(SUBROWS):
                xb16[:, c * LANES:(c + 1) * LANES] = xs_ref[pl.ds(c, tm, stride=SUBROWS), :].astype(BF16)
            acc_ref[...] = jnp.zeros(acc_ref.shape, F32)

        acc_ref[...] += _swiglu_step(xb16[...], wg_ref.at[0], wu_ref.at[0], wd_ref.at[0])

        @pl.when(j == nf - 1)
        def _():
            for c in range(SUBROWS):
                o_ref[pl.ds(c, tm, stride=SUBROWS), :] = acc_ref[:, c * LANES:(c + 1) * LANES]

    @pl.when(jnp.logical_and(i >= nt_ref[0], j == nf - 1))
    def _():
        o_ref[...] = jnp.zeros(o_ref.shape, o_ref.dtype)


def _moe_ffn(xs, wg, wu, wd, li, tile_e, n_tiles, tm):
    d = wg.shape[2]
    f = wg.shape[3]
    rows_total = xs.shape[0] // SUBROWS
    tf = TF_FFN
    nf = f // tf

    def row_map(i, j, te, nt):
        return (jnp.minimum(i, nt[0] - 1), 0)

    def _fj(i, j, nt):
        return jnp.where(i < nt[0], j, nf - 1)

    def up_map(i, j, te, nt):
        return (li, te[jnp.minimum(i, nt[0] - 1)], 0, _fj(i, j, nt))

    def down_map(i, j, te, nt):
        return (li, te[jnp.minimum(i, nt[0] - 1)], _fj(i, j, nt), 0)

    return pl.pallas_call(
        functools.partial(_moe_ffn_kernel, nf=nf),
        grid_spec=pltpu.PrefetchScalarGridSpec(
            num_scalar_prefetch=2,
            grid=(rows_total // tm, nf),
            in_specs=[pl.BlockSpec(_to_subrows((tm, d)), row_map),
                      pl.BlockSpec((None, 1, d, tf), up_map),
                      pl.BlockSpec((None, 1, d, tf), up_map),
                      pl.BlockSpec((None, 1, tf, d), down_map)],
            out_specs=pl.BlockSpec(_to_subrows((tm, d)), lambda i, j, te, nt: (i, 0)),
            scratch_shapes=[pltpu.VMEM((tm, d), BF16), pltpu.VMEM((tm, d), F32)]),
        out_shape=jax.ShapeDtypeStruct(_to_subrows((rows_total, d)), F32),
        compiler_params=_params(("arbitrary", "arbitrary")),
        name="moe_ffn",
    )(tile_e, n_tiles, xs, wg, wu, wd)


def _combine_kernel(pos_ref, ys_hbm, wgt_ref, x_ref, mod_ref, fg_ref, o_ref, buf, sem, *, tm, final):
    i = pl.program_id(0)
    slot = i % 2

    def gather(step, to_slot):
        for k in range(2):
            _row_gather(lambda r: pos_ref[2 * (step * tm + r) + k], ys_hbm, buf.at[to_slot, k],
                        sem.at[to_slot], tm)

    @pl.when(i == 0)
    def _():
        gather(0, 0)

    @pl.when(i + 1 < pl.num_programs(0))
    def _():
        gather(i + 1, 1 - slot)

    for k in range(2):
        _row_gather_wait(ys_hbm, buf.at[slot, k], sem.at[slot], tm)

    wgt = wgt_ref[...]
    w0, w1 = wgt[:, 0:1], wgt[:, 1:2]
    y = jnp.concatenate([w0 * buf[slot, 0, pl.ds(c, tm, stride=SUBROWS), :]
                         + w1 * buf[slot, 1, pl.ds(c, tm, stride=SUBROWS), :] for c in range(SUBROWS)], axis=1)
    out = x_ref[...] + mod_ref[0, 5:6, :] * y
    if final:
        out = out * lax.rsqrt(jnp.mean(out * out, axis=-1, keepdims=True) + EPS) * fg_ref[...]
    o_ref[...] = out


def _moe_combine(ys, pos, wgt, x1, mod, final_g, seq, final):
    n, d = x1.shape
    tm = TM_COMB
    per_b = seq // tm
    return pl.pallas_call(
        functools.partial(_combine_kernel, tm=tm, final=final),
        grid_spec=pltpu.PrefetchScalarGridSpec(
            num_scalar_prefetch=1,
            grid=(n // tm,),
            in_specs=[pl.BlockSpec(memory_space=pl.ANY),
                      pl.BlockSpec((tm, LANES), lambda i, p: (i, 0)),
                      pl.BlockSpec((tm, d), lambda i, p: (i, 0)),
                      pl.BlockSpec((1, N_MOD, d), lambda i, p: (i // per_b, 0, 0)),
                      pl.BlockSpec((1, d), lambda i, p: (0, 0))],
            out_specs=pl.BlockSpec((tm, d), lambda i, p: (i, 0)),
            scratch_shapes=[pltpu.VMEM((2, 2) + _to_subrows((tm, d)), F32),
                            pltpu.SemaphoreType.DMA((2,))]),
        out_shape=jax.ShapeDtypeStruct((n, d), F32),
        compiler_params=_params(("arbitrary",)),
        name="moe_combine_final" if final else "moe_combine",
    )(pos.reshape(-1), ys, wgt, x1, mod, final_g)


def _final_norm_kernel(x_ref, g_ref, o_ref):
    x = x_ref[...]
    o_ref[...] = x * lax.rsqrt(jnp.mean(x * x, axis=-1, keepdims=True) + EPS) * g_ref[...]


def _final_norm(x2, g):
    n, d = x2.shape
    tm = TM_PROJ
    return pl.pallas_call(
        _final_norm_kernel,
        grid=(n // tm,),
        in_specs=[pl.BlockSpec((tm, d), lambda i: (i, 0)), pl.BlockSpec((1, d), lambda i: (0, 0))],
        out_specs=pl.BlockSpec((tm, d), lambda i: (i, 0)),
        out_shape=jax.ShapeDtypeStruct((n, d), F32),
        compiler_params=_params(("arbitrary",)),
        name="final_norm",
    )(x2, g)


def kernel(x, c, positions, ada_w, ada_b, norm_mix_g, norm_ffn_g, w_in, w_out, sink, lam_q1, lam_k1, lam_q2, lam_k2, subln_g, ffn_w_gate, ffn_w_up, ffn_w_down, router_w, moe_w_gate, moe_w_up, moe_w_down, final_g):
    batch, seq, d = x.shape
    depth = w_in.shape[0]
    n = batch * seq

    c_pad = jnp.zeros((8, d), F32).at[:batch].set(c)
    mod_all = _adaln(c_pad, ada_w, ada_b)
    cos_t, sin_t = _rope_tables(positions)
    w_in_p = _permute_in_proj(w_in).astype(BF16)
    w_vdt = _value_rows(w_in).astype(BF16)
    w_out_b = w_out.astype(BF16)
    ffn_w = [w.astype(BF16) for w in (ffn_w_gate, ffn_w_up, ffn_w_down)]
    moe_w = [moe_w_gate, moe_w_up, moe_w_down]
    final_g2 = final_g.reshape(1, d)

    x2 = x.reshape(n, d)
    for layer in range(depth):
        lambda_init = 0.8 - 0.6 * math.exp(-0.3 * layer)
        mod = mod_all[layer, :batch].reshape(batch, N_MOD, d)
        proj, vt = _inproj(x2, mod, norm_mix_g[layer].reshape(1, d), w_in_p, w_vdt, layer, cos_t, sin_t, seq)
        ya = _window_attention(proj, vt, sink[layer], batch, seq)
        lam_vecs = jnp.stack([lam_q1[layer], lam_k1[layer], lam_q2[layer], lam_k2[layer]])
        yd = _diff_attention(proj, vt, lam_vecs, subln_g[layer].reshape(1, LANES), batch, seq, lambda_init)
        i = layer // 2
        g_ffn = norm_ffn_g[layer].reshape(1, d)
        if layer % 2 == 0:
            x1, h = _outproj(ya, yd, w_out_b, layer, x2, mod, g_ffn, seq)
            x2 = _dense_ffn(h, *ffn_w, i, x1, mod, seq)
        else:
            rw = jnp.zeros((d, LANES), F32).at[:, :N_EXPERTS].set(router_w[i])
            rw_hi = rw.astype(BF16)
            rw_lo = (rw - rw_hi.astype(F32)).astype(BF16)
            x1, h, idx, wgt = _outproj(ya, yd, w_out_b, layer, x2, mod, g_ffn, seq, router_w=(rw_hi, rw_lo))
            dest, pad_ends, tile_e, n_tiles, rows_total = _moe_plan(idx[:, :2], TM_MOE)
            pos = dest.reshape(n, 2)
            xs = _moe_scatter(h, dest, pad_ends, n_tiles, rows_total, TM_MOE)
            ys = _moe_ffn(xs, *moe_w, i, tile_e, n_tiles, TM_MOE)
            final = layer == depth - 1
            x2 = _moe_combine(ys, pos, wgt, x1, mod, final_g2, seq, final)
    if depth % 2 == 1:
        x2 = _final_norm(x2, final_g2)
    return x2.reshape(batch, seq, d)
```

```python
import functools
import math

import jax
import jax.numpy as jnp
from jax import lax
from jax.experimental import pallas as pl
from jax.experimental.pallas import tpu as pltpu

F32 = jnp.float32
BF16 = jnp.bfloat16

HEAD_DIM = 64
HALF_DIM = HEAD_DIM // 2
ROPE_THETA = 10000.0
EPS = 1e-6
SWA_Q_HEADS = 8
SWA_KV_HEADS = 2
WINDOW = 128
DIFF_HEADS = 4
N_EXPERTS = 8
N_MOD = 6
LANES = 128
SUBROWS = 8
VMEM_LIMIT = 56 * 1024 * 1024

QA_BLK, KA_BLK, QD_BLK, KD_BLK = 0, 4, 6, 10
ROPE_BLKS = 14
PROJ_COLS = ROPE_BLKS * LANES
VA_COL0 = 640
VD_COL0 = 1792
VT_WIN_BLK = 4
Q_SCALE = HEAD_DIM ** -0.5
LOG2E = math.log2(math.e)

TM_PROJ = 512
TQ_WIN = 256
TQ_DIFF = 512
TK_DIFF = 1024
TM_FFN = 1024
TF_FFN = 512
TM_MOE = 1024
TM_COMB = 256


def _permute_in_proj(w_in):
    depth, d, _ = w_in.shape
    qa, ka, qd, kd = 0, 512, 768, 1280

    def pairs(lo, n_pairs):
        t = w_in[:, :, lo:lo + n_pairs * LANES].reshape(depth, d, n_pairs, 2, 2, HALF_DIM)
        return t.transpose(0, 1, 2, 4, 3, 5).reshape(depth, d, n_pairs * LANES)

    def dup_halves(lo):
        t = w_in[:, :, lo:lo + SWA_KV_HEADS * HEAD_DIM].reshape(depth, d, SWA_KV_HEADS, 2, 1, HALF_DIM)
        return jnp.broadcast_to(t, (depth, d, SWA_KV_HEADS, 2, 2, HALF_DIM)).reshape(depth, d, SWA_KV_HEADS * LANES)

    return jnp.concatenate([pairs(qa, 4), dup_halves(ka), pairs(qd, DIFF_HEADS), pairs(kd, DIFF_HEADS)], axis=-1)


def _value_rows(w_in):
    depth, d, _ = w_in.shape
    vd = jnp.swapaxes(w_in[:, :, VD_COL0:], 1, 2)
    va = jnp.swapaxes(w_in[:, :, VA_COL0:VA_COL0 + SWA_KV_HEADS * HEAD_DIM], 1, 2)
    va = jnp.broadcast_to(va.reshape(depth, SWA_KV_HEADS, 1, HEAD_DIM, d), (depth, SWA_KV_HEADS, 2, HEAD_DIM, d))
    return jnp.concatenate([vd, va.reshape(depth, SWA_KV_HEADS * LANES, d)], axis=1)


def _first_of_pair_mask():
    lane = lax.broadcasted_iota(jnp.int32, (1, LANES), 1)
    return (lane // HALF_DIM) % 2 == 0


def _silu(g):
    return g / (1.0 + jnp.exp(-g))


def _params(sem):
    return pltpu.CompilerParams(dimension_semantics=sem, vmem_limit_bytes=VMEM_LIMIT)


def _adaln_kernel(c_ref, w_ref, b_ref, o_ref):
    a = _silu(c_ref[...])
    o_ref[0] = jnp.dot(a, w_ref[0], preferred_element_type=F32,
                       precision=lax.Precision.HIGHEST) + b_ref[0]


def _adaln(c_pad, ada_w, ada_b):
    depth, d, cols = ada_w.shape
    tn = 1536
    return pl.pallas_call(
        _adaln_kernel,
        grid=(depth, cols // tn),
        in_specs=[pl.BlockSpec((8, d), lambda l, j: (0, 0)),
                  pl.BlockSpec((1, d, tn), lambda l, j: (l, 0, j)),
                  pl.BlockSpec((1, 1, tn), lambda l, j: (l, 0, j))],
        out_specs=pl.BlockSpec((1, 8, tn), lambda l, j: (l, 0, j)),
        out_shape=jax.ShapeDtypeStruct((depth, 8, cols), F32),
        compiler_params=_params(("arbitrary", "arbitrary")),
        name="adaln",
    )(c_pad, ada_w, ada_b.reshape(depth, 1, cols))


def _rope_table_kernel(pos_ref, cos_ref, sin_ref):
    lane = lax.broadcasted_iota(jnp.int32, (1, LANES), 1)
    freq = (lane % HALF_DIM).astype(F32)
    inv = 1.0 / (ROPE_THETA ** (2.0 * freq / HEAD_DIM))
    ang = pos_ref[...].astype(F32) * inv
    sign = jnp.where(lane < HEAD_DIM, -1.0, 1.0)
    cos_ref[...] = jnp.cos(ang)
    sin_ref[...] = jnp.sin(ang) * sign


def _rope_tables(positions):
    n = positions.size
    tm = 1024
    pos = positions.reshape(n, 1)
    return pl.pallas_call(
        _rope_table_kernel,
        grid=(n // tm,),
        in_specs=[pl.BlockSpec((tm, 1), lambda i: (i, 0))],
        out_specs=[pl.BlockSpec((tm, LANES), lambda i: (i, 0))] * 2,
        out_shape=[jax.ShapeDtypeStruct((n, LANES), F32)] * 2,
        compiler_params=_params(("arbitrary",)),
        name="rope_tables",
    )(pos)


def _rms_mod(x, g, scale, shift):
    y = x * lax.rsqrt(jnp.mean(x * x, axis=-1, keepdims=True) + EPS) * g
    return y * (1.0 + scale) + shift


def _inproj_kernel(x_ref, mod_ref, g_ref, w_ref, wvt_ref, cos_ref, sin_ref, o_ref, vt_ref):
    h = _rms_mod(x_ref[...], g_ref[...], mod_ref[0, 1:2, :], mod_ref[0, 0:1, :]).astype(BF16)
    proj = jnp.dot(h, w_ref[...], preferred_element_type=F32)
    vt_ref[...] = lax.dot_general(wvt_ref[...], h, (((1,), (1,)), ((), ())),
                                  preferred_element_type=F32).astype(BF16)
    cos = cos_ref[...]
    sin = sin_ref[...]
    for blk in range(ROPE_BLKS):
        t = proj[:, blk * LANES:(blk + 1) * LANES]
        r = t * cos + pltpu.roll(t, HEAD_DIM, axis=1) * sin
        if blk < KA_BLK or QD_BLK <= blk < KD_BLK:
            r = r * (Q_SCALE * LOG2E)
        o_ref[:, blk * LANES:(blk + 1) * LANES] = r.astype(BF16)


def _inproj(x2, mod, g, w, wvt, layer, cos_t, sin_t, seq):
    n, d = x2.shape
    tm = TM_PROJ
    per_b = seq // tm
    vt_rows = wvt.shape[1]
    return pl.pallas_call(
        _inproj_kernel,
        grid=(n // tm,),
        in_specs=[pl.BlockSpec((tm, d), lambda i: (i, 0)),
                  pl.BlockSpec((1, N_MOD, d), lambda i: (i // per_b, 0, 0)),
                  pl.BlockSpec((1, d), lambda i: (0, 0)),
                  pl.BlockSpec((None, d, PROJ_COLS), lambda i: (layer, 0, 0)),
                  pl.BlockSpec((None, vt_rows, d), lambda i: (layer, 0, 0)),
                  pl.BlockSpec((tm, LANES), lambda i: (i, 0)),
                  pl.BlockSpec((tm, LANES), lambda i: (i, 0))],
        out_specs=[pl.BlockSpec((tm, PROJ_COLS), lambda i: (i, 0)),
                   pl.BlockSpec((vt_rows, tm), lambda i: (0, i))],
        out_shape=[jax.ShapeDtypeStruct((n, PROJ_COLS), BF16),
                   jax.ShapeDtypeStruct((vt_rows, n), BF16)],
        compiler_params=_params(("arbitrary",)),
        name="inproj",
    )(x2, mod, g, w, wvt, cos_t, sin_t)


def _win_kernel(sink_ref, q_ref, k_ref, vt_ref, o_ref, qt_sc, *, tq, seq):
    i = pl.program_id(1)
    heads = SWA_Q_HEADS // SWA_KV_HEADS
    width = tq + 2 * WINDOW
    start = pl.multiple_of(jnp.clip(i * tq - WINDOW, 0, seq - width), LANES)
    first = _first_of_pair_mask()
    kpos = start + lax.broadcasted_iota(jnp.int32, (width, tq), 0)
    qpos = i * tq + lax.broadcasted_iota(jnp.int32, (width, tq), 1)
    valid = jnp.abs(qpos - kpos) <= WINDOW
    col = lax.broadcasted_iota(jnp.int32, (1, heads * tq), 1)
    lane = lax.broadcasted_iota(jnp.int32, (1, LANES), 1)
    for g in range(SWA_KV_HEADS):
        for blk in range(heads // 2):
            qb = (heads // 2) * g + blk
            q = q_ref[:, qb * LANES:(qb + 1) * LANES].astype(F32)
            zero = jnp.zeros_like(q)
            qt_sc[g, :, (2 * blk) * tq:(2 * blk + 1) * tq] = jnp.where(first, q, zero).T.astype(BF16)
            qt_sc[g, :, (2 * blk + 1) * tq:(2 * blk + 2) * tq] = jnp.where(first, zero, q).T.astype(BF16)

        s = jnp.dot(k_ref[pl.ds(start, width), g * LANES:(g + 1) * LANES], qt_sc[g],
                    preferred_element_type=F32)
        s = jnp.concatenate([jnp.where(valid, s[:, c * tq:(c + 1) * tq], -1e30) for c in range(heads)], axis=1)
        sk = jnp.zeros((1, heads * tq), F32)
        for c in range(heads):
            sk = jnp.where(col // tq == c, sink_ref[heads * g + c] * LOG2E, sk)
        m = jnp.maximum(jnp.max(s, axis=0, keepdims=True), sk)
        p = jnp.exp2(s - m)
        den = jnp.sum(p, axis=0, keepdims=True) + jnp.exp2(sk - m)
        o = jnp.dot(vt_ref[g * LANES:(g + 1) * LANES, pl.ds(start, width)], p.astype(BF16),
                    preferred_element_type=F32) / den
        for blk in range(heads // 2):
            qb = (heads // 2) * g + blk
            oa = o[:, (2 * blk) * tq:(2 * blk + 1) * tq].T
            ob = o[:, (2 * blk + 1) * tq:(2 * blk + 2) * tq].T
            o_ref[:, qb * LANES:(qb + 1) * LANES] = jnp.where(lane < HEAD_DIM, oa, ob).astype(o_ref.dtype)


def _window_attention(proj, vt, sink, batch, seq):
    n = proj.shape[0]
    tq = TQ_WIN
    nq = seq // tq
    heads = SWA_Q_HEADS // SWA_KV_HEADS
    q_cols = SWA_Q_HEADS // 2 * LANES
    kv_cols = SWA_KV_HEADS * LANES
    assert QA_BLK == 0 and KA_BLK * LANES % kv_cols == 0 and VT_WIN_BLK * LANES % kv_cols == 0
    kern = functools.partial(_win_kernel, tq=tq, seq=seq)
    return pl.pallas_call(
        kern,
        grid=(batch, nq),
        in_specs=[pl.BlockSpec(memory_space=pltpu.SMEM),
                  pl.BlockSpec((tq, q_cols), lambda b, i: (b * nq + i, 0)),
                  pl.BlockSpec((seq, kv_cols), lambda b, i: (b, KA_BLK * LANES // kv_cols)),
                  pl.BlockSpec((kv_cols, seq), lambda b, i: (VT_WIN_BLK * LANES // kv_cols, b))],
        out_specs=pl.BlockSpec((tq, q_cols), lambda b, i: (b * nq + i, 0)),
        out_shape=jax.ShapeDtypeStruct((n, q_cols), BF16),
        scratch_shapes=[pltpu.VMEM((SWA_KV_HEADS, LANES, heads * tq), BF16)],
        compiler_params=_params(("arbitrary", "arbitrary")),
        name="window_attn",
    )(sink, proj, proj, vt)


def _diff_kernel(lam_ref, g_ref, q_ref, k_ref, vt_ref, o_ref, qt_sc, s_a, s_b, cm_a, cm_b, m_sc, l_sc, acc_sc,
                 *, tk, seq, lambda_init):
    tq = q_ref.shape[0]
    nk = seq // tk
    q = q_ref[...].astype(F32)
    first = _first_of_pair_mask()
    zero = jnp.zeros_like(q)
    qt_sc[:, :tq] = jnp.where(first, q, zero).T.astype(BF16)
    qt_sc[:, tq:] = jnp.where(first, zero, q).T.astype(BF16)

    m_sc[...] = jnp.full(m_sc.shape, -1e30, F32)
    l_sc[...] = jnp.zeros(l_sc.shape, F32)
    acc_sc[...] = jnp.zeros(acc_sc.shape, F32)

    def scores(t, s_ref, cm_ref):
        off = pl.multiple_of(t * tk, tk)
        s = jnp.dot(k_ref[pl.ds(off, tk), :], qt_sc[...], preferred_element_type=F32)
        s_ref[...] = s
        cm_ref[...] = jnp.max(s, axis=0, keepdims=True)

    def consume(t, s_ref, cm_ref):
        off = pl.multiple_of(t * tk, tk)
        m_prev = m_sc[...]
        m_new = jnp.maximum(m_prev, cm_ref[...])
        alpha = jnp.exp2(m_prev - m_new)
        p = jnp.exp2(s_ref[...] - m_new)
        l_sc[...] = alpha * l_sc[...] + jnp.sum(p, axis=0, keepdims=True)
        vb = vt_ref[:, pl.ds(off, tk)]
        acc_sc[...] = alpha * acc_sc[...] + jnp.dot(vb, p.astype(BF16), preferred_element_type=F32)
        m_sc[...] = m_new

    scores(0, s_a, cm_a)

    def body(u, carry):
        t = 2 * u
        scores(t + 1, s_b, cm_b)
        consume(t, s_a, cm_a)
        scores(t + 2, s_a, cm_a)
        consume(t + 1, s_b, cm_b)
        return carry

    lax.fori_loop(0, nk // 2 - 1, body, 0)
    scores(nk - 1, s_b, cm_b)
    consume(nk - 2, s_a, cm_a)
    consume(nk - 1, s_b, cm_b)

    lam1 = jnp.sum(lam_ref[0:1, :] * lam_ref[1:2, :], axis=-1, keepdims=True)
    lam2 = jnp.sum(lam_ref[2:3, :] * lam_ref[3:4, :], axis=-1, keepdims=True)
    lam = jnp.exp(lam1) - jnp.exp(lam2) + lambda_init
    o = acc_sc[...] / l_sc[...]
    o = o[:, :tq] - lam * o[:, tq:]
    y = o * lax.rsqrt(jnp.mean(o * o, axis=0, keepdims=True) + EPS)
    y = y.T * (g_ref[...] * (1.0 - lambda_init))
    o_ref[...] = y.astype(o_ref.dtype)


def _diff_attention(proj, vt, lam_vecs, subln_g, batch, seq, lambda_init):
    n = proj.shape[0]
    tq, tk = TQ_DIFF, TK_DIFF
    nq = seq // tq
    assert seq % (2 * tk) == 0 and seq % tq == 0, "the pipelined key loop consumes key blocks in pairs"
    kern = functools.partial(_diff_kernel, tk=tk, seq=seq, lambda_init=lambda_init)
    return pl.pallas_call(
        kern,
        grid=(batch, DIFF_HEADS, nq),
        in_specs=[pl.BlockSpec((4, HEAD_DIM), lambda b, h, i: (0, 0)),
                  pl.BlockSpec((1, LANES), lambda b, h, i: (0, 0)),
                  pl.BlockSpec((tq, LANES), lambda b, h, i: (b * nq + i, QD_BLK + h)),
                  pl.BlockSpec((seq, LANES), lambda b, h, i: (b, KD_BLK + h)),
                  pl.BlockSpec((LANES, seq), lambda b, h, i: (h, b))],
        out_specs=pl.BlockSpec((tq, LANES), lambda b, h, i: (b * nq + i, h)),
        out_shape=jax.ShapeDtypeStruct((n, DIFF_HEADS * LANES), BF16),
        scratch_shapes=[pltpu.VMEM((LANES, 2 * tq), BF16),
                        pltpu.VMEM((tk, 2 * tq), F32), pltpu.VMEM((tk, 2 * tq), F32),
                        pltpu.VMEM((1, 2 * tq), F32), pltpu.VMEM((1, 2 * tq), F32),
                        pltpu.VMEM((1, 2 * tq), F32), pltpu.VMEM((1, 2 * tq), F32),
                        pltpu.VMEM((LANES, 2 * tq), F32)],
        compiler_params=_params(("arbitrary", "arbitrary", "arbitrary")),
        name="diff_attn",
    )(lam_vecs, subln_g, proj, proj, vt)


def _route_top2(logits):
    lane = lax.broadcasted_iota(jnp.int32, logits.shape, 1)
    lg = jnp.where(lane < N_EXPERTS, logits, -jnp.inf)
    v1 = jnp.max(lg, axis=-1, keepdims=True)
    i1 = jnp.min(jnp.where(lg == v1, lane, LANES), axis=-1, keepdims=True)
    lg2 = jnp.where(lane == i1, -jnp.inf, lg)
    v2 = jnp.max(lg2, axis=-1, keepdims=True)
    i2 = jnp.min(jnp.where(lg2 == v2, lane, LANES), axis=-1, keepdims=True)
    e = jnp.exp(v2 - v1)
    w1 = 1.0 / (1.0 + e)
    w2 = e / (1.0 + e)
    idx = jnp.where(lane == 0, i1, jnp.where(lane == 1, i2, 0))
    wgt = jnp.where(lane == 0, w1, jnp.where(lane == 1, w2, 0.0))
    return idx, wgt


def _outproj_kernel(ya_ref, yd_ref, w_ref, x_ref, mod_ref, g_ref, *rest, moe):
    if moe:
        rwh_ref, rwl_ref, x1_ref, h_ref, idx_ref, wgt_ref = rest
    else:
        x1_ref, h_ref = rest
    half = ya_ref.shape[1]
    y = (jnp.dot(ya_ref[...], w_ref[:half, :], preferred_element_type=F32)
         + jnp.dot(yd_ref[...], w_ref[half:, :], preferred_element_type=F32))
    x1 = x_ref[...] + mod_ref[0, 2:3, :] * y
    x1_ref[...] = x1
    h = _rms_mod(x1, g_ref[...], mod_ref[0, 4:5, :], mod_ref[0, 3:4, :])
    if moe:
        for c in range(SUBROWS):
            h_ref[pl.ds(c, h.shape[0], stride=SUBROWS), :] = h[:, c * LANES:(c + 1) * LANES]
    else:
        h_ref[...] = h.astype(h_ref.dtype)
    if moe:
        h_hi = h.astype(BF16)
        h_lo = (h - h_hi.astype(F32)).astype(BF16)
        logits = (jnp.dot(h_hi, rwh_ref[...], preferred_element_type=F32)
                  + jnp.dot(h_hi, rwl_ref[...], preferred_element_type=F32)
                  + jnp.dot(h_lo, rwh_ref[...], preferred_element_type=F32))
        idx, wgt = _route_top2(logits)
        idx_ref[...] = idx
        wgt_ref[...] = wgt


def _outproj(ya, yd, w, layer, x2, mod, g, seq, router_w=None):
    n, d = x2.shape
    tm = TM_PROJ
    per_b = seq // tm
    moe = router_w is not None
    row = lambda i: (i, 0)
    in_specs = [pl.BlockSpec((tm, ya.shape[1]), row),
                pl.BlockSpec((tm, yd.shape[1]), row),
                pl.BlockSpec((None,) + w.shape[1:], lambda i: (layer, 0, 0)),
                pl.BlockSpec((tm, d), row),
                pl.BlockSpec((1, N_MOD, d), lambda i: (i // per_b, 0, 0)),
                pl.BlockSpec((1, d), lambda i: (0, 0))]
    out_specs = [pl.BlockSpec((tm, d), row), pl.BlockSpec(_to_subrows((tm, d)) if moe else (tm, d), row)]
    out_shape = [jax.ShapeDtypeStruct((n, d), F32),
                 jax.ShapeDtypeStruct(_to_subrows((n, d)), F32) if moe else jax.ShapeDtypeStruct((n, d), BF16)]
    args = [ya, yd, w, x2, mod, g]
    if moe:
        in_specs += [pl.BlockSpec((d, LANES), lambda i: (0, 0))] * 2
        out_specs += [pl.BlockSpec((tm, LANES), row)] * 2
        out_shape += [jax.ShapeDtypeStruct((n, LANES), jnp.int32), jax.ShapeDtypeStruct((n, LANES), F32)]
        args += list(router_w)
    return pl.pallas_call(
        functools.partial(_outproj_kernel, moe=moe),
        grid=(n // tm,),
        in_specs=in_specs, out_specs=out_specs, out_shape=out_shape,
        compiler_params=_params(("arbitrary",)),
        name="outproj_moe" if moe else "outproj",
    )(*args)


def _swiglu_step(h, wg_ref, wu_ref, wd_ref):
    g = jnp.dot(h, wg_ref[...].astype(BF16), preferred_element_type=F32)
    u = jnp.dot(h, wu_ref[...].astype(BF16), preferred_element_type=F32)
    return jnp.dot((_silu(g) * u).astype(BF16), wd_ref[...].astype(BF16), preferred_element_type=F32)


def _ffn_kernel(h_ref, wg_ref, wu_ref, wd_ref, x_ref, mod_ref, o_ref, acc_ref):
    j = pl.program_id(1)

    @pl.when(j == 0)
    def _():
        acc_ref[...] = jnp.zeros(acc_ref.shape, F32)

    acc_ref[...] += _swiglu_step(h_ref[...], wg_ref, wu_ref, wd_ref)

    @pl.when(j == pl.num_programs(1) - 1)
    def _():
        o_ref[...] = x_ref[...] + mod_ref[0, 5:6, :] * acc_ref[...]


def _dense_ffn(h, wg, wu, wd, li, x1, mod, seq):
    n, d = x1.shape
    f = wg.shape[2]
    tm, tf = TM_FFN, TF_FFN
    per_b = seq // tm
    return pl.pallas_call(
        _ffn_kernel,
        grid=(n // tm, f // tf),
        in_specs=[pl.BlockSpec((tm, d), lambda i, j: (i, 0)),
                  pl.BlockSpec((None, d, tf), lambda i, j: (li, 0, j)),
                  pl.BlockSpec((None, d, tf), lambda i, j: (li, 0, j)),
                  pl.BlockSpec((None, tf, d), lambda i, j: (li, j, 0)),
                  pl.BlockSpec((tm, d), lambda i, j: (i, 0)),
                  pl.BlockSpec((1, N_MOD, d), lambda i, j: (i // per_b, 0, 0))],
        out_specs=pl.BlockSpec((tm, d), lambda i, j: (i, 0)),
        out_shape=jax.ShapeDtypeStruct((n, d), F32),
        scratch_shapes=[pltpu.VMEM((tm, d), F32)],
        compiler_params=_params(("arbitrary", "arbitrary")),
        name="dense_ffn",
    )(h, wg, wu, wd, x1, mod)


def _moe_plan(idx, tm):
    n = idx.shape[0]
    flat_e = idx.reshape(-1)
    onehot = (flat_e[:, None] == jnp.arange(N_EXPERTS, dtype=jnp.int32)[None, :]).astype(jnp.int32)
    csum = jnp.cumsum(onehot, axis=0)
    rank = jnp.sum((csum - onehot) * onehot, axis=1)
    counts = csum[-1]
    padded = (counts + tm - 1) // tm * tm
    pad_ends = jnp.cumsum(padded)
    pad_starts = pad_ends - padded
    dest = (pad_starts[flat_e] + rank).astype(jnp.int32)
    rows = 2 * n + N_EXPERTS * tm
    n_tiles = (pad_ends[-1] // tm).astype(jnp.int32).reshape(1)
    tile_start = jnp.arange(rows // tm, dtype=jnp.int32) * tm
    tile_e = jnp.minimum(jnp.sum((pad_ends[None, :] <= tile_start[:, None]).astype(jnp.int32), axis=1),
                         N_EXPERTS - 1).astype(jnp.int32)
    return dest, pad_ends.astype(jnp.int32), tile_e, n_tiles, rows


def _to_subrows(ref_or_shape):
    rows, d = ref_or_shape
    assert d == SUBROWS * LANES
    return (rows * SUBROWS, LANES)


def _row_copy(src_hbm, src_row, dst_ref, dst_row, sem):
    s = pl.multiple_of(src_row * SUBROWS, SUBROWS)
    t = pl.multiple_of(dst_row * SUBROWS, SUBROWS)
    return pltpu.make_async_copy(src_hbm.at[pl.ds(s, SUBROWS)], dst_ref.at[pl.ds(t, SUBROWS)], sem)


def _row_gather(idx_fns, src_hbm, dst_refs, sem, rows):
    def issue(r, carry):
        for k, (idx_of, dst_ref) in enumerate(zip(idx_fns, dst_refs)):
            _row_copy(src_hbm, idx_of(r), dst_ref, r, sem).start(priority=k % 2)
        return carry

    lax.fori_loop(0, rows, issue, 0, unroll=4)


def _row_gather_wait(src_hbm, dst_ref, sem, rows):
    pltpu.make_async_copy(src_hbm.at[pl.ds(0, rows * SUBROWS)], dst_ref, sem).wait()


def _moe_scatter_kernel(dest_ref, pe_ref, nt_ref, h_ref, xs_hbm, zeros, sem, *, tb, tm, n_tiles_max):
    i = pl.program_id(0)
    zero_sem, row_sem = sem.at[0], sem.at[1]
    tile_rows = tm * SUBROWS

    def zero_tile(t):
        start = pl.multiple_of(t * tile_rows, tile_rows)
        return pltpu.make_async_copy(zeros, xs_hbm.at[pl.ds(start, tile_rows)], zero_sem)

    @pl.when(i == 0)
    def _():
        zeros[...] = jnp.zeros(zeros.shape, zeros.dtype)
        for e in range(N_EXPERTS):
            last_tile = jnp.maximum(pe_ref[e] // tm - 1, 0)
            zero_tile(last_tile).start()
            zero_tile(last_tile).wait()
        for t in range(n_tiles_max):
            @pl.when(t >= nt_ref[0])
            def _():
                zero_tile(t).start()
                zero_tile(t).wait()

    base = i * tb

    def issue(r, carry):
        src = h_ref.at[pl.ds(pl.multiple_of(r * SUBROWS, SUBROWS), SUBROWS)]
        for k in range(2):
            d = dest_ref[2 * (base + r) + k]
            pltpu.make_async_copy(src, xs_hbm.at[pl.ds(pl.multiple_of(d * SUBROWS, SUBROWS), SUBROWS)],
                                  row_sem).start(priority=k)
        return carry

    lax.fori_loop(0, tb, issue, 0, unroll=4)
    for k in range(2):
        pltpu.make_async_copy(h_ref, xs_hbm.at[pl.ds(0, tb * SUBROWS)], row_sem).wait()


def _moe_scatter(h_sub, dest, pad_ends, n_tiles, rows_total, tm):
    n_sub = h_sub.shape[0]
    tb = TM_COMB
    d = SUBROWS * LANES
    n_tiles_max = rows_total // tm
    return pl.pallas_call(
        functools.partial(_moe_scatter_kernel, tb=tb, tm=tm, n_tiles_max=n_tiles_max),
        grid_spec=pltpu.PrefetchScalarGridSpec(
            num_scalar_prefetch=3,
            grid=(n_sub // (tb * SUBROWS),),
            in_specs=[pl.BlockSpec(_to_subrows((tb, d)), lambda i, de, pe, nt: (i, 0))],
            out_specs=pl.BlockSpec(memory_space=pl.ANY),
            scratch_shapes=[pltpu.VMEM(_to_subrows((tm, d)), F32), pltpu.SemaphoreType.DMA((2,))]),
        out_shape=jax.ShapeDtypeStruct(_to_subrows((rows_total, d)), F32),
        compiler_params=_params(("arbitrary",)),
        name="moe_scatter",
    )(dest, pad_ends, n_tiles, h_sub)


def _moe_ffn_kernel(te_ref, nt_ref, xs_ref, wg_ref, wu_ref, wd_ref, o_ref, xb16, acc_ref, *, nf):
    i = pl.program_id(0)
    j = pl.program_id(1)
    tm = xb16.shape[0]

    @pl.when(i < nt_ref[0])
    def _():
        @pl.when(j == 0)
        def _():
            for c in range(SUBROWS):
                xb16[:, c * LANES:(c + 1) * LANES] = xs_ref[pl.ds(c, tm, stride=SUBROWS), :].astype(BF16)
            acc_ref[...] = jnp.zeros(acc_ref.shape, F32)

        acc_ref[...] += _swiglu_step(xb16[...], wg_ref.at[0], wu_ref.at[0], wd_ref.at[0])

        @pl.when(j == nf - 1)
        def _():
            for c in range(SUBROWS):
                o_ref[pl.ds(c, tm, stride=SUBROWS), :] = acc_ref[:, c * LANES:(c + 1) * LANES]

    @pl.when(jnp.logical_and(i >= nt_ref[0], j == nf - 1))
    def _():
        o_ref[...] = jnp.zeros(o_ref.shape, o_ref.dtype)


def _moe_ffn(xs, wg, wu, wd, li, tile_e, n_tiles, tm):
    d = wg.shape[2]
    f = wg.shape[3]
    rows_total = xs.shape[0] // SUBROWS
    tf = TF_FFN
    nf = f // tf

    def row_map(i, j, te, nt):
        return (jnp.minimum(i, nt[0] - 1), 0)

    def _fj(i, j, nt):
        return jnp.where(i < nt[0], j, nf - 1)

    def up_map(i, j, te, nt):
        return (li, te[jnp.minimum(i, nt[0] - 1)], 0, _fj(i, j, nt))

    def down_map(i, j, te, nt):
        return (li, te[jnp.minimum(i, nt[0] - 1)], _fj(i, j, nt), 0)

    return pl.pallas_call(
        functools.partial(_moe_ffn_kernel, nf=nf),
        grid_spec=pltpu.PrefetchScalarGridSpec(
            num_scalar_prefetch=2,
            grid=(rows_total // tm, nf),
            in_specs=[pl.BlockSpec(_to_subrows((tm, d)), row_map),
                      pl.BlockSpec((None, 1, d, tf), up_map),
                      pl.BlockSpec((None, 1, d, tf), up_map),
                      pl.BlockSpec((None, 1, tf, d), down_map)],
            out_specs=pl.BlockSpec(_to_subrows((tm, d)), lambda i, j, te, nt: (i, 0)),
            scratch_shapes=[pltpu.VMEM((tm, d), BF16), pltpu.VMEM((tm, d), F32)]),
        out_shape=jax.ShapeDtypeStruct(_to_subrows((rows_total, d)), F32),
        compiler_params=_params(("arbitrary", "arbitrary")),
        name="moe_ffn",
    )(tile_e, n_tiles, xs, wg, wu, wd)


def _combine_kernel(pos_ref, ys_hbm, wgt_ref, x_ref, mod_ref, fg_ref, o_ref, buf, sem, *, tm, final):
    i = pl.program_id(0)
    slot = i % 2

    def gather(step, to_slot):
        _row_gather([lambda r: pos_ref[2 * (step * tm + r)], lambda r: pos_ref[2 * (step * tm + r) + 1]],
                    ys_hbm, [buf.at[to_slot, 0], buf.at[to_slot, 1]], sem.at[to_slot], tm)

    @pl.when(i == 0)
    def _():
        gather(0, 0)

    @pl.when(i + 1 < pl.num_programs(0))
    def _():
        gather(i + 1, 1 - slot)

    for k in range(2):
        _row_gather_wait(ys_hbm, buf.at[slot, k], sem.at[slot], tm)

    wgt = wgt_ref[...]
    w0, w1 = wgt[:, 0:1], wgt[:, 1:2]
    y = jnp.concatenate([w0 * buf[slot, 0, pl.ds(c, tm, stride=SUBROWS), :]
                         + w1 * buf[slot, 1, pl.ds(c, tm, stride=SUBROWS), :] for c in range(SUBROWS)], axis=1)
    out = x_ref[...] + mod_ref[0, 5:6, :] * y
    if final:
        out = out * lax.rsqrt(jnp.mean(out * out, axis=-1, keepdims=True) + EPS) * fg_ref[...]
    o_ref[...] = out


def _moe_combine(ys, pos, wgt, x1, mod, final_g, seq, final):
    n, d = x1.shape
    tm = TM_COMB
    per_b = seq // tm
    return pl.pallas_call(
        functools.partial(_combine_kernel, tm=tm, final=final),
        grid_spec=pltpu.PrefetchScalarGridSpec(
            num_scalar_prefetch=1,
            grid=(n // tm,),
            in_specs=[pl.BlockSpec(memory_space=pl.ANY),
                      pl.BlockSpec((tm, LANES), lambda i, p: (i, 0)),
                      pl.BlockSpec((tm, d), lambda i, p: (i, 0)),
                      pl.BlockSpec((1, N_MOD, d), lambda i, p: (i // per_b, 0, 0)),
                      pl.BlockSpec((1, d), lambda i, p: (0, 0))],
            out_specs=pl.BlockSpec((tm, d), lambda i, p: (i, 0)),
            scratch_shapes=[pltpu.VMEM((2, 2) + _to_subrows((tm, d)), F32),
                            pltpu.SemaphoreType.DMA((2,))]),
        out_shape=jax.ShapeDtypeStruct((n, d), F32),
        compiler_params=_params(("arbitrary",)),
        name="moe_combine_final" if final else "moe_combine",
    )(pos.reshape(-1), ys, wgt, x1, mod, final_g)


def _final_norm_kernel(x_ref, g_ref, o_ref):
    x = x_ref[...]
    o_ref[...] = x * lax.rsqrt(jnp.mean(x * x, axis=-1, keepdims=True) + EPS) * g_ref[...]


def _final_norm(x2, g):
    n, d = x2.shape
    tm = TM_PROJ
    return pl.pallas_call(
        _final_norm_kernel,
        grid=(n // tm,),
        in_specs=[pl.BlockSpec((tm, d), lambda i: (i, 0)), pl.BlockSpec((1, d), lambda i: (0, 0))],
        out_specs=pl.BlockSpec((tm, d), lambda i: (i, 0)),
        out_shape=jax.ShapeDtypeStruct((n, d), F32),
        compiler_params=_params(("arbitrary",)),
        name="final_norm",
    )(x2, g)


def kernel(x, c, positions, ada_w, ada_b, norm_mix_g, norm_ffn_g, w_in, w_out, sink, lam_q1, lam_k1, lam_q2, lam_k2, subln_g, ffn_w_gate, ffn_w_up, ffn_w_down, router_w, moe_w_gate, moe_w_up, moe_w_down, final_g):
    batch, seq, d = x.shape
    depth = w_in.shape[0]
    n = batch * seq

    c_pad = jnp.zeros((8, d), F32).at[:batch].set(c)
    mod_all = _adaln(c_pad, ada_w, ada_b)
    cos_t, sin_t = _rope_tables(positions)
    w_in_p = _permute_in_proj(w_in).astype(BF16)
    w_vdt = _value_rows(w_in).astype(BF16)
    w_out_b = w_out.astype(BF16)
    ffn_w = [w.astype(BF16) for w in (ffn_w_gate, ffn_w_up, ffn_w_down)]
    moe_w = [moe_w_gate, moe_w_up, moe_w_down]
    final_g2 = final_g.reshape(1, d)

    x2 = x.reshape(n, d)
    for layer in range(depth):
        lambda_init = 0.8 - 0.6 * math.exp(-0.3 * layer)
        mod = mod_all[layer, :batch].reshape(batch, N_MOD, d)
        proj, vt = _inproj(x2, mod, norm_mix_g[layer].reshape(1, d), w_in_p, w_vdt, layer, cos_t, sin_t, seq)
        ya = _window_attention(proj, vt, sink[layer], batch, seq)
        lam_vecs = jnp.stack([lam_q1[layer], lam_k1[layer], lam_q2[layer], lam_k2[layer]])
        yd = _diff_attention(proj, vt, lam_vecs, subln_g[layer].reshape(1, LANES), batch, seq, lambda_init)
        i = layer // 2
        g_ffn = norm_ffn_g[layer].reshape(1, d)
        if layer % 2 == 0:
            x1, h = _outproj(ya, yd, w_out_b, layer, x2, mod, g_ffn, seq)
            x2 = _dense_ffn(h, *ffn_w, i, x1, mod, seq)
        else:
            rw = jnp.zeros((d, LANES), F32).at[:, :N_EXPERTS].set(router_w[i])
            rw_hi = rw.astype(BF16)
            rw_lo = (rw - rw_hi.astype(F32)).astype(BF16)
            x1, h, idx, wgt = _outproj(ya, yd, w_out_b, layer, x2, mod, g_ffn, seq, router_w=(rw_hi, rw_lo))
            dest, pad_ends, tile_e, n_tiles, rows_total = _moe_plan(idx[:, :2], TM_MOE)
            pos = dest.reshape(n, 2)
            xs = _moe_scatter(h, dest, pad_ends, n_tiles, rows_total, TM_MOE)
            ys = _moe_ffn(xs, *moe_w, i, tile_e, n_tiles, TM_MOE)
            final = layer == depth - 1
            x2 = _moe_combine(ys, pos, wgt, x1, mod, final_g2, seq, final)
    if depth % 2 == 1:
        x2 = _final_norm(x2, final_g2)
    return x2.reshape(batch, seq, d)
```

```python
import functools
import math

import jax
import jax.numpy as jnp
from jax import lax
from jax.experimental import pallas as pl
from jax.experimental.pallas import tpu as pltpu

F32 = jnp.float32
BF16 = jnp.bfloat16

HEAD_DIM = 64
HALF_DIM = HEAD_DIM // 2
ROPE_THETA = 10000.0
EPS = 1e-6
SWA_Q_HEADS = 8
SWA_KV_HEADS = 2
WINDOW = 128
DIFF_HEADS = 4
N_EXPERTS = 8
N_MOD = 6
LANES = 128
SUBROWS = 8
VMEM_LIMIT = 56 * 1024 * 1024

QA_BLK, KA_BLK, QD_BLK, KD_BLK = 0, 4, 6, 10
ROPE_BLKS = 14
PROJ_COLS = ROPE_BLKS * LANES
VA_COL0 = 640
VD_COL0 = 1792
VT_WIN_BLK = 4
Q_SCALE = HEAD_DIM ** -0.5
LOG2E = math.log2(math.e)

TM_PROJ = 512
TQ_WIN = 256
TQ_DIFF = 512
TK_DIFF = 1024
TM_FFN = 1024
TF_FFN = 512
TM_MOE = 1024
TM_COMB = 256


def _permute_in_proj(w_in):
    depth, d, _ = w_in.shape
    qa, ka, qd, kd = 0, 512, 768, 1280

    def pairs(lo, n_pairs):
        t = w_in[:, :, lo:lo + n_pairs * LANES].reshape(depth, d, n_pairs, 2, 2, HALF_DIM)
        return t.transpose(0, 1, 2, 4, 3, 5).reshape(depth, d, n_pairs * LANES)

    def dup_halves(lo):
        t = w_in[:, :, lo:lo + SWA_KV_HEADS * HEAD_DIM].reshape(depth, d, SWA_KV_HEADS, 2, 1, HALF_DIM)
        return jnp.broadcast_to(t, (depth, d, SWA_KV_HEADS, 2, 2, HALF_DIM)).reshape(depth, d, SWA_KV_HEADS * LANES)

    return jnp.concatenate([pairs(qa, 4), dup_halves(ka), pairs(qd, DIFF_HEADS), pairs(kd, DIFF_HEADS)], axis=-1)


def _value_rows(w_in):
    depth, d, _ = w_in.shape
    vd = jnp.swapaxes(w_in[:, :, VD_COL0:], 1, 2)
    va = jnp.swapaxes(w_in[:, :, VA_COL0:VA_COL0 + SWA_KV_HEADS * HEAD_DIM], 1, 2)
    va = jnp.broadcast_to(va.reshape(depth, SWA_KV_HEADS, 1, HEAD_DIM, d), (depth, SWA_KV_HEADS, 2, HEAD_DIM, d))
    return jnp.concatenate([vd, va.reshape(depth, SWA_KV_HEADS * LANES, d)], axis=1)


def _first_of_pair_mask():
    lane = lax.broadcasted_iota(jnp.int32, (1, LANES), 1)
    return (lane // HALF_DIM) % 2 == 0


def _silu(g):
    return g / (1.0 + jnp.exp(-g))


def _params(sem):
    return pltpu.CompilerParams(dimension_semantics=sem, vmem_limit_bytes=VMEM_LIMIT)


def _adaln_kernel(c_ref, w_ref, b_ref, o_ref):
    a = _silu(c_ref[...])
    o_ref[0] = jnp.dot(a, w_ref[0], preferred_element_type=F32,
                       precision=lax.Precision.HIGHEST) + b_ref[0]


def _adaln(c_pad, ada_w, ada_b):
    depth, d, cols = ada_w.shape
    tn = 1536
    return pl.pallas_call(
        _adaln_kernel,
        grid=(depth, cols // tn),
        in_specs=[pl.BlockSpec((8, d), lambda l, j: (0, 0)),
                  pl.BlockSpec((1, d, tn), lambda l, j: (l, 0, j)),
                  pl.BlockSpec((1, 1, tn), lambda l, j: (l, 0, j))],
        out_specs=pl.BlockSpec((1, 8, tn), lambda l, j: (l, 0, j)),
        out_shape=jax.ShapeDtypeStruct((depth, 8, cols), F32),
        compiler_params=_params(("arbitrary", "arbitrary")),
        name="adaln",
    )(c_pad, ada_w, ada_b.reshape(depth, 1, cols))


def _rope_table_kernel(pos_ref, cos_ref, sin_ref):
    lane = lax.broadcasted_iota(jnp.int32, (1, LANES), 1)
    freq = (lane % HALF_DIM).astype(F32)
    inv = 1.0 / (ROPE_THETA ** (2.0 * freq / HEAD_DIM))
    ang = pos_ref[...].astype(F32) * inv
    sign = jnp.where(lane < HEAD_DIM, -1.0, 1.0)
    cos_ref[...] = jnp.cos(ang)
    sin_ref[...] = jnp.sin(ang) * sign


def _rope_tables(positions):
    n = positions.size
    tm = 1024
    pos = positions.reshape(n, 1)
    return pl.pallas_call(
        _rope_table_kernel,
        grid=(n // tm,),
        in_specs=[pl.BlockSpec((tm, 1), lambda i: (i, 0))],
        out_specs=[pl.BlockSpec((tm, LANES), lambda i: (i, 0))] * 2,
        out_shape=[jax.ShapeDtypeStruct((n, LANES), F32)] * 2,
        compiler_params=_params(("arbitrary",)),
        name="rope_tables",
    )(pos)


def _rms_mod(x, g, scale, shift):
    y = x * lax.rsqrt(jnp.mean(x * x, axis=-1, keepdims=True) + EPS) * g
    return y * (1.0 + scale) + shift


def _inproj_kernel(x_ref, mod_ref, g_ref, w_ref, wvt_ref, cos_ref, sin_ref, o_ref, vt_ref):
    h = _rms_mod(x_ref[...], g_ref[...], mod_ref[0, 1:2, :], mod_ref[0, 0:1, :]).astype(BF16)
    proj = jnp.dot(h, w_ref[...], preferred_element_type=F32)
    vt_ref[...] = lax.dot_general(wvt_ref[...], h, (((1,), (1,)), ((), ())),
                                  preferred_element_type=F32).astype(BF16)
    cos = cos_ref[...]
    sin = sin_ref[...]
    for blk in range(ROPE_BLKS):
        t = proj[:, blk * LANES:(blk + 1) * LANES]
        r = t * cos + pltpu.roll(t, HEAD_DIM, axis=1) * sin
        if blk < KA_BLK or QD_BLK <= blk < KD_BLK:
            r = r * (Q_SCALE * LOG2E)
        o_ref[:, blk * LANES:(blk + 1) * LANES] = r.astype(BF16)


def _inproj(x2, mod, g, w, wvt, layer, cos_t, sin_t, seq):
    n, d = x2.shape
    tm = TM_PROJ
    per_b = seq // tm
    vt_rows = wvt.shape[1]
    return pl.pallas_call(
        _inproj_kernel,
        grid=(n // tm,),
        in_specs=[pl.BlockSpec((tm, d), lambda i: (i, 0)),
                  pl.BlockSpec((1, N_MOD, d), lambda i: (i // per_b, 0, 0)),
                  pl.BlockSpec((1, d), lambda i: (0, 0)),
                  pl.BlockSpec((None, d, PROJ_COLS), lambda i: (layer, 0, 0)),
                  pl.BlockSpec((None, vt_rows, d), lambda i: (layer, 0, 0)),
                  pl.BlockSpec((tm, LANES), lambda i: (i, 0)),
                  pl.BlockSpec((tm, LANES), lambda i: (i, 0))],
        out_specs=[pl.BlockSpec((tm, PROJ_COLS), lambda i: (i, 0)),
                   pl.BlockSpec((vt_rows, tm), lambda i: (0, i))],
        out_shape=[jax.ShapeDtypeStruct((n, PROJ_COLS), BF16),
                   jax.ShapeDtypeStruct((vt_rows, n), BF16)],
        compiler_params=_params(("arbitrary",)),
        name="inproj",
    )(x2, mod, g, w, wvt, cos_t, sin_t)


def _win_kernel(sink_ref, q_ref, k_ref, vt_ref, o_ref, qt_sc, *, tq, seq):
    i = pl.program_id(1)
    heads = SWA_Q_HEADS // SWA_KV_HEADS
    width = tq + 2 * WINDOW
    start = pl.multiple_of(jnp.clip(i * tq - WINDOW, 0, seq - width), LANES)
    first = _first_of_pair_mask()
    kpos = start + lax.broadcasted_iota(jnp.int32, (width, tq), 0)
    qpos = i * tq + lax.broadcasted_iota(jnp.int32, (width, tq), 1)
    valid = jnp.abs(qpos - kpos) <= WINDOW
    col = lax.broadcasted_iota(jnp.int32, (1, heads * tq), 1)
    lane = lax.broadcasted_iota(jnp.int32, (1, LANES), 1)
    for g in range(SWA_KV_HEADS):
        for blk in range(heads // 2):
            qb = (heads // 2) * g + blk
            q = q_ref[:, qb * LANES:(qb + 1) * LANES].astype(F32)
            zero = jnp.zeros_like(q)
            qt_sc[g, :, (2 * blk) * tq:(2 * blk + 1) * tq] = jnp.where(first, q, zero).T.astype(BF16)
            qt_sc[g, :, (2 * blk + 1) * tq:(2 * blk + 2) * tq] = jnp.where(first, zero, q).T.astype(BF16)

        s = jnp.dot(k_ref[pl.ds(start, width), g * LANES:(g + 1) * LANES], qt_sc[g],
                    preferred_element_type=F32)
        s = jnp.concatenate([jnp.where(valid, s[:, c * tq:(c + 1) * tq], -1e30) for c in range(heads)], axis=1)
        sk = jnp.zeros((1, heads * tq), F32)
        for c in range(heads):
            sk = jnp.where(col // tq == c, sink_ref[heads * g + c] * LOG2E, sk)
        m = jnp.maximum(jnp.max(s, axis=0, keepdims=True), sk)
        p = jnp.exp2(s - m)
        den = jnp.sum(p, axis=0, keepdims=True) + jnp.exp2(sk - m)
        o = jnp.dot(vt_ref[g * LANES:(g + 1) * LANES, pl.ds(start, width)], p.astype(BF16),
                    preferred_element_type=F32) / den
        for blk in range(heads // 2):
            qb = (heads // 2) * g + blk
            oa = o[:, (2 * blk) * tq:(2 * blk + 1) * tq].T
            ob = o[:, (2 * blk + 1) * tq:(2 * blk + 2) * tq].T
            o_ref[:, qb * LANES:(qb + 1) * LANES] = jnp.where(lane < HEAD_DIM, oa, ob).astype(o_ref.dtype)


def _window_attention(proj, vt, sink, batch, seq):
    n = proj.shape[0]
    tq = TQ_WIN
    nq = seq // tq
    heads = SWA_Q_HEADS // SWA_KV_HEADS
    q_cols = SWA_Q_HEADS // 2 * LANES
    kv_cols = SWA_KV_HEADS * LANES
    assert QA_BLK == 0 and KA_BLK * LANES % kv_cols == 0 and VT_WIN_BLK * LANES % kv_cols == 0
    kern = functools.partial(_win_kernel, tq=tq, seq=seq)
    return pl.pallas_call(
        kern,
        grid=(batch, nq),
        in_specs=[pl.BlockSpec(memory_space=pltpu.SMEM),
                  pl.BlockSpec((tq, q_cols), lambda b, i: (b * nq + i, 0)),
                  pl.BlockSpec((seq, kv_cols), lambda b, i: (b, KA_BLK * LANES // kv_cols)),
                  pl.BlockSpec((kv_cols, seq), lambda b, i: (VT_WIN_BLK * LANES // kv_cols, b))],
        out_specs=pl.BlockSpec((tq, q_cols), lambda b, i: (b * nq + i, 0)),
        out_shape=jax.ShapeDtypeStruct((n, q_cols), BF16),
        scratch_shapes=[pltpu.VMEM((SWA_KV_HEADS, LANES, heads * tq), BF16)],
        compiler_params=_params(("arbitrary", "arbitrary")),
        name="window_attn",
    )(sink, proj, proj, vt)


def _diff_kernel(lam_ref, g_ref, q_ref, k_ref, vt_ref, o_ref, qt_sc, s_a, s_b, cm_a, cm_b, m_sc, l_sc, acc_sc,
                 *, tk, seq, lambda_init):
    tq = q_ref.shape[0]
    nk = seq // tk
    q = q_ref[...].astype(F32)
    first = _first_of_pair_mask()
    zero = jnp.zeros_like(q)
    qt_sc[:, :tq] = jnp.where(first, q, zero).T.astype(BF16)
    qt_sc[:, tq:] = jnp.where(first, zero, q).T.astype(BF16)

    m_sc[...] = jnp.full(m_sc.shape, -1e30, F32)
    l_sc[...] = jnp.zeros(l_sc.shape, F32)
    acc_sc[...] = jnp.zeros(acc_sc.shape, F32)

    def scores(t, s_ref, cm_ref):
        off = pl.multiple_of(t * tk, tk)
        s = jnp.dot(k_ref[pl.ds(off, tk), :], qt_sc[...], preferred_element_type=F32)
        s_ref[...] = s
        cm_ref[...] = jnp.max(s, axis=0, keepdims=True)

    def consume(t, s_ref, cm_ref):
        off = pl.multiple_of(t * tk, tk)
        m_prev = m_sc[...]
        m_new = jnp.maximum(m_prev, cm_ref[...])
        alpha = jnp.exp2(m_prev - m_new)
        p = jnp.exp2(s_ref[...] - m_new)
        l_sc[...] = alpha * l_sc[...] + jnp.sum(p, axis=0, keepdims=True)
        vb = vt_ref[:, pl.ds(off, tk)]
        acc_sc[...] = alpha * acc_sc[...] + jnp.dot(vb, p.astype(BF16), preferred_element_type=F32)
        m_sc[...] = m_new

    scores(0, s_a, cm_a)

    def body(u, carry):
        t = 2 * u
        scores(t + 1, s_b, cm_b)
        consume(t, s_a, cm_a)
        scores(t + 2, s_a, cm_a)
        consume(t + 1, s_b, cm_b)
        return carry

    lax.fori_loop(0, nk // 2 - 1, body, 0)
    scores(nk - 1, s_b, cm_b)
    consume(nk - 2, s_a, cm_a)
    consume(nk - 1, s_b, cm_b)

    lam1 = jnp.sum(lam_ref[0:1, :] * lam_ref[1:2, :], axis=-1, keepdims=True)
    lam2 = jnp.sum(lam_ref[2:3, :] * lam_ref[3:4, :], axis=-1, keepdims=True)
    lam = jnp.exp(lam1) - jnp.exp(lam2) + lambda_init
    o = acc_sc[...] / l_sc[...]
    o = o[:, :tq] - lam * o[:, tq:]
    y = o * lax.rsqrt(jnp.mean(o * o, axis=0, keepdims=True) + EPS)
    y = y.T * (g_ref[...] * (1.0 - lambda_init))
    o_ref[...] = y.astype(o_ref.dtype)


def _diff_attention(proj, vt, lam_vecs, subln_g, batch, seq, lambda_init):
    n = proj.shape[0]
    tq, tk = TQ_DIFF, TK_DIFF
    nq = seq // tq
    assert seq % (2 * tk) == 0 and seq % tq == 0, "the pipelined key loop consumes key blocks in pairs"
    kern = functools.partial(_diff_kernel, tk=tk, seq=seq, lambda_init=lambda_init)
    return pl.pallas_call(
        kern,
        grid=(batch, DIFF_HEADS, nq),
        in_specs=[pl.BlockSpec((4, HEAD_DIM), lambda b, h, i: (0, 0)),
                  pl.BlockSpec((1, LANES), lambda b, h, i: (0, 0)),
                  pl.BlockSpec((tq, LANES), lambda b, h, i: (b * nq + i, QD_BLK + h)),
                  pl.BlockSpec((seq, LANES), lambda b, h, i: (b, KD_BLK + h)),
                  pl.BlockSpec((LANES, seq), lambda b, h, i: (h, b))],
        out_specs=pl.BlockSpec((tq, LANES), lambda b, h, i: (b * nq + i, h)),
        out_shape=jax.ShapeDtypeStruct((n, DIFF_HEADS * LANES), BF16),
        scratch_shapes=[pltpu.VMEM((LANES, 2 * tq), BF16),
                        pltpu.VMEM((tk, 2 * tq), F32), pltpu.VMEM((tk, 2 * tq), F32),
                        pltpu.VMEM((1, 2 * tq), F32), pltpu.VMEM((1, 2 * tq), F32),
                        pltpu.VMEM((1, 2 * tq), F32), pltpu.VMEM((1, 2 * tq), F32),
                        pltpu.VMEM((LANES, 2 * tq), F32)],
        compiler_params=_params(("arbitrary", "arbitrary", "arbitrary")),
        name="diff_attn",
    )(lam_vecs, subln_g, proj, proj, vt)


def _route_top2(logits):
    lane = lax.broadcasted_iota(jnp.int32, logits.shape, 1)
    lg = jnp.where(lane < N_EXPERTS, logits, -jnp.inf)
    v1 = jnp.max(lg, axis=-1, keepdims=True)
    i1 = jnp.min(jnp.where(lg == v1, lane, LANES), axis=-1, keepdims=True)
    lg2 = jnp.where(lane == i1, -jnp.inf, lg)
    v2 = jnp.max(lg2, axis=-1, keepdims=True)
    i2 = jnp.min(jnp.where(lg2 == v2, lane, LANES), axis=-1, keepdims=True)
    e = jnp.exp(v2 - v1)
    w1 = 1.0 / (1.0 + e)
    w2 = e / (1.0 + e)
    idx = jnp.where(lane == 0, i1, jnp.where(lane == 1, i2, 0))
    wgt = jnp.where(lane == 0, w1, jnp.where(lane == 1, w2, 0.0))
    return idx, wgt


def _outproj_kernel(ya_ref, yd_ref, w_ref, x_ref, mod_ref, g_ref, *rest, moe):
    if moe:
        rwh_ref, rwl_ref, x1_ref, h_ref, idx_ref, wgt_ref = rest
    else:
        x1_ref, h_ref = rest
    half = ya_ref.shape[1]
    y = (jnp.dot(ya_ref[...], w_ref[:half, :], preferred_element_type=F32)
         + jnp.dot(yd_ref[...], w_ref[half:, :], preferred_element_type=F32))
    x1 = x_ref[...] + mod_ref[0, 2:3, :] * y
    x1_ref[...] = x1
    h = _rms_mod(x1, g_ref[...], mod_ref[0, 4:5, :], mod_ref[0, 3:4, :])
    if moe:
        for c in range(SUBROWS):
            h_ref[pl.ds(c, h.shape[0], stride=SUBROWS), :] = h[:, c * LANES:(c + 1) * LANES]
    else:
        h_ref[...] = h.astype(h_ref.dtype)
    if moe:
        h_hi = h.astype(BF16)
        h_lo = (h - h_hi.astype(F32)).astype(BF16)
        logits = (jnp.dot(h_hi, rwh_ref[...], preferred_element_type=F32)
                  + jnp.dot(h_hi, rwl_ref[...], preferred_element_type=F32)
                  + jnp.dot(h_lo, rwh_ref[...], preferred_element_type=F32))
        idx, wgt = _route_top2(logits)
        idx_ref[...] = idx
        wgt_ref[...] = wgt


def _outproj(ya, yd, w, layer, x2, mod, g, seq, router_w=None):
    n, d = x2.shape
    tm = TM_PROJ
    per_b = seq // tm
    moe = router_w is not None
    row = lambda i: (i, 0)
    in_specs = [pl.BlockSpec((tm, ya.shape[1]), row),
                pl.BlockSpec((tm, yd.shape[1]), row),
                pl.BlockSpec((None,) + w.shape[1:], lambda i: (layer, 0, 0)),
                pl.BlockSpec((tm, d), row),
                pl.BlockSpec((1, N_MOD, d), lambda i: (i // per_b, 0, 0)),
                pl.BlockSpec((1, d), lambda i: (0, 0))]
    out_specs = [pl.BlockSpec((tm, d), row), pl.BlockSpec(_to_subrows((tm, d)) if moe else (tm, d), row)]
    out_shape = [jax.ShapeDtypeStruct((n, d), F32),
                 jax.ShapeDtypeStruct(_to_subrows((n, d)), F32) if moe else jax.ShapeDtypeStruct((n, d), BF16)]
    args = [ya, yd, w, x2, mod, g]
    if moe:
        in_specs += [pl.BlockSpec((d, LANES), lambda i: (0, 0))] * 2
        out_specs += [pl.BlockSpec((tm, LANES), row)] * 2
        out_shape += [jax.ShapeDtypeStruct((n, LANES), jnp.int32), jax.ShapeDtypeStruct((n, LANES), F32)]
        args += list(router_w)
    return pl.pallas_call(
        functools.partial(_outproj_kernel, moe=moe),
        grid=(n // tm,),
        in_specs=in_specs, out_specs=out_specs, out_shape=out_shape,
        compiler_params=_params(("arbitrary",)),
        name="outproj_moe" if moe else "outproj",
    )(*args)


def _swiglu_step(h, wg_ref, wu_ref, wd_ref):
    g = jnp.dot(h, wg_ref[...].astype(BF16), preferred_element_type=F32)
    u = jnp.dot(h, wu_ref[...].astype(BF16), preferred_element_type=F32)
    return jnp.dot((_silu(g) * u).astype(BF16), wd_ref[...].astype(BF16), preferred_element_type=F32)


def _ffn_kernel(h_ref, wg_ref, wu_ref, wd_ref, x_ref, mod_ref, o_ref, acc_ref):
    j = pl.program_id(1)

    @pl.when(j == 0)
    def _():
        acc_ref[...] = jnp.zeros(acc_ref.shape, F32)

    acc_ref[...] += _swiglu_step(h_ref[...], wg_ref, wu_ref, wd_ref)

    @pl.when(j == pl.num_programs(1) - 1)
    def _():
        o_ref[...] = x_ref[...] + mod_ref[0, 5:6, :] * acc_ref[...]


def _dense_ffn(h, wg, wu, wd, li, x1, mod, seq):
    n, d = x1.shape
    f = wg.shape[2]
    tm, tf = TM_FFN, TF_FFN
    per_b = seq // tm
    return pl.pallas_call(
        _ffn_kernel,
        grid=(n // tm, f // tf),
        in_specs=[pl.BlockSpec((tm, d), lambda i, j: (i, 0)),
                  pl.BlockSpec((None, d, tf), lambda i, j: (li, 0, j)),
                  pl.BlockSpec((None, d, tf), lambda i, j: (li, 0, j)),
                  pl.BlockSpec((None, tf, d), lambda i, j: (li, j, 0)),
                  pl.BlockSpec((tm, d), lambda i, j: (i, 0)),
                  pl.BlockSpec((1, N_MOD, d), lambda i, j: (i // per_b, 0, 0))],
        out_specs=pl.BlockSpec((tm, d), lambda i, j: (i, 0)),
        out_shape=jax.ShapeDtypeStruct((n, d), F32),
        scratch_shapes=[pltpu.VMEM((tm, d), F32)],
        compiler_params=_params(("arbitrary", "arbitrary")),
        name="dense_ffn",
    )(h, wg, wu, wd, x1, mod)


def _moe_plan(idx, tm):
    n = idx.shape[0]
    flat_e = idx.reshape(-1)
    onehot = (flat_e[:, None] == jnp.arange(N_EXPERTS, dtype=jnp.int32)[None, :]).astype(jnp.int32)
    csum = jnp.cumsum(onehot, axis=0)
    rank = jnp.sum((csum - onehot) * onehot, axis=1)
    counts = csum[-1]
    padded = (counts + tm - 1) // tm * tm
    pad_ends = jnp.cumsum(padded)
    pad_starts = pad_ends - padded
    dest = (pad_starts[flat_e] + rank).astype(jnp.int32)
    rows = 2 * n + N_EXPERTS * tm
    n_tiles = (pad_ends[-1] // tm).astype(jnp.int32).reshape(1)
    tile_start = jnp.arange(rows // tm, dtype=jnp.int32) * tm
    tile_e = jnp.minimum(jnp.sum((pad_ends[None, :] <= tile_start[:, None]).astype(jnp.int32), axis=1),
                         N_EXPERTS - 1).astype(jnp.int32)
    tile_rows = jnp.clip((pad_starts + counts)[tile_e] - tile_start, 0, tm).astype(jnp.int32)
    return dest, pad_ends.astype(jnp.int32), tile_e, tile_rows, n_tiles, rows


def _to_subrows(ref_or_shape):
    rows, d = ref_or_shape
    assert d == SUBROWS * LANES
    return (rows * SUBROWS, LANES)


def _row_copy(src_hbm, src_row, dst_ref, dst_row, sem):
    s = pl.multiple_of(src_row * SUBROWS, SUBROWS)
    t = pl.multiple_of(dst_row * SUBROWS, SUBROWS)
    return pltpu.make_async_copy(src_hbm.at[pl.ds(s, SUBROWS)], dst_ref.at[pl.ds(t, SUBROWS)], sem)


def _row_gather(idx_fns, src_hbm, dst_refs, sem, rows):
    def issue(r, carry):
        for k, (idx_of, dst_ref) in enumerate(zip(idx_fns, dst_refs)):
            _row_copy(src_hbm, idx_of(r), dst_ref, r, sem).start(priority=k % 2)
        return carry

    lax.fori_loop(0, rows, issue, 0, unroll=4)


def _row_gather_wait(src_hbm, dst_ref, sem, rows):
    pltpu.make_async_copy(src_hbm.at[pl.ds(0, rows * SUBROWS)], dst_ref, sem).wait()


def _moe_scatter_kernel(dest_ref, pe_ref, nt_ref, h_ref, xs_hbm, zeros, sem, *, tb, tm, n_tiles_max):
    i = pl.program_id(0)
    zero_sem, row_sem = sem.at[0], sem.at[1]
    tile_rows = tm * SUBROWS

    def zero_tile(t):
        start = pl.multiple_of(t * tile_rows, tile_rows)
        return pltpu.make_async_copy(zeros, xs_hbm.at[pl.ds(start, tile_rows)], zero_sem)

    @pl.when(i == 0)
    def _():
        zeros[...] = jnp.zeros(zeros.shape, zeros.dtype)
        for e in range(N_EXPERTS):
            last_tile = jnp.maximum(pe_ref[e] // tm - 1, 0)
            zero_tile(last_tile).start()
            zero_tile(last_tile).wait()
        for t in range(n_tiles_max):
            @pl.when(t >= nt_ref[0])
            def _():
                zero_tile(t).start()
                zero_tile(t).wait()

    base = i * tb

    def issue(r, carry):
        src = h_ref.at[pl.ds(pl.multiple_of(r * SUBROWS, SUBROWS), SUBROWS)]
        for k in range(2):
            d = dest_ref[2 * (base + r) + k]
            pltpu.make_async_copy(src, xs_hbm.at[pl.ds(pl.multiple_of(d * SUBROWS, SUBROWS), SUBROWS)],
                                  row_sem).start(priority=k)
        return carry

    lax.fori_loop(0, tb, issue, 0, unroll=4)
    for k in range(2):
        pltpu.make_async_copy(h_ref, xs_hbm.at[pl.ds(0, tb * SUBROWS)], row_sem).wait()


def _moe_scatter(h_sub, dest, pad_ends, n_tiles, rows_total, tm):
    n_sub = h_sub.shape[0]
    tb = TM_COMB
    d = SUBROWS * LANES
    n_tiles_max = rows_total // tm
    return pl.pallas_call(
        functools.partial(_moe_scatter_kernel, tb=tb, tm=tm, n_tiles_max=n_tiles_max),
        grid_spec=pltpu.PrefetchScalarGridSpec(
            num_scalar_prefetch=3,
            grid=(n_sub // (tb * SUBROWS),),
            in_specs=[pl.BlockSpec(_to_subrows((tb, d)), lambda i, de, pe, nt: (i, 0))],
            out_specs=pl.BlockSpec(memory_space=pl.ANY),
            scratch_shapes=[pltpu.VMEM(_to_subrows((tm, d)), F32), pltpu.SemaphoreType.DMA((2,))]),
        out_shape=jax.ShapeDtypeStruct(_to_subrows((rows_total, d)), F32),
        compiler_params=_params(("arbitrary",)),
        name="moe_scatter",
    )(dest, pad_ends, n_tiles, h_sub)


def _moe_ffn_kernel(te_ref, tr_ref, nt_ref, xs_ref, wg_ref, wu_ref, wd_ref, o_ref, xb16, acc_ref, *, nf):
    i = pl.program_id(0)
    j = pl.program_id(1)
    tm = xb16.shape[0]
    half = tm // 2

    @pl.when(i < nt_ref[0])
    def _():
        @pl.when(j == 0)
        def _():
            for c in range(SUBROWS):
                xb16[:, c * LANES:(c + 1) * LANES] = xs_ref[pl.ds(c, tm, stride=SUBROWS), :].astype(BF16)
            acc_ref[...] = jnp.zeros(acc_ref.shape, F32)

        @pl.when(tr_ref[i] > half)
        def _():
            acc_ref[...] += _swiglu_step(xb16[...], wg_ref.at[0], wu_ref.at[0], wd_ref.at[0])

        @pl.when(tr_ref[i] <= half)
        def _():
            acc_ref[:half, :] += _swiglu_step(xb16[:half, :], wg_ref.at[0], wu_ref.at[0], wd_ref.at[0])

        @pl.when(j == nf - 1)
        def _():
            for c in range(SUBROWS):
                o_ref[pl.ds(c, tm, stride=SUBROWS), :] = acc_ref[:, c * LANES:(c + 1) * LANES]

    @pl.when(jnp.logical_and(i >= nt_ref[0], j == nf - 1))
    def _():
        o_ref[...] = jnp.zeros(o_ref.shape, o_ref.dtype)


def _moe_ffn(xs, wg, wu, wd, li, tile_e, tile_rows, n_tiles, tm):
    d = wg.shape[2]
    f = wg.shape[3]
    rows_total = xs.shape[0] // SUBROWS
    tf = TF_FFN
    nf = f // tf

    def row_map(i, j, te, tr, nt):
        return (jnp.minimum(i, nt[0] - 1), 0)

    def _fj(i, j, nt):
        return jnp.where(i < nt[0], j, nf - 1)

    def up_map(i, j, te, tr, nt):
        return (li, te[jnp.minimum(i, nt[0] - 1)], 0, _fj(i, j, nt))

    def down_map(i, j, te, tr, nt):
        return (li, te[jnp.minimum(i, nt[0] - 1)], _fj(i, j, nt), 0)

    return pl.pallas_call(
        functools.partial(_moe_ffn_kernel, nf=nf),
        grid_spec=pltpu.PrefetchScalarGridSpec(
            num_scalar_prefetch=3,
            grid=(rows_total // tm, nf),
            in_specs=[pl.BlockSpec(_to_subrows((tm, d)), row_map),
                      pl.BlockSpec((None, 1, d, tf), up_map),
                      pl.BlockSpec((None, 1, d, tf), up_map),
                      pl.BlockSpec((None, 1, tf, d), down_map)],
            out_specs=pl.BlockSpec(_to_subrows((tm, d)), lambda i, j, te, tr, nt: (i, 0)),
            scratch_shapes=[pltpu.VMEM((tm, d), BF16), pltpu.VMEM((tm, d), F32)]),
        out_shape=jax.ShapeDtypeStruct(_to_subrows((rows_total, d)), F32),
        compiler_params=_params(("arbitrary", "arbitrary")),
        name="moe_ffn",
    )(tile_e, tile_rows, n_tiles, xs, wg, wu, wd)


def _combine_kernel(pos_ref, ys_hbm, wgt_ref, x_ref, mod_ref, fg_ref, o_ref, buf, sem, *, tm, final):
    i = pl.program_id(0)
    slot = i % 2

    def gather(step, to_slot):
        _row_gather([lambda r: pos_ref[2 * (step * tm + r)], lambda r: pos_ref[2 * (step * tm + r) + 1]],
                    ys_hbm, [buf.at[to_slot, 0], buf.at[to_slot, 1]], sem.at[to_slot], tm)

    @pl.when(i == 0)
    def _():
        gather(0, 0)

    @pl.when(i + 1 < pl.num_programs(0))
    def _():
        gather(i + 1, 1 - slot)

    for k in range(2):
        _row_gather_wait(ys_hbm, buf.at[slot, k], sem.at[slot], tm)

    wgt = wgt_ref[...]
    w0, w1 = wgt[:, 0:1], wgt[:, 1:2]
    y = jnp.concatenate([w0 * buf[slot, 0, pl.ds(c, tm, stride=SUBROWS), :]
                         + w1 * buf[slot, 1, pl.ds(c, tm, stride=SUBROWS), :] for c in range(SUBROWS)], axis=1)
    out = x_ref[...] + mod_ref[0, 5:6, :] * y
    if final:
        out = out * lax.rsqrt(jnp.mean(out * out, axis=-1, keepdims=True) + EPS) * fg_ref[...]
    o_ref[...] = out


def _moe_combine(ys, pos, wgt, x1, mod, final_g, seq, final):
    n, d = x1.shape
    tm = TM_COMB
    per_b = seq // tm
    return pl.pallas_call(
        functools.partial(_combine_kernel, tm=tm, final=final),
        grid_spec=pltpu.PrefetchScalarGridSpec(
            num_scalar_prefetch=1,
            grid=(n // tm,),
            in_specs=[pl.BlockSpec(memory_space=pl.ANY),
                      pl.BlockSpec((tm, LANES), lambda i, p: (i, 0)),
                      pl.BlockSpec((tm, d), lambda i, p: (i, 0)),
                      pl.BlockSpec((1, N_MOD, d), lambda i, p: (i // per_b, 0, 0)),
                      pl.BlockSpec((1, d), lambda i, p: (0, 0))],
            out_specs=pl.BlockSpec((tm, d), lambda i, p: (i, 0)),
            scratch_shapes=[pltpu.VMEM((2, 2) + _to_subrows((tm, d)), F32),
                            pltpu.SemaphoreType.DMA((2,))]),
        out_shape=jax.ShapeDtypeStruct((n, d), F32),
        compiler_params=_params(("arbitrary",)),
        name="moe_combine_final" if final else "moe_combine",
    )(pos.reshape(-1), ys, wgt, x1, mod, final_g)


def _final_norm_kernel(x_ref, g_ref, o_ref):
    x = x_ref[...]
    o_ref[...] = x * lax.rsqrt(jnp.mean(x * x, axis=-1, keepdims=True) + EPS) * g_ref[...]


def _final_norm(x2, g):
    n, d = x2.shape
    tm = TM_PROJ
    return pl.pallas_call(
        _final_norm_kernel,
        grid=(n // tm,),
        in_specs=[pl.BlockSpec((tm, d), lambda i: (i, 0)), pl.BlockSpec((1, d), lambda i: (0, 0))],
        out_specs=pl.BlockSpec((tm, d), lambda i: (i, 0)),
        out_shape=jax.ShapeDtypeStruct((n, d), F32),
        compiler_params=_params(("arbitrary",)),
        name="final_norm",
    )(x2, g)


def kernel(x, c, positions, ada_w, ada_b, norm_mix_g, norm_ffn_g, w_in, w_out, sink, lam_q1, lam_k1, lam_q2, lam_k2, subln_g, ffn_w_gate, ffn_w_up, ffn_w_down, router_w, moe_w_gate, moe_w_up, moe_w_down, final_g):
    batch, seq, d = x.shape
    depth = w_in.shape[0]
    n = batch * seq

    c_pad = jnp.zeros((8, d), F32).at[:batch].set(c)
    mod_all = _adaln(c_pad, ada_w, ada_b)
    cos_t, sin_t = _rope_tables(positions)
    w_in_p = _permute_in_proj(w_in).astype(BF16)
    w_vdt = _value_rows(w_in).astype(BF16)
    w_out_b = w_out.astype(BF16)
    ffn_w = [w.astype(BF16) for w in (ffn_w_gate, ffn_w_up, ffn_w_down)]
    moe_w = [moe_w_gate, moe_w_up, moe_w_down]
    final_g2 = final_g.reshape(1, d)

    x2 = x.reshape(n, d)
    for layer in range(depth):
        lambda_init = 0.8 - 0.6 * math.exp(-0.3 * layer)
        mod = mod_all[layer, :batch].reshape(batch, N_MOD, d)
        proj, vt = _inproj(x2, mod, norm_mix_g[layer].reshape(1, d), w_in_p, w_vdt, layer, cos_t, sin_t, seq)
        ya = _window_attention(proj, vt, sink[layer], batch, seq)
        lam_vecs = jnp.stack([lam_q1[layer], lam_k1[layer], lam_q2[layer], lam_k2[layer]])
        yd = _diff_attention(proj, vt, lam_vecs, subln_g[layer].reshape(1, LANES), batch, seq, lambda_init)
        i = layer // 2
        g_ffn = norm_ffn_g[layer].reshape(1, d)
        if layer % 2 == 0:
            x1, h = _outproj(ya, yd, w_out_b, layer, x2, mod, g_ffn, seq)
            x2 = _dense_ffn(h, *ffn_w, i, x1, mod, seq)
        else:
            rw = jnp.zeros((d, LANES), F32).at[:, :N_EXPERTS].set(router_w[i])
            rw_hi = rw.astype(BF16)
            rw_lo = (rw - rw_hi.astype(F32)).astype(BF16)
            x1, h, idx, wgt = _outproj(ya, yd, w_out_b, layer, x2, mod, g_ffn, seq, router_w=(rw_hi, rw_lo))
            dest, pad_ends, tile_e, tile_rows, n_tiles, rows_total = _moe_plan(idx[:, :2], TM_MOE)
            pos = dest.reshape(n, 2)
            xs = _moe_scatter(h, dest, pad_ends, n_tiles, rows_total, TM_MOE)
            ys = _moe_ffn(xs, *moe_w, i, tile_e, tile_rows, n_tiles, TM_MOE)
            final = layer == depth - 1
            x2 = _moe_combine(ys, pos, wgt, x1, mod, final_g2, seq, final)
    if depth % 2 == 1:
        x2 = _final_norm(x2, final_g2)
    return x2.reshape(batch, seq, d)
```

```python
import functools
import math

import jax
import jax.numpy as jnp
from jax import lax
from jax.experimental import pallas as pl
from jax.experimental.pallas import tpu as pltpu

F32 = jnp.float32
BF16 = jnp.bfloat16

HEAD_DIM = 64
HALF_DIM = HEAD_DIM // 2
ROPE_THETA = 10000.0
EPS = 1e-6
SWA_Q_HEADS = 8
SWA_KV_HEADS = 2
WINDOW = 128
DIFF_HEADS = 4
N_EXPERTS = 8
N_MOD = 6
LANES = 128
SUBROWS = 8
VMEM_LIMIT = 56 * 1024 * 1024

QA_BLK, KA_BLK, QD_BLK, KD_BLK = 0, 4, 6, 10
ROPE_BLKS = 14
PROJ_COLS = ROPE_BLKS * LANES
VA_COL0 = 640
VD_COL0 = 1792
VT_WIN_BLK = 4
Q_SCALE = HEAD_DIM ** -0.5
LOG2E = math.log2(math.e)

TM_PROJ = 512
TQ_WIN = 256
TQ_DIFF = 512
TK_DIFF = 1024
TM_FFN = 1024
TF_FFN = 512
TM_MOE = 1024
TM_COMB = 256


def _permute_in_proj(w_in):
    depth, d, _ = w_in.shape
    qa, ka, qd, kd = 0, 512, 768, 1280

    def pairs(lo, n_pairs):
        t = w_in[:, :, lo:lo + n_pairs * LANES].reshape(depth, d, n_pairs, 2, 2, HALF_DIM)
        return t.transpose(0, 1, 2, 4, 3, 5).reshape(depth, d, n_pairs * LANES)

    def dup_halves(lo):
        t = w_in[:, :, lo:lo + SWA_KV_HEADS * HEAD_DIM].reshape(depth, d, SWA_KV_HEADS, 2, 1, HALF_DIM)
        return jnp.broadcast_to(t, (depth, d, SWA_KV_HEADS, 2, 2, HALF_DIM)).reshape(depth, d, SWA_KV_HEADS * LANES)

    return jnp.concatenate([pairs(qa, 4), dup_halves(ka), pairs(qd, DIFF_HEADS), pairs(kd, DIFF_HEADS)], axis=-1)


def _value_rows(w_in):
    depth, d, _ = w_in.shape
    vd = jnp.swapaxes(w_in[:, :, VD_COL0:], 1, 2)
    va = jnp.swapaxes(w_in[:, :, VA_COL0:VA_COL0 + SWA_KV_HEADS * HEAD_DIM], 1, 2)
    va = jnp.broadcast_to(va.reshape(depth, SWA_KV_HEADS, 1, HEAD_DIM, d), (depth, SWA_KV_HEADS, 2, HEAD_DIM, d))
    return jnp.concatenate([vd, va.reshape(depth, SWA_KV_HEADS * LANES, d)], axis=1)


def _first_of_pair_mask():
    lane = lax.broadcasted_iota(jnp.int32, (1, LANES), 1)
    return (lane // HALF_DIM) % 2 == 0


def _silu(g):
    return g / (1.0 + jnp.exp(-g))


def _params(sem):
    return pltpu.CompilerParams(dimension_semantics=sem, vmem_limit_bytes=VMEM_LIMIT)


def _adaln_kernel(c_ref, w_ref, b_ref, o_ref):
    a = _silu(c_ref[...])
    o_ref[0] = jnp.dot(a, w_ref[0], preferred_element_type=F32,
                       precision=lax.Precision.HIGHEST) + b_ref[0]


def _adaln(c_pad, ada_w, ada_b):
    depth, d, cols = ada_w.shape
    tn = 1536
    return pl.pallas_call(
        _adaln_kernel,
        grid=(depth, cols // tn),
        in_specs=[pl.BlockSpec((8, d), lambda l, j: (0, 0)),
                  pl.BlockSpec((1, d, tn), lambda l, j: (l, 0, j)),
                  pl.BlockSpec((1, 1, tn), lambda l, j: (l, 0, j))],
        out_specs=pl.BlockSpec((1, 8, tn), lambda l, j: (l, 0, j)),
        out_shape=jax.ShapeDtypeStruct((depth, 8, cols), F32),
        compiler_params=_params(("arbitrary", "arbitrary")),
        name="adaln",
    )(c_pad, ada_w, ada_b.reshape(depth, 1, cols))


def _rope_table_kernel(pos_ref, cos_ref, sin_ref):
    lane = lax.broadcasted_iota(jnp.int32, (1, LANES), 1)
    freq = (lane % HALF_DIM).astype(F32)
    inv = 1.0 / (ROPE_THETA ** (2.0 * freq / HEAD_DIM))
    ang = pos_ref[...].astype(F32) * inv
    sign = jnp.where(lane < HEAD_DIM, -1.0, 1.0)
    cos_ref[...] = jnp.cos(ang)
    sin_ref[...] = jnp.sin(ang) * sign


def _rope_tables(positions):
    n = positions.size
    tm = 1024
    pos = positions.reshape(n, 1)
    return pl.pallas_call(
        _rope_table_kernel,
        grid=(n // tm,),
        in_specs=[pl.BlockSpec((tm, 1), lambda i: (i, 0))],
        out_specs=[pl.BlockSpec((tm, LANES), lambda i: (i, 0))] * 2,
        out_shape=[jax.ShapeDtypeStruct((n, LANES), F32)] * 2,
        compiler_params=_params(("arbitrary",)),
        name="rope_tables",
    )(pos)


def _rms_mod(x, g, scale, shift):
    y = x * lax.rsqrt(jnp.mean(x * x, axis=-1, keepdims=True) + EPS) * g
    return y * (1.0 + scale) + shift


def _inproj_kernel(x_ref, mod_ref, g_ref, w_ref, wvt_ref, cos_ref, sin_ref, o_ref, vt_ref):
    h = _rms_mod(x_ref[...], g_ref[...], mod_ref[0, 1:2, :], mod_ref[0, 0:1, :]).astype(BF16)
    proj = jnp.dot(h, w_ref[...], preferred_element_type=F32)
    vt_ref[...] = lax.dot_general(wvt_ref[...], h, (((1,), (1,)), ((), ())),
                                  preferred_element_type=F32).astype(BF16)
    cos = cos_ref[...]
    sin = sin_ref[...]
    for blk in range(ROPE_BLKS):
        t = proj[:, blk * LANES:(blk + 1) * LANES]
        r = t * cos + pltpu.roll(t, HEAD_DIM, axis=1) * sin
        if blk < KA_BLK or QD_BLK <= blk < KD_BLK:
            r = r * (Q_SCALE * LOG2E)
        o_ref[:, blk * LANES:(blk + 1) * LANES] = r.astype(BF16)


def _inproj(x2, mod, g, w, wvt, layer, cos_t, sin_t, seq):
    n, d = x2.shape
    tm = TM_PROJ
    per_b = seq // tm
    vt_rows = wvt.shape[1]
    return pl.pallas_call(
        _inproj_kernel,
        grid=(n // tm,),
        in_specs=[pl.BlockSpec((tm, d), lambda i: (i, 0)),
                  pl.BlockSpec((1, N_MOD, d), lambda i: (i // per_b, 0, 0)),
                  pl.BlockSpec((1, d), lambda i: (0, 0)),
                  pl.BlockSpec((None, d, PROJ_COLS), lambda i: (layer, 0, 0)),
                  pl.BlockSpec((None, vt_rows, d), lambda i: (layer, 0, 0)),
                  pl.BlockSpec((tm, LANES), lambda i: (i, 0)),
                  pl.BlockSpec((tm, LANES), lambda i: (i, 0))],
        out_specs=[pl.BlockSpec((tm, PROJ_COLS), lambda i: (i, 0)),
                   pl.BlockSpec((vt_rows, tm), lambda i: (0, i))],
        out_shape=[jax.ShapeDtypeStruct((n, PROJ_COLS), BF16),
                   jax.ShapeDtypeStruct((vt_rows, n), BF16)],
        compiler_params=_params(("arbitrary",)),
        name="inproj",
    )(x2, mod, g, w, wvt, cos_t, sin_t)


def _win_kernel(sink_ref, q_ref, k_ref, vt_ref, o_ref, qt_sc, *, tq, seq):
    i = pl.program_id(1)
    heads = SWA_Q_HEADS // SWA_KV_HEADS
    width = tq + 2 * WINDOW
    start = pl.multiple_of(jnp.clip(i * tq - WINDOW, 0, seq - width), LANES)
    first = _first_of_pair_mask()
    kpos = start + lax.broadcasted_iota(jnp.int32, (width, tq), 0)
    qpos = i * tq + lax.broadcasted_iota(jnp.int32, (width, tq), 1)
    valid = jnp.abs(qpos - kpos) <= WINDOW
    col = lax.broadcasted_iota(jnp.int32, (1, heads * tq), 1)
    lane = lax.broadcasted_iota(jnp.int32, (1, LANES), 1)
    for g in range(SWA_KV_HEADS):
        for blk in range(heads // 2):
            qb = (heads // 2) * g + blk
            q = q_ref[:, qb * LANES:(qb + 1) * LANES].astype(F32)
            zero = jnp.zeros_like(q)
            qt_sc[g, :, (2 * blk) * tq:(2 * blk + 1) * tq] = jnp.where(first, q, zero).T.astype(BF16)
            qt_sc[g, :, (2 * blk + 1) * tq:(2 * blk + 2) * tq] = jnp.where(first, zero, q).T.astype(BF16)

        s = jnp.dot(k_ref[pl.ds(start, width), g * LANES:(g + 1) * LANES], qt_sc[g],
                    preferred_element_type=F32)
        s = jnp.concatenate([jnp.where(valid, s[:, c * tq:(c + 1) * tq], -1e30) for c in range(heads)], axis=1)
        sk = jnp.zeros((1, heads * tq), F32)
        for c in range(heads):
            sk = jnp.where(col // tq == c, sink_ref[heads * g + c] * LOG2E, sk)
        m = jnp.maximum(jnp.max(s, axis=0, keepdims=True), sk)
        p = jnp.exp2(s - m)
        den = jnp.sum(p, axis=0, keepdims=True) + jnp.exp2(sk - m)
        o = jnp.dot(vt_ref[g * LANES:(g + 1) * LANES, pl.ds(start, width)], p.astype(BF16),
                    preferred_element_type=F32) / den
        for blk in range(heads // 2):
            qb = (heads // 2) * g + blk
            oa = o[:, (2 * blk) * tq:(2 * blk + 1) * tq].T
            ob = o[:, (2 * blk + 1) * tq:(2 * blk + 2) * tq].T
            o_ref[:, qb * LANES:(qb + 1) * LANES] = jnp.where(lane < HEAD_DIM, oa, ob).astype(o_ref.dtype)


def _window_attention(proj, vt, sink, batch, seq):
    n = proj.shape[0]
    tq = TQ_WIN
    nq = seq // tq
    heads = SWA_Q_HEADS // SWA_KV_HEADS
    q_cols = SWA_Q_HEADS // 2 * LANES
    kv_cols = SWA_KV_HEADS * LANES
    assert QA_BLK == 0 and KA_BLK * LANES % kv_cols == 0 and VT_WIN_BLK * LANES % kv_cols == 0
    kern = functools.partial(_win_kernel, tq=tq, seq=seq)
    return pl.pallas_call(
        kern,
        grid=(batch, nq),
        in_specs=[pl.BlockSpec(memory_space=pltpu.SMEM),
                  pl.BlockSpec((tq, q_cols), lambda b, i: (b * nq + i, 0)),
                  pl.BlockSpec((seq, kv_cols), lambda b, i: (b, KA_BLK * LANES // kv_cols)),
                  pl.BlockSpec((kv_cols, seq), lambda b, i: (VT_WIN_BLK * LANES // kv_cols, b))],
        out_specs=pl.BlockSpec((tq, q_cols), lambda b, i: (b * nq + i, 0)),
        out_shape=jax.ShapeDtypeStruct((n, q_cols), BF16),
        scratch_shapes=[pltpu.VMEM((SWA_KV_HEADS, LANES, heads * tq), BF16)],
        compiler_params=_params(("arbitrary", "arbitrary")),
        name="window_attn",
    )(sink, proj, proj, vt)


def _diff_kernel(lam_ref, g_ref, q_ref, k_ref, vt_ref, o_ref, qt_sc, s_a, s_b, cm_a, cm_b, m_sc, l_sc, acc_sc,
                 *, tq, tk, seq, lambda_init):
    nq = seq // tq
    nk = seq // tk
    first = _first_of_pair_mask()
    bufs = ((s_a, cm_a), (s_b, cm_b))

    def load_queries(i):
        row = pl.multiple_of(i * tq, tq)
        q = q_ref[pl.ds(row, tq), :].astype(F32)
        zero = jnp.zeros_like(q)
        qt_sc[:, :tq] = jnp.where(first, q, zero).T.astype(BF16)
        qt_sc[:, tq:] = jnp.where(first, zero, q).T.astype(BF16)

    def reset():
        m_sc[...] = jnp.full(m_sc.shape, -1e30, F32)
        l_sc[...] = jnp.zeros(l_sc.shape, F32)
        acc_sc[...] = jnp.zeros(acc_sc.shape, F32)

    def scores(t, s_ref, cm_ref):
        s = jnp.dot(k_ref[t * tk:(t + 1) * tk, :], qt_sc[...], preferred_element_type=F32)
        s_ref[...] = s
        cm_ref[...] = jnp.max(s, axis=0, keepdims=True)

    def consume(t, s_ref, cm_ref):
        m_prev = m_sc[...]
        m_new = jnp.maximum(m_prev, cm_ref[...])
        alpha = jnp.exp2(m_prev - m_new)
        p = jnp.exp2(s_ref[...] - m_new)
        l_sc[...] = alpha * l_sc[...] + jnp.sum(p, axis=0, keepdims=True)
        vb = vt_ref[:, t * tk:(t + 1) * tk]
        acc_sc[...] = alpha * acc_sc[...] + jnp.dot(vb, p.astype(BF16), preferred_element_type=F32)
        m_sc[...] = m_new

    lam1 = jnp.sum(lam_ref[0:1, :] * lam_ref[1:2, :], axis=-1, keepdims=True)
    lam2 = jnp.sum(lam_ref[2:3, :] * lam_ref[3:4, :], axis=-1, keepdims=True)
    lam = jnp.exp(lam1) - jnp.exp(lam2) + lambda_init
    gain = g_ref[...] * (1.0 - lambda_init)

    def finish(i):
        o = acc_sc[...] / l_sc[...]
        o = o[:, :tq] - lam * o[:, tq:]
        y = o * lax.rsqrt(jnp.mean(o * o, axis=0, keepdims=True) + EPS)
        row = pl.multiple_of(i * tq, tq)
        o_ref[pl.ds(row, tq), :] = (y.T * gain).astype(o_ref.dtype)

    load_queries(0)
    scores(0, *bufs[0])

    def tile(i, carry):
        reset()
        for t in range(nk):
            if t + 1 < nk:
                scores(t + 1, *bufs[(t + 1) % 2])
            else:
                load_queries(jnp.minimum(i + 1, nq - 1))
                scores(0, *bufs[0])
            consume(t, *bufs[t % 2])
        finish(i)
        return carry

    lax.fori_loop(0, nq, tile, 0)


def _diff_attention(proj, vt, lam_vecs, subln_g, batch, seq, lambda_init):
    n = proj.shape[0]
    tq, tk = TQ_DIFF, TK_DIFF
    assert seq % (2 * tk) == 0 and seq % tq == 0, "key blocks alternate between two score buffers"
    kern = functools.partial(_diff_kernel, tq=tq, tk=tk, seq=seq, lambda_init=lambda_init)
    return pl.pallas_call(
        kern,
        grid=(batch, DIFF_HEADS),
        in_specs=[pl.BlockSpec((4, HEAD_DIM), lambda b, h: (0, 0)),
                  pl.BlockSpec((1, LANES), lambda b, h: (0, 0)),
                  pl.BlockSpec((seq, LANES), lambda b, h: (b, QD_BLK + h)),
                  pl.BlockSpec((seq, LANES), lambda b, h: (b, KD_BLK + h)),
                  pl.BlockSpec((LANES, seq), lambda b, h: (h, b))],
        out_specs=pl.BlockSpec((seq, LANES), lambda b, h: (b, h)),
        out_shape=jax.ShapeDtypeStruct((n, DIFF_HEADS * LANES), BF16),
        scratch_shapes=[pltpu.VMEM((LANES, 2 * tq), BF16),
                        pltpu.VMEM((tk, 2 * tq), F32), pltpu.VMEM((tk, 2 * tq), F32),
                        pltpu.VMEM((1, 2 * tq), F32), pltpu.VMEM((1, 2 * tq), F32),
                        pltpu.VMEM((1, 2 * tq), F32), pltpu.VMEM((1, 2 * tq), F32),
                        pltpu.VMEM((LANES, 2 * tq), F32)],
        compiler_params=_params(("arbitrary", "arbitrary")),
        name="diff_attn",
    )(lam_vecs, subln_g, proj, proj, vt)


def _route_top2(logits):
    lane = lax.broadcasted_iota(jnp.int32, logits.shape, 1)
    lg = jnp.where(lane < N_EXPERTS, logits, -jnp.inf)
    v1 = jnp.max(lg, axis=-1, keepdims=True)
    i1 = jnp.min(jnp.where(lg == v1, lane, LANES), axis=-1, keepdims=True)
    lg2 = jnp.where(lane == i1, -jnp.inf, lg)
    v2 = jnp.max(lg2, axis=-1, keepdims=True)
    i2 = jnp.min(jnp.where(lg2 == v2, lane, LANES), axis=-1, keepdims=True)
    e = jnp.exp(v2 - v1)
    w1 = 1.0 / (1.0 + e)
    w2 = e / (1.0 + e)
    idx = jnp.where(lane == 0, i1, jnp.where(lane == 1, i2, 0))
    wgt = jnp.where(lane == 0, w1, jnp.where(lane == 1, w2, 0.0))
    return idx, wgt


def _outproj_kernel(ya_ref, yd_ref, w_ref, x_ref, mod_ref, g_ref, *rest, moe):
    if moe:
        rwh_ref, rwl_ref, x1_ref, h_ref, idx_ref, wgt_ref = rest
    else:
        x1_ref, h_ref = rest
    half = ya_ref.shape[1]
    y = (jnp.dot(ya_ref[...], w_ref[:half, :], preferred_element_type=F32)
         + jnp.dot(yd_ref[...], w_ref[half:, :], preferred_element_type=F32))
    x1 = x_ref[...] + mod_ref[0, 2:3, :] * y
    x1_ref[...] = x1
    h = _rms_mod(x1, g_ref[...], mod_ref[0, 4:5, :], mod_ref[0, 3:4, :])
    if moe:
        for c in range(SUBROWS):
            h_ref[pl.ds(c, h.shape[0], stride=SUBROWS), :] = h[:, c * LANES:(c + 1) * LANES]
    else:
        h_ref[...] = h.astype(h_ref.dtype)
    if moe:
        h_hi = h.astype(BF16)
        h_lo = (h - h_hi.astype(F32)).astype(BF16)
        logits = (jnp.dot(h_hi, rwh_ref[...], preferred_element_type=F32)
                  + jnp.dot(h_hi, rwl_ref[...], preferred_element_type=F32)
                  + jnp.dot(h_lo, rwh_ref[...], preferred_element_type=F32))
        idx, wgt = _route_top2(logits)
        idx_ref[...] = idx
        wgt_ref[...] = wgt


def _outproj(ya, yd, w, layer, x2, mod, g, seq, router_w=None):
    n, d = x2.shape
    tm = TM_PROJ
    per_b = seq // tm
    moe = router_w is not None
    row = lambda i: (i, 0)
    in_specs = [pl.BlockSpec((tm, ya.shape[1]), row),
                pl.BlockSpec((tm, yd.shape[1]), row),
                pl.BlockSpec((None,) + w.shape[1:], lambda i: (layer, 0, 0)),
                pl.BlockSpec((tm, d), row),
                pl.BlockSpec((1, N_MOD, d), lambda i: (i // per_b, 0, 0)),
                pl.BlockSpec((1, d), lambda i: (0, 0))]
    out_specs = [pl.BlockSpec((tm, d), row), pl.BlockSpec(_to_subrows((tm, d)) if moe else (tm, d), row)]
    out_shape = [jax.ShapeDtypeStruct((n, d), F32),
                 jax.ShapeDtypeStruct(_to_subrows((n, d)), F32) if moe else jax.ShapeDtypeStruct((n, d), BF16)]
    args = [ya, yd, w, x2, mod, g]
    if moe:
        in_specs += [pl.BlockSpec((d, LANES), lambda i: (0, 0))] * 2
        out_specs += [pl.BlockSpec((tm, LANES), row)] * 2
        out_shape += [jax.ShapeDtypeStruct((n, LANES), jnp.int32), jax.ShapeDtypeStruct((n, LANES), F32)]
        args += list(router_w)
    return pl.pallas_call(
        functools.partial(_outproj_kernel, moe=moe),
        grid=(n // tm,),
        in_specs=in_specs, out_specs=out_specs, out_shape=out_shape,
        compiler_params=_params(("arbitrary",)),
        name="outproj_moe" if moe else "outproj",
    )(*args)


def _swiglu_step(h, wg_ref, wu_ref, wd_ref):
    g = jnp.dot(h, wg_ref[...].astype(BF16), preferred_element_type=F32)
    u = jnp.dot(h, wu_ref[...].astype(BF16), preferred_element_type=F32)
    return jnp.dot((_silu(g) * u).astype(BF16), wd_ref[...].astype(BF16), preferred_element_type=F32)


def _ffn_kernel(h_ref, wg_ref, wu_ref, wd_ref, x_ref, mod_ref, o_ref, acc_ref):
    j = pl.program_id(1)

    @pl.when(j == 0)
    def _():
        acc_ref[...] = jnp.zeros(acc_ref.shape, F32)

    acc_ref[...] += _swiglu_step(h_ref[...], wg_ref, wu_ref, wd_ref)

    @pl.when(j == pl.num_programs(1) - 1)
    def _():
        o_ref[...] = x_ref[...] + mod_ref[0, 5:6, :] * acc_ref[...]


def _dense_ffn(h, wg, wu, wd, li, x1, mod, seq):
    n, d = x1.shape
    f = wg.shape[2]
    tm, tf = TM_FFN, TF_FFN
    per_b = seq // tm
    return pl.pallas_call(
        _ffn_kernel,
        grid=(n // tm, f // tf),
        in_specs=[pl.BlockSpec((tm, d), lambda i, j: (i, 0)),
                  pl.BlockSpec((None, d, tf), lambda i, j: (li, 0, j)),
                  pl.BlockSpec((None, d, tf), lambda i, j: (li, 0, j)),
                  pl.BlockSpec((None, tf, d), lambda i, j: (li, j, 0)),
                  pl.BlockSpec((tm, d), lambda i, j: (i, 0)),
                  pl.BlockSpec((1, N_MOD, d), lambda i, j: (i // per_b, 0, 0))],
        out_specs=pl.BlockSpec((tm, d), lambda i, j: (i, 0)),
        out_shape=jax.ShapeDtypeStruct((n, d), F32),
        scratch_shapes=[pltpu.VMEM((tm, d), F32)],
        compiler_params=_params(("arbitrary", "arbitrary")),
        name="dense_ffn",
    )(h, wg, wu, wd, x1, mod)


def _moe_plan(idx, tm):
    n = idx.shape[0]
    flat_e = idx.reshape(-1)
    onehot = (flat_e[:, None] == jnp.arange(N_EXPERTS, dtype=jnp.int32)[None, :]).astype(jnp.int32)
    csum = jnp.cumsum(onehot, axis=0)
    rank = jnp.sum((csum - onehot) * onehot, axis=1)
    counts = csum[-1]
    padded = (counts + tm - 1) // tm * tm
    pad_ends = jnp.cumsum(padded)
    pad_starts = pad_ends - padded
    dest = (pad_starts[flat_e] + rank).astype(jnp.int32)
    rows = 2 * n + N_EXPERTS * tm
    n_tiles = (pad_ends[-1] // tm).astype(jnp.int32).reshape(1)
    tile_start = jnp.arange(rows // tm, dtype=jnp.int32) * tm
    tile_e = jnp.minimum(jnp.sum((pad_ends[None, :] <= tile_start[:, None]).astype(jnp.int32), axis=1),
                         N_EXPERTS - 1).astype(jnp.int32)
    tile_rows = jnp.clip((pad_starts + counts)[tile_e] - tile_start, 0, tm).astype(jnp.int32)
    return dest, pad_ends.astype(jnp.int32), tile_e, tile_rows, n_tiles, rows


def _to_subrows(ref_or_shape):
    rows, d = ref_or_shape
    assert d == SUBROWS * LANES
    return (rows * SUBROWS, LANES)


def _row_copy(src_hbm, src_row, dst_ref, dst_row, sem):
    s = pl.multiple_of(src_row * SUBROWS, SUBROWS)
    t = pl.multiple_of(dst_row * SUBROWS, SUBROWS)
    return pltpu.make_async_copy(src_hbm.at[pl.ds(s, SUBROWS)], dst_ref.at[pl.ds(t, SUBROWS)], sem)


def _row_gather(idx_fns, src_hbm, dst_refs, sem, rows):
    def issue(r, carry):
        for k, (idx_of, dst_ref) in enumerate(zip(idx_fns, dst_refs)):
            _row_copy(src_hbm, idx_of(r), dst_ref, r, sem).start(priority=k % 2)
        return carry

    lax.fori_loop(0, rows, issue, 0, unroll=4)


def _row_gather_wait(src_hbm, dst_ref, sem, rows):
    pltpu.make_async_copy(src_hbm.at[pl.ds(0, rows * SUBROWS)], dst_ref, sem).wait()


def _moe_scatter_kernel(dest_ref, pe_ref, nt_ref, h_ref, xs_hbm, zeros, sem, *, tb, tm, n_tiles_max):
    i = pl.program_id(0)
    zero_sem, row_sem = sem.at[0], sem.at[1]
    tile_rows = tm * SUBROWS

    def zero_tile(t):
        start = pl.multiple_of(t * tile_rows, tile_rows)
        return pltpu.make_async_copy(zeros, xs_hbm.at[pl.ds(start, tile_rows)], zero_sem)

    @pl.when(i == 0)
    def _():
        zeros[...] = jnp.zeros(zeros.shape, zeros.dtype)
        for e in range(N_EXPERTS):
            last_tile = jnp.maximum(pe_ref[e] // tm - 1, 0)
            zero_tile(last_tile).start()
            zero_tile(last_tile).wait()
        for t in range(n_tiles_max):
            @pl.when(t >= nt_ref[0])
            def _():
                zero_tile(t).start()
                zero_tile(t).wait()

    base = i * tb

    def issue(r, carry):
        src = h_ref.at[pl.ds(pl.multiple_of(r * SUBROWS, SUBROWS), SUBROWS)]
        for k in range(2):
            d = dest_ref[2 * (base + r) + k]
            pltpu.make_async_copy(src, xs_hbm.at[pl.ds(pl.multiple_of(d * SUBROWS, SUBROWS), SUBROWS)],
                                  row_sem).start(priority=k)
        return carry

    lax.fori_loop(0, tb, issue, 0, unroll=4)
    for k in range(2):
        pltpu.make_async_copy(h_ref, xs_hbm.at[pl.ds(0, tb * SUBROWS)], row_sem).wait()


def _moe_scatter(h_sub, dest, pad_ends, n_tiles, rows_total, tm):
    n_sub = h_sub.shape[0]
    tb = TM_COMB
    d = SUBROWS * LANES
    n_tiles_max = rows_total // tm
    return pl.pallas_call(
        functools.partial(_moe_scatter_kernel, tb=tb, tm=tm, n_tiles_max=n_tiles_max),
        grid_spec=pltpu.PrefetchScalarGridSpec(
            num_scalar_prefetch=3,
            grid=(n_sub // (tb * SUBROWS),),
            in_specs=[pl.BlockSpec(_to_subrows((tb, d)), lambda i, de, pe, nt: (i, 0))],
            out_specs=pl.BlockSpec(memory_space=pl.ANY),
            scratch_shapes=[pltpu.VMEM(_to_subrows((tm, d)), F32), pltpu.SemaphoreType.DMA((2,))]),
        out_shape=jax.ShapeDtypeStruct(_to_subrows((rows_total, d)), F32),
        compiler_params=_params(("arbitrary",)),
        name="moe_scatter",
    )(dest, pad_ends, n_tiles, h_sub)


def _moe_ffn_kernel(te_ref, tr_ref, nt_ref, xs_ref, wg_ref, wu_ref, wd_ref, o_ref, xb16, acc_ref, *, nf):
    i = pl.program_id(0)
    j = pl.program_id(1)
    tm = xb16.shape[0]
    half = tm // 2

    @pl.when(i < nt_ref[0])
    def _():
        @pl.when(j == 0)
        def _():
            for c in range(SUBROWS):
                xb16[:, c * LANES:(c + 1) * LANES] = xs_ref[pl.ds(c, tm, stride=SUBROWS), :].astype(BF16)
            acc_ref[...] = jnp.zeros(acc_ref.shape, F32)

        @pl.when(tr_ref[i] > half)
        def _():
            acc_ref[...] += _swiglu_step(xb16[...], wg_ref.at[0], wu_ref.at[0], wd_ref.at[0])

        @pl.when(tr_ref[i] <= half)
        def _():
            acc_ref[:half, :] += _swiglu_step(xb16[:half, :], wg_ref.at[0], wu_ref.at[0], wd_ref.at[0])

        @pl.when(j == nf - 1)
        def _():
            for c in range(SUBROWS):
                o_ref[pl.ds(c, tm, stride=SUBROWS), :] = acc_ref[:, c * LANES:(c + 1) * LANES]

    @pl.when(jnp.logical_and(i >= nt_ref[0], j == nf - 1))
    def _():
        o_ref[...] = jnp.zeros(o_ref.shape, o_ref.dtype)


def _moe_ffn(xs, wg, wu, wd, li, tile_e, tile_rows, n_tiles, tm):
    d = wg.shape[2]
    f = wg.shape[3]
    rows_total = xs.shape[0] // SUBROWS
    tf = TF_FFN
    nf = f // tf

    def row_map(i, j, te, tr, nt):
        return (jnp.minimum(i, nt[0] - 1), 0)

    def _fj(i, j, nt):
        return jnp.where(i < nt[0], j, nf - 1)

    def up_map(i, j, te, tr, nt):
        return (li, te[jnp.minimum(i, nt[0] - 1)], 0, _fj(i, j, nt))

    def down_map(i, j, te, tr, nt):
        return (li, te[jnp.minimum(i, nt[0] - 1)], _fj(i, j, nt), 0)

    return pl.pallas_call(
        functools.partial(_moe_ffn_kernel, nf=nf),
        grid_spec=pltpu.PrefetchScalarGridSpec(
            num_scalar_prefetch=3,
            grid=(rows_total // tm, nf),
            in_specs=[pl.BlockSpec(_to_subrows((tm, d)), row_map),
                      pl.BlockSpec((None, 1, d, tf), up_map),
                      pl.BlockSpec((None, 1, d, tf), up_map),
                      pl.BlockSpec((None, 1, tf, d), down_map)],
            out_specs=pl.BlockSpec(_to_subrows((tm, d)), lambda i, j, te, tr, nt: (i, 0)),
            scratch_shapes=[pltpu.VMEM((tm, d), BF16), pltpu.VMEM((tm, d), F32)]),
        out_shape=jax.ShapeDtypeStruct(_to_subrows((rows_total, d)), F32),
        compiler_params=_params(("arbitrary", "arbitrary")),
        name="moe_ffn",
    )(tile_e, tile_rows, n_tiles, xs, wg, wu, wd)


def _combine_kernel(pos_ref, ys_hbm, wgt_ref, x_ref, mod_ref, fg_ref, o_ref, buf, sem, *, tm, final):
    i = pl.program_id(0)
    slot = i % 2

    def gather(step, to_slot):
        _row_gather([lambda r: pos_ref[2 * (step * tm + r)], lambda r: pos_ref[2 * (step * tm + r) + 1]],
                    ys_hbm, [buf.at[to_slot, 0], buf.at[to_slot, 1]], sem.at[to_slot], tm)

    @pl.when(i == 0)
    def _():
        gather(0, 0)

    @pl.when(i + 1 < pl.num_programs(0))
    def _():
        gather(i + 1, 1 - slot)

    for k in range(2):
        _row_gather_wait(ys_hbm, buf.at[slot, k], sem.at[slot], tm)

    wgt = wgt_ref[...]
    w0, w1 = wgt[:, 0:1], wgt[:, 1:2]
    y = jnp.concatenate([w0 * buf[slot, 0, pl.ds(c, tm, stride=SUBROWS), :]
                         + w1 * buf[slot, 1, pl.ds(c, tm, stride=SUBROWS), :] for c in range(SUBROWS)], axis=1)
    out = x_ref[...] + mod_ref[0, 5:6, :] * y
    if final:
        out = out * lax.rsqrt(jnp.mean(out * out, axis=-1, keepdims=True) + EPS) * fg_ref[...]
    o_ref[...] = out


def _moe_combine(ys, pos, wgt, x1, mod, final_g, seq, final):
    n, d = x1.shape
    tm = TM_COMB
    per_b = seq // tm
    return pl.pallas_call(
        functools.partial(_combine_kernel, tm=tm, final=final),
        grid_spec=pltpu.PrefetchScalarGridSpec(
            num_scalar_prefetch=1,
            grid=(n // tm,),
            in_specs=[pl.BlockSpec(memory_space=pl.ANY),
                      pl.BlockSpec((tm, LANES), lambda i, p: (i, 0)),
                      pl.BlockSpec((tm, d), lambda i, p: (i, 0)),
                      pl.BlockSpec((1, N_MOD, d), lambda i, p: (i // per_b, 0, 0)),
                      pl.BlockSpec((1, d), lambda i, p: (0, 0))],
            out_specs=pl.BlockSpec((tm, d), lambda i, p: (i, 0)),
            scratch_shapes=[pltpu.VMEM((2, 2) + _to_subrows((tm, d)), F32),
                            pltpu.SemaphoreType.DMA((2,))]),
        out_shape=jax.ShapeDtypeStruct((n, d), F32),
        compiler_params=_params(("arbitrary",)),
        name="moe_combine_final" if final else "moe_combine",
    )(pos.reshape(-1), ys, wgt, x1, mod, final_g)


def _final_norm_kernel(x_ref, g_ref, o_ref):
    x = x_ref[...]
    o_ref[...] = x * lax.rsqrt(jnp.mean(x * x, axis=-1, keepdims=True) + EPS) * g_ref[...]


def _final_norm(x2, g):
    n, d = x2.shape
    tm = TM_PROJ
    return pl.pallas_call(
        _final_norm_kernel,
        grid=(n // tm,),
        in_specs=[pl.BlockSpec((tm, d), lambda i: (i, 0)), pl.BlockSpec((1, d), lambda i: (0, 0))],
        out_specs=pl.BlockSpec((tm, d), lambda i: (i, 0)),
        out_shape=jax.ShapeDtypeStruct((n, d), F32),
        compiler_params=_params(("arbitrary",)),
        name="final_norm",
    )(x2, g)


def kernel(x, c, positions, ada_w, ada_b, norm_mix_g, norm_ffn_g, w_in, w_out, sink, lam_q1, lam_k1, lam_q2, lam_k2, subln_g, ffn_w_gate, ffn_w_up, ffn_w_down, router_w, moe_w_gate, moe_w_up, moe_w_down, final_g):
    batch, seq, d = x.shape
    depth = w_in.shape[0]
    n = batch * seq

    c_pad = jnp.zeros((8, d), F32).at[:batch].set(c)
    mod_all = _adaln(c_pad, ada_w, ada_b)
    cos_t, sin_t = _rope_tables(positions)
    w_in_p = _permute_in_proj(w_in).astype(BF16)
    w_vdt = _value_rows(w_in).astype(BF16)
    w_out_b = w_out.astype(BF16)
    ffn_w = [w.astype(BF16) for w in (ffn_w_gate, ffn_w_up, ffn_w_down)]
    moe_w = [moe_w_gate, moe_w_up, moe_w_down]
    final_g2 = final_g.reshape(1, d)

    x2 = x.reshape(n, d)
    for layer in range(depth):
        lambda_init = 0.8 - 0.6 * math.exp(-0.3 * layer)
        mod = mod_all[layer, :batch].reshape(batch, N_MOD, d)
        proj, vt = _inproj(x2, mod, norm_mix_g[layer].reshape(1, d), w_in_p, w_vdt, layer, cos_t, sin_t, seq)
        ya = _window_attention(proj, vt, sink[layer], batch, seq)
        lam_vecs = jnp.stack([lam_q1[layer], lam_k1[layer], lam_q2[layer], lam_k2[layer]])
        yd = _diff_attention(proj, vt, lam_vecs, subln_g[layer].reshape(1, LANES), batch, seq, lambda_init)
        i = layer // 2
        g_ffn = norm_ffn_g[layer].reshape(1, d)
        if layer % 2 == 0:
            x1, h = _outproj(ya, yd, w_out_b, layer, x2, mod, g_ffn, seq)
            x2 = _dense_ffn(h, *ffn_w, i, x1, mod, seq)
        else:
            rw = jnp.zeros((d, LANES), F32).at[:, :N_EXPERTS].set(router_w[i])
            rw_hi = rw.astype(BF16)
            rw_lo = (rw - rw_hi.astype(F32)).astype(BF16)
            x1, h, idx, wgt = _outproj(ya, yd, w_out_b, layer, x2, mod, g_ffn, seq, router_w=(rw_hi, rw_lo))
            dest, pad_ends, tile_e, tile_rows, n_tiles, rows_total = _moe_plan(idx[:, :2], TM_MOE)
            pos = dest.reshape(n, 2)
            xs = _moe_scatter(h, dest, pad_ends, n_tiles, rows_total, TM_MOE)
            ys = _moe_ffn(xs, *moe_w, i, tile_e, tile_rows, n_tiles, TM_MOE)
            final = layer == depth - 1
            x2 = _moe_combine(ys, pos, wgt, x1, mod, final_g2, seq, final)
    if depth % 2 == 1:
        x2 = _final_norm(x2, final_g2)
    return x2.reshape(batch, seq, d)
```

```python
import functools
import math

import jax
import jax.numpy as jnp
from jax import lax
from jax.experimental import pallas as pl
from jax.experimental.pallas import tpu as pltpu

F32 = jnp.float32
BF16 = jnp.bfloat16

HEAD_DIM = 64
HALF_DIM = HEAD_DIM // 2
ROPE_THETA = 10000.0
EPS = 1e-6
SWA_Q_HEADS = 8
SWA_KV_HEADS = 2
WINDOW = 128
DIFF_HEADS = 4
N_EXPERTS = 8
N_MOD = 6
LANES = 128
SUBROWS = 8
VMEM_LIMIT = 56 * 1024 * 1024

QA_BLK, KA_BLK, QD_BLK, KD_BLK = 0, 4, 6, 10
ROPE_BLKS = 14
PROJ_COLS = ROPE_BLKS * LANES
VA_COL0 = 640
VD_COL0 = 1792
VT_WIN_BLK = 4
Q_SCALE = HEAD_DIM ** -0.5
LOG2E = math.log2(math.e)

TM_PROJ = 512
TQ_WIN = 256
TQ_DIFF = 512
TK_DIFF = 1024
TM_FFN = 1024
TF_FFN = 512
TM_MOE = 1024
TM_COMB = 512


def _permute_in_proj(w_in):
    depth, d, _ = w_in.shape
    qa, ka, qd, kd = 0, 512, 768, 1280

    def pairs(lo, n_pairs):
        t = w_in[:, :, lo:lo + n_pairs * LANES].reshape(depth, d, n_pairs, 2, 2, HALF_DIM)
        return t.transpose(0, 1, 2, 4, 3, 5).reshape(depth, d, n_pairs * LANES)

    def dup_halves(lo):
        t = w_in[:, :, lo:lo + SWA_KV_HEADS * HEAD_DIM].reshape(depth, d, SWA_KV_HEADS, 2, 1, HALF_DIM)
        return jnp.broadcast_to(t, (depth, d, SWA_KV_HEADS, 2, 2, HALF_DIM)).reshape(depth, d, SWA_KV_HEADS * LANES)

    return jnp.concatenate([pairs(qa, 4), dup_halves(ka), pairs(qd, DIFF_HEADS), pairs(kd, DIFF_HEADS)], axis=-1)


def _value_rows(w_in):
    depth, d, _ = w_in.shape
    vd = jnp.swapaxes(w_in[:, :, VD_COL0:], 1, 2)
    va = jnp.swapaxes(w_in[:, :, VA_COL0:VA_COL0 + SWA_KV_HEADS * HEAD_DIM], 1, 2)
    va = jnp.broadcast_to(va.reshape(depth, SWA_KV_HEADS, 1, HEAD_DIM, d), (depth, SWA_KV_HEADS, 2, HEAD_DIM, d))
    return jnp.concatenate([vd, va.reshape(depth, SWA_KV_HEADS * LANES, d)], axis=1)


def _first_of_pair_mask():
    lane = lax.broadcasted_iota(jnp.int32, (1, LANES), 1)
    return (lane // HALF_DIM) % 2 == 0


def _silu(g):
    return g / (1.0 + jnp.exp(-g))


def _params(sem):
    return pltpu.CompilerParams(dimension_semantics=sem, vmem_limit_bytes=VMEM_LIMIT)


def _adaln_kernel(c_ref, w_ref, b_ref, o_ref):
    a = _silu(c_ref[...])
    o_ref[0] = jnp.dot(a, w_ref[0], preferred_element_type=F32,
                       precision=lax.Precision.HIGHEST) + b_ref[0]


def _adaln(c_pad, ada_w, ada_b):
    depth, d, cols = ada_w.shape
    tn = 1536
    return pl.pallas_call(
        _adaln_kernel,
        grid=(depth, cols // tn),
        in_specs=[pl.BlockSpec((8, d), lambda l, j: (0, 0)),
                  pl.BlockSpec((1, d, tn), lambda l, j: (l, 0, j)),
                  pl.BlockSpec((1, 1, tn), lambda l, j: (l, 0, j))],
        out_specs=pl.BlockSpec((1, 8, tn), lambda l, j: (l, 0, j)),
        out_shape=jax.ShapeDtypeStruct((depth, 8, cols), F32),
        compiler_params=_params(("arbitrary", "arbitrary")),
        name="adaln",
    )(c_pad, ada_w, ada_b.reshape(depth, 1, cols))


def _rope_table_kernel(pos_ref, cos_ref, sin_ref):
    lane = lax.broadcasted_iota(jnp.int32, (1, LANES), 1)
    freq = (lane % HALF_DIM).astype(F32)
    inv = 1.0 / (ROPE_THETA ** (2.0 * freq / HEAD_DIM))
    ang = pos_ref[...].astype(F32) * inv
    sign = jnp.where(lane < HEAD_DIM, -1.0, 1.0)
    cos_ref[...] = jnp.cos(ang)
    sin_ref[...] = jnp.sin(ang) * sign


def _rope_tables(positions):
    n = positions.size
    tm = 1024
    pos = positions.reshape(n, 1)
    return pl.pallas_call(
        _rope_table_kernel,
        grid=(n // tm,),
        in_specs=[pl.BlockSpec((tm, 1), lambda i: (i, 0))],
        out_specs=[pl.BlockSpec((tm, LANES), lambda i: (i, 0))] * 2,
        out_shape=[jax.ShapeDtypeStruct((n, LANES), F32)] * 2,
        compiler_params=_params(("arbitrary",)),
        name="rope_tables",
    )(pos)


def _rms_mod(x, g, scale, shift):
    y = x * lax.rsqrt(jnp.mean(x * x, axis=-1, keepdims=True) + EPS) * g
    return y * (1.0 + scale) + shift


def _inproj_kernel(x_ref, mod_ref, g_ref, w_ref, wvt_ref, cos_ref, sin_ref, o_ref, vt_ref):
    h = _rms_mod(x_ref[...], g_ref[...], mod_ref[0, 1:2, :], mod_ref[0, 0:1, :]).astype(BF16)
    proj = jnp.dot(h, w_ref[...], preferred_element_type=F32)
    vt_ref[...] = lax.dot_general(wvt_ref[...], h, (((1,), (1,)), ((), ())),
                                  preferred_element_type=F32).astype(BF16)
    cos = cos_ref[...]
    sin = sin_ref[...]
    for blk in range(ROPE_BLKS):
        t = proj[:, blk * LANES:(blk + 1) * LANES]
        r = t * cos + pltpu.roll(t, HEAD_DIM, axis=1) * sin
        if blk < KA_BLK or QD_BLK <= blk < KD_BLK:
            r = r * (Q_SCALE * LOG2E)
        o_ref[:, blk * LANES:(blk + 1) * LANES] = r.astype(BF16)


def _inproj(x2, mod, g, w, wvt, layer, cos_t, sin_t, seq):
    n, d = x2.shape
    tm = TM_PROJ
    per_b = seq // tm
    vt_rows = wvt.shape[1]
    return pl.pallas_call(
        _inproj_kernel,
        grid=(n // tm,),
        in_specs=[pl.BlockSpec((tm, d), lambda i: (i, 0)),
                  pl.BlockSpec((1, N_MOD, d), lambda i: (i // per_b, 0, 0)),
                  pl.BlockSpec((1, d), lambda i: (0, 0)),
                  pl.BlockSpec((None, d, PROJ_COLS), lambda i: (layer, 0, 0)),
                  pl.BlockSpec((None, vt_rows, d), lambda i: (layer, 0, 0)),
                  pl.BlockSpec((tm, LANES), lambda i: (i, 0)),
                  pl.BlockSpec((tm, LANES), lambda i: (i, 0))],
        out_specs=[pl.BlockSpec((tm, PROJ_COLS), lambda i: (i, 0)),
                   pl.BlockSpec((vt_rows, tm), lambda i: (0, i))],
        out_shape=[jax.ShapeDtypeStruct((n, PROJ_COLS), BF16),
                   jax.ShapeDtypeStruct((vt_rows, n), BF16)],
        compiler_params=_params(("arbitrary",)),
        name="inproj",
    )(x2, mod, g, w, wvt, cos_t, sin_t)


def _win_kernel(sink_ref, q_ref, k_ref, vt_ref, o_ref, qt_sc, *, tq, seq):
    i = pl.program_id(1)
    heads = SWA_Q_HEADS // SWA_KV_HEADS
    width = tq + 2 * WINDOW
    start = pl.multiple_of(jnp.clip(i * tq - WINDOW, 0, seq - width), LANES)
    first = _first_of_pair_mask()
    kpos = start + lax.broadcasted_iota(jnp.int32, (width, tq), 0)
    qpos = i * tq + lax.broadcasted_iota(jnp.int32, (width, tq), 1)
    valid = jnp.abs(qpos - kpos) <= WINDOW
    col = lax.broadcasted_iota(jnp.int32, (1, heads * tq), 1)
    lane = lax.broadcasted_iota(jnp.int32, (1, LANES), 1)
    for g in range(SWA_KV_HEADS):
        for blk in range(heads // 2):
            qb = (heads // 2) * g + blk
            q = q_ref[:, qb * LANES:(qb + 1) * LANES].astype(F32)
            zero = jnp.zeros_like(q)
            qt_sc[g, :, (2 * blk) * tq:(2 * blk + 1) * tq] = jnp.where(first, q, zero).T.astype(BF16)
            qt_sc[g, :, (2 * blk + 1) * tq:(2 * blk + 2) * tq] = jnp.where(first, zero, q).T.astype(BF16)

        s = jnp.dot(k_ref[pl.ds(start, width), g * LANES:(g + 1) * LANES], qt_sc[g],
                    preferred_element_type=F32)
        s = jnp.concatenate([jnp.where(valid, s[:, c * tq:(c + 1) * tq], -1e30) for c in range(heads)], axis=1)
        sk = jnp.zeros((1, heads * tq), F32)
        for c in range(heads):
            sk = jnp.where(col // tq == c, sink_ref[heads * g + c] * LOG2E, sk)
        m = jnp.maximum(jnp.max(s, axis=0, keepdims=True), sk)
        p = jnp.exp2(s - m)
        den = jnp.sum(p, axis=0, keepdims=True) + jnp.exp2(sk - m)
        o = jnp.dot(vt_ref[g * LANES:(g + 1) * LANES, pl.ds(start, width)], p.astype(BF16),
                    preferred_element_type=F32) / den
        for blk in range(heads // 2):
            qb = (heads // 2) * g + blk
            oa = o[:, (2 * blk) * tq:(2 * blk + 1) * tq].T
            ob = o[:, (2 * blk + 1) * tq:(2 * blk + 2) * tq].T
            o_ref[:, qb * LANES:(qb + 1) * LANES] = jnp.where(lane < HEAD_DIM, oa, ob).astype(o_ref.dtype)


def _window_attention(proj, vt, sink, batch, seq):
    n = proj.shape[0]
    tq = TQ_WIN
    nq = seq // tq
    heads = SWA_Q_HEADS // SWA_KV_HEADS
    q_cols = SWA_Q_HEADS // 2 * LANES
    kv_cols = SWA_KV_HEADS * LANES
    assert QA_BLK == 0 and KA_BLK * LANES % kv_cols == 0 and VT_WIN_BLK * LANES % kv_cols == 0
    kern = functools.partial(_win_kernel, tq=tq, seq=seq)
    return pl.pallas_call(
        kern,
        grid=(batch, nq),
        in_specs=[pl.BlockSpec(memory_space=pltpu.SMEM),
                  pl.BlockSpec((tq, q_cols), lambda b, i: (b * nq + i, 0)),
                  pl.BlockSpec((seq, kv_cols), lambda b, i: (b, KA_BLK * LANES // kv_cols)),
                  pl.BlockSpec((kv_cols, seq), lambda b, i: (VT_WIN_BLK * LANES // kv_cols, b))],
        out_specs=pl.BlockSpec((tq, q_cols), lambda b, i: (b * nq + i, 0)),
        out_shape=jax.ShapeDtypeStruct((n, q_cols), BF16),
        scratch_shapes=[pltpu.VMEM((SWA_KV_HEADS, LANES, heads * tq), BF16)],
        compiler_params=_params(("arbitrary", "arbitrary")),
        name="window_attn",
    )(sink, proj, proj, vt)


def _diff_kernel(lam_ref, g_ref, q_ref, k_ref, vt_ref, o_ref, qt_sc, s_a, s_b, cm_a, cm_b, m_sc, l_sc, acc_sc,
                 *, tq, tk, seq, lambda_init):
    nq = seq // tq
    nk = seq // tk
    first = _first_of_pair_mask()
    bufs = ((s_a, cm_a), (s_b, cm_b))

    def load_queries(i):
        row = pl.multiple_of(i * tq, tq)
        q = q_ref[pl.ds(row, tq), :].astype(F32)
        zero = jnp.zeros_like(q)
        qt_sc[:, :tq] = jnp.where(first, q, zero).T.astype(BF16)
        qt_sc[:, tq:] = jnp.where(first, zero, q).T.astype(BF16)

    def reset():
        m_sc[...] = jnp.full(m_sc.shape, -1e30, F32)
        l_sc[...] = jnp.zeros(l_sc.shape, F32)
        acc_sc[...] = jnp.zeros(acc_sc.shape, F32)

    def scores(t, s_ref, cm_ref):
        s = jnp.dot(k_ref[t * tk:(t + 1) * tk, :], qt_sc[...], preferred_element_type=F32)
        s_ref[...] = s
        cm_ref[...] = jnp.max(s, axis=0, keepdims=True)

    def consume(t, s_ref, cm_ref):
        m_prev = m_sc[...]
        m_new = jnp.maximum(m_prev, cm_ref[...])
        alpha = jnp.exp2(m_prev - m_new)
        p = jnp.exp2(s_ref[...] - m_new)
        l_sc[...] = alpha * l_sc[...] + jnp.sum(p, axis=0, keepdims=True)
        vb = vt_ref[:, t * tk:(t + 1) * tk]
        acc_sc[...] = alpha * acc_sc[...] + jnp.dot(vb, p.astype(BF16), preferred_element_type=F32)
        m_sc[...] = m_new

    lam1 = jnp.sum(lam_ref[0:1, :] * lam_ref[1:2, :], axis=-1, keepdims=True)
    lam2 = jnp.sum(lam_ref[2:3, :] * lam_ref[3:4, :], axis=-1, keepdims=True)
    lam = jnp.exp(lam1) - jnp.exp(lam2) + lambda_init
    gain = g_ref[...] * (1.0 - lambda_init)

    def finish(i):
        o = acc_sc[...] / l_sc[...]
        o = o[:, :tq] - lam * o[:, tq:]
        y = o * lax.rsqrt(jnp.mean(o * o, axis=0, keepdims=True) + EPS)
        row = pl.multiple_of(i * tq, tq)
        o_ref[pl.ds(row, tq), :] = (y.T * gain).astype(o_ref.dtype)

    load_queries(0)
    scores(0, *bufs[0])

    def tile(i, carry):
        reset()
        for t in range(nk):
            if t + 1 < nk:
                scores(t + 1, *bufs[(t + 1) % 2])
            else:
                load_queries(jnp.minimum(i + 1, nq - 1))
                scores(0, *bufs[0])
            consume(t, *bufs[t % 2])
        finish(i)
        return carry

    lax.fori_loop(0, nq, tile, 0)


def _diff_attention(proj, vt, lam_vecs, subln_g, batch, seq, lambda_init):
    n = proj.shape[0]
    tq, tk = TQ_DIFF, TK_DIFF
    assert seq % (2 * tk) == 0 and seq % tq == 0, "key blocks alternate between two score buffers"
    kern = functools.partial(_diff_kernel, tq=tq, tk=tk, seq=seq, lambda_init=lambda_init)
    return pl.pallas_call(
        kern,
        grid=(batch, DIFF_HEADS),
        in_specs=[pl.BlockSpec((4, HEAD_DIM), lambda b, h: (0, 0)),
                  pl.BlockSpec((1, LANES), lambda b, h: (0, 0)),
                  pl.BlockSpec((seq, LANES), lambda b, h: (b, QD_BLK + h)),
                  pl.BlockSpec((seq, LANES), lambda b, h: (b, KD_BLK + h)),
                  pl.BlockSpec((LANES, seq), lambda b, h: (h, b))],
        out_specs=pl.BlockSpec((seq, LANES), lambda b, h: (b, h)),
        out_shape=jax.ShapeDtypeStruct((n, DIFF_HEADS * LANES), BF16),
        scratch_shapes=[pltpu.VMEM((LANES, 2 * tq), BF16),
                        pltpu.VMEM((tk, 2 * tq), F32), pltpu.VMEM((tk, 2 * tq), F32),
                        pltpu.VMEM((1, 2 * tq), F32), pltpu.VMEM((1, 2 * tq), F32),
                        pltpu.VMEM((1, 2 * tq), F32), pltpu.VMEM((1, 2 * tq), F32),
                        pltpu.VMEM((LANES, 2 * tq), F32)],
        compiler_params=_params(("arbitrary", "arbitrary")),
        name="diff_attn",
    )(lam_vecs, subln_g, proj, proj, vt)


def _route_top2(logits):
    lane = lax.broadcasted_iota(jnp.int32, logits.shape, 1)
    lg = jnp.where(lane < N_EXPERTS, logits, -jnp.inf)
    v1 = jnp.max(lg, axis=-1, keepdims=True)
    i1 = jnp.min(jnp.where(lg == v1, lane, LANES), axis=-1, keepdims=True)
    lg2 = jnp.where(lane == i1, -jnp.inf, lg)
    v2 = jnp.max(lg2, axis=-1, keepdims=True)
    i2 = jnp.min(jnp.where(lg2 == v2, lane, LANES), axis=-1, keepdims=True)
    e = jnp.exp(v2 - v1)
    w1 = 1.0 / (1.0 + e)
    w2 = e / (1.0 + e)
    idx = jnp.where(lane == 0, i1, jnp.where(lane == 1, i2, 0))
    wgt = jnp.where(lane == 0, w1, jnp.where(lane == 1, w2, 0.0))
    return idx, wgt


def _outproj_kernel(ya_ref, yd_ref, w_ref, x_ref, mod_ref, g_ref, *rest, moe):
    if moe:
        rwh_ref, rwl_ref, x1_ref, h_ref, idx_ref, wgt_ref = rest
    else:
        x1_ref, h_ref = rest
    half = ya_ref.shape[1]
    y = (jnp.dot(ya_ref[...], w_ref[:half, :], preferred_element_type=F32)
         + jnp.dot(yd_ref[...], w_ref[half:, :], preferred_element_type=F32))
    x1 = x_ref[...] + mod_ref[0, 2:3, :] * y
    x1_ref[...] = x1
    h = _rms_mod(x1, g_ref[...], mod_ref[0, 4:5, :], mod_ref[0, 3:4, :])
    if moe:
        for c in range(SUBROWS):
            h_ref[pl.ds(c, h.shape[0], stride=SUBROWS), :] = h[:, c * LANES:(c + 1) * LANES]
    else:
        h_ref[...] = h.astype(h_ref.dtype)
    if moe:
        h_hi = h.astype(BF16)
        h_lo = (h - h_hi.astype(F32)).astype(BF16)
        logits = (jnp.dot(h_hi, rwh_ref[...], preferred_element_type=F32)
                  + jnp.dot(h_hi, rwl_ref[...], preferred_element_type=F32)
                  + jnp.dot(h_lo, rwh_ref[...], preferred_element_type=F32))
        idx, wgt = _route_top2(logits)
        idx_ref[...] = idx
        wgt_ref[...] = wgt


def _outproj(ya, yd, w, layer, x2, mod, g, seq, router_w=None):
    n, d = x2.shape
    tm = TM_PROJ
    per_b = seq // tm
    moe = router_w is not None
    row = lambda i: (i, 0)
    in_specs = [pl.BlockSpec((tm, ya.shape[1]), row),
                pl.BlockSpec((tm, yd.shape[1]), row),
                pl.BlockSpec((None,) + w.shape[1:], lambda i: (layer, 0, 0)),
                pl.BlockSpec((tm, d), row),
                pl.BlockSpec((1, N_MOD, d), lambda i: (i // per_b, 0, 0)),
                pl.BlockSpec((1, d), lambda i: (0, 0))]
    out_specs = [pl.BlockSpec((tm, d), row), pl.BlockSpec(_to_subrows((tm, d)) if moe else (tm, d), row)]
    out_shape = [jax.ShapeDtypeStruct((n, d), F32),
                 jax.ShapeDtypeStruct(_to_subrows((n, d)), F32) if moe else jax.ShapeDtypeStruct((n, d), BF16)]
    args = [ya, yd, w, x2, mod, g]
    if moe:
        in_specs += [pl.BlockSpec((d, LANES), lambda i: (0, 0))] * 2
        out_specs += [pl.BlockSpec((tm, LANES), row)] * 2
        out_shape += [jax.ShapeDtypeStruct((n, LANES), jnp.int32), jax.ShapeDtypeStruct((n, LANES), F32)]
        args += list(router_w)
    return pl.pallas_call(
        functools.partial(_outproj_kernel, moe=moe),
        grid=(n // tm,),
        in_specs=in_specs, out_specs=out_specs, out_shape=out_shape,
        compiler_params=_params(("arbitrary",)),
        name="outproj_moe" if moe else "outproj",
    )(*args)


def _swiglu_step(h, wg_ref, wu_ref, wd_ref):
    g = jnp.dot(h, wg_ref[...].astype(BF16), preferred_element_type=F32)
    u = jnp.dot(h, wu_ref[...].astype(BF16), preferred_element_type=F32)
    return jnp.dot((_silu(g) * u).astype(BF16), wd_ref[...].astype(BF16), preferred_element_type=F32)


def _ffn_kernel(h_ref, wg_ref, wu_ref, wd_ref, x_ref, mod_ref, o_ref, acc_ref):
    j = pl.program_id(1)

    @pl.when(j == 0)
    def _():
        acc_ref[...] = jnp.zeros(acc_ref.shape, F32)

    acc_ref[...] += _swiglu_step(h_ref[...], wg_ref, wu_ref, wd_ref)

    @pl.when(j == pl.num_programs(1) - 1)
    def _():
        o_ref[...] = x_ref[...] + mod_ref[0, 5:6, :] * acc_ref[...]


def _dense_ffn(h, wg, wu, wd, li, x1, mod, seq):
    n, d = x1.shape
    f = wg.shape[2]
    tm, tf = TM_FFN, TF_FFN
    per_b = seq // tm
    return pl.pallas_call(
        _ffn_kernel,
        grid=(n // tm, f // tf),
        in_specs=[pl.BlockSpec((tm, d), lambda i, j: (i, 0)),
                  pl.BlockSpec((None, d, tf), lambda i, j: (li, 0, j)),
                  pl.BlockSpec((None, d, tf), lambda i, j: (li, 0, j)),
                  pl.BlockSpec((None, tf, d), lambda i, j: (li, j, 0)),
                  pl.BlockSpec((tm, d), lambda i, j: (i, 0)),
                  pl.BlockSpec((1, N_MOD, d), lambda i, j: (i // per_b, 0, 0))],
        out_specs=pl.BlockSpec((tm, d), lambda i, j: (i, 0)),
        out_shape=jax.ShapeDtypeStruct((n, d), F32),
        scratch_shapes=[pltpu.VMEM((tm, d), F32)],
        compiler_params=_params(("arbitrary", "arbitrary")),
        name="dense_ffn",
    )(h, wg, wu, wd, x1, mod)


def _moe_plan(idx, tm):
    n = idx.shape[0]
    flat_e = idx.reshape(-1)
    onehot = (flat_e[:, None] == jnp.arange(N_EXPERTS, dtype=jnp.int32)[None, :]).astype(jnp.int32)
    csum = jnp.cumsum(onehot, axis=0)
    rank = jnp.sum((csum - onehot) * onehot, axis=1)
    counts = csum[-1]
    padded = (counts + tm - 1) // tm * tm
    pad_ends = jnp.cumsum(padded)
    pad_starts = pad_ends - padded
    dest = (pad_starts[flat_e] + rank).astype(jnp.int32)
    rows = 2 * n + N_EXPERTS * tm
    n_tiles = (pad_ends[-1] // tm).astype(jnp.int32).reshape(1)
    tile_start = jnp.arange(rows // tm, dtype=jnp.int32) * tm
    tile_e = jnp.minimum(jnp.sum((pad_ends[None, :] <= tile_start[:, None]).astype(jnp.int32), axis=1),
                         N_EXPERTS - 1).astype(jnp.int32)
    tile_rows = jnp.clip((pad_starts + counts)[tile_e] - tile_start, 0, tm).astype(jnp.int32)
    return dest, pad_ends.astype(jnp.int32), tile_e, tile_rows, n_tiles, rows


def _to_subrows(ref_or_shape):
    rows, d = ref_or_shape
    assert d == SUBROWS * LANES
    return (rows * SUBROWS, LANES)


def _row_copy(src_hbm, src_row, dst_ref, dst_row, sem):
    s = pl.multiple_of(src_row * SUBROWS, SUBROWS)
    t = pl.multiple_of(dst_row * SUBROWS, SUBROWS)
    return pltpu.make_async_copy(src_hbm.at[pl.ds(s, SUBROWS)], dst_ref.at[pl.ds(t, SUBROWS)], sem)


def _row_gather(idx_fns, src_hbm, dst_refs, sem, rows):
    def issue(r, carry):
        for k, (idx_of, dst_ref) in enumerate(zip(idx_fns, dst_refs)):
            _row_copy(src_hbm, idx_of(r), dst_ref, r, sem).start(priority=k % 2)
        return carry

    lax.fori_loop(0, rows, issue, 0, unroll=4)


def _row_gather_wait(src_hbm, dst_ref, sem, rows):
    pltpu.make_async_copy(src_hbm.at[pl.ds(0, rows * SUBROWS)], dst_ref, sem).wait()


def _moe_scatter_kernel(dest_ref, pe_ref, nt_ref, h_ref, xs_hbm, zeros, sem, *, tb, tm, n_tiles_max):
    i = pl.program_id(0)
    zero_sem, row_sem = sem.at[0], sem.at[1]
    tile_rows = tm * SUBROWS

    def zero_tile(t):
        start = pl.multiple_of(t * tile_rows, tile_rows)
        return pltpu.make_async_copy(zeros, xs_hbm.at[pl.ds(start, tile_rows)], zero_sem)

    @pl.when(i == 0)
    def _():
        zeros[...] = jnp.zeros(zeros.shape, zeros.dtype)
        for e in range(N_EXPERTS):
            last_tile = jnp.maximum(pe_ref[e] // tm - 1, 0)
            zero_tile(last_tile).start()
            zero_tile(last_tile).wait()
        for t in range(n_tiles_max):
            @pl.when(t >= nt_ref[0])
            def _():
                zero_tile(t).start()
                zero_tile(t).wait()

    base = i * tb

    def issue(r, carry):
        src = h_ref.at[pl.ds(pl.multiple_of(r * SUBROWS, SUBROWS), SUBROWS)]
        for k in range(2):
            d = dest_ref[2 * (base + r) + k]
            pltpu.make_async_copy(src, xs_hbm.at[pl.ds(pl.multiple_of(d * SUBROWS, SUBROWS), SUBROWS)],
                                  row_sem).start(priority=k)
        return carry

    lax.fori_loop(0, tb, issue, 0, unroll=4)
    for k in range(2):
        pltpu.make_async_copy(h_ref, xs_hbm.at[pl.ds(0, tb * SUBROWS)], row_sem).wait()


def _moe_scatter(h_sub, dest, pad_ends, n_tiles, rows_total, tm):
    n_sub = h_sub.shape[0]
    tb = TM_COMB
    d = SUBROWS * LANES
    n_tiles_max = rows_total // tm
    return pl.pallas_call(
        functools.partial(_moe_scatter_kernel, tb=tb, tm=tm, n_tiles_max=n_tiles_max),
        grid_spec=pltpu.PrefetchScalarGridSpec(
            num_scalar_prefetch=3,
            grid=(n_sub // (tb * SUBROWS),),
            in_specs=[pl.BlockSpec(_to_subrows((tb, d)), lambda i, de, pe, nt: (i, 0))],
            out_specs=pl.BlockSpec(memory_space=pl.ANY),
            scratch_shapes=[pltpu.VMEM(_to_subrows((tm, d)), F32), pltpu.SemaphoreType.DMA((2,))]),
        out_shape=jax.ShapeDtypeStruct(_to_subrows((rows_total, d)), F32),
        compiler_params=_params(("arbitrary",)),
        name="moe_scatter",
    )(dest, pad_ends, n_tiles, h_sub)


def _moe_ffn_kernel(te_ref, tr_ref, nt_ref, xs_ref, wg_ref, wu_ref, wd_ref, o_ref, xb16, acc_ref, *, nf):
    i = pl.program_id(0)
    j = pl.program_id(1)
    tm = xb16.shape[0]
    half = tm // 2

    @pl.when(i < nt_ref[0])
    def _():
        @pl.when(j == 0)
        def _():
            for c in range(SUBROWS):
                xb16[:, c * LANES:(c + 1) * LANES] = xs_ref[pl.ds(c, tm, stride=SUBROWS), :].astype(BF16)
            acc_ref[...] = jnp.zeros(acc_ref.shape, F32)

        @pl.when(tr_ref[i] > half)
        def _():
            acc_ref[...] += _swiglu_step(xb16[...], wg_ref.at[0], wu_ref.at[0], wd_ref.at[0])

        @pl.when(tr_ref[i] <= half)
        def _():
            acc_ref[:half, :] += _swiglu_step(xb16[:half, :], wg_ref.at[0], wu_ref.at[0], wd_ref.at[0])

        @pl.when(j == nf - 1)
        def _():
            for c in range(SUBROWS):
                o_ref[pl.ds(c, tm, stride=SUBROWS), :] = acc_ref[:, c * LANES:(c + 1) * LANES]

    @pl.when(jnp.logical_and(i >= nt_ref[0], j == nf - 1))
    def _():
        o_ref[...] = jnp.zeros(o_ref.shape, o_ref.dtype)


def _moe_ffn(xs, wg, wu, wd, li, tile_e, tile_rows, n_tiles, tm):
    d = wg.shape[2]
    f = wg.shape[3]
    rows_total = xs.shape[0] // SUBROWS
    tf = TF_FFN
    nf = f // tf

    def row_map(i, j, te, tr, nt):
        return (jnp.minimum(i, nt[0] - 1), 0)

    def _fj(i, j, nt):
        return jnp.where(i < nt[0], j, nf - 1)

    def up_map(i, j, te, tr, nt):
        return (li, te[jnp.minimum(i, nt[0] - 1)], 0, _fj(i, j, nt))

    def down_map(i, j, te, tr, nt):
        return (li, te[jnp.minimum(i, nt[0] - 1)], _fj(i, j, nt), 0)

    return pl.pallas_call(
        functools.partial(_moe_ffn_kernel, nf=nf),
        grid_spec=pltpu.PrefetchScalarGridSpec(
            num_scalar_prefetch=3,
            grid=(rows_total // tm, nf),
            in_specs=[pl.BlockSpec(_to_subrows((tm, d)), row_map),
                      pl.BlockSpec((None, 1, d, tf), up_map),
                      pl.BlockSpec((None, 1, d, tf), up_map),
                      pl.BlockSpec((None, 1, tf, d), down_map)],
            out_specs=pl.BlockSpec(_to_subrows((tm, d)), lambda i, j, te, tr, nt: (i, 0)),
            scratch_shapes=[pltpu.VMEM((tm, d), BF16), pltpu.VMEM((tm, d), F32)]),
        out_shape=jax.ShapeDtypeStruct(_to_subrows((rows_total, d)), F32),
        compiler_params=_params(("arbitrary", "arbitrary")),
        name="moe_ffn",
    )(tile_e, tile_rows, n_tiles, xs, wg, wu, wd)


def _combine_kernel(pos_ref, ys_hbm, wgt_ref, x_ref, mod_ref, fg_ref, o_ref, buf, sem, *, tm, final):
    i = pl.program_id(0)
    slot = i % 2

    def gather(step, to_slot):
        _row_gather([lambda r: pos_ref[2 * (step * tm + r)], lambda r: pos_ref[2 * (step * tm + r) + 1]],
                    ys_hbm, [buf.at[to_slot, 0], buf.at[to_slot, 1]], sem.at[to_slot], tm)

    @pl.when(i == 0)
    def _():
        gather(0, 0)

    @pl.when(i + 1 < pl.num_programs(0))
    def _():
        gather(i + 1, 1 - slot)

    for k in range(2):
        _row_gather_wait(ys_hbm, buf.at[slot, k], sem.at[slot], tm)

    wgt = wgt_ref[...]
    w0, w1 = wgt[:, 0:1], wgt[:, 1:2]
    y = jnp.concatenate([w0 * buf[slot, 0, pl.ds(c, tm, stride=SUBROWS), :]
                         + w1 * buf[slot, 1, pl.ds(c, tm, stride=SUBROWS), :] for c in range(SUBROWS)], axis=1)
    out = x_ref[...] + mod_ref[0, 5:6, :] * y
    if final:
        out = out * lax.rsqrt(jnp.mean(out * out, axis=-1, keepdims=True) + EPS) * fg_ref[...]
    o_ref[...] = out


def _moe_combine(ys, pos, wgt, x1, mod, final_g, seq, final):
    n, d = x1.shape
    tm = TM_COMB
    per_b = seq // tm
    return pl.pallas_call(
        functools.partial(_combine_kernel, tm=tm, final=final),
        grid_spec=pltpu.PrefetchScalarGridSpec(
            num_scalar_prefetch=1,
            grid=(n // tm,),
            in_specs=[pl.BlockSpec(memory_space=pl.ANY),
                      pl.BlockSpec((tm, LANES), lambda i, p: (i, 0)),
                      pl.BlockSpec((tm, d), lambda i, p: (i, 0)),
                      pl.BlockSpec((1, N_MOD, d), lambda i, p: (i // per_b, 0, 0)),
                      pl.BlockSpec((1, d), lambda i, p: (0, 0))],
            out_specs=pl.BlockSpec((tm, d), lambda i, p: (i, 0)),
            scratch_shapes=[pltpu.VMEM((2, 2) + _to_subrows((tm, d)), F32),
                            pltpu.SemaphoreType.DMA((2,))]),
        out_shape=jax.ShapeDtypeStruct((n, d), F32),
        compiler_params=_params(("arbitrary",)),
        name="moe_combine_final" if final else "moe_combine",
    )(pos.reshape(-1), ys, wgt, x1, mod, final_g)


def _final_norm_kernel(x_ref, g_ref, o_ref):
    x = x_ref[...]
    o_ref[...] = x * lax.rsqrt(jnp.mean(x * x, axis=-1, keepdims=True) + EPS) * g_ref[...]


def _final_norm(x2, g):
    n, d = x2.shape
    tm = TM_PROJ
    return pl.pallas_call(
        _final_norm_kernel,
        grid=(n // tm,),
        in_specs=[pl.BlockSpec((tm, d), lambda i: (i, 0)), pl.BlockSpec((1, d), lambda i: (0, 0))],
        out_specs=pl.BlockSpec((tm, d), lambda i: (i, 0)),
        out_shape=jax.ShapeDtypeStruct((n, d), F32),
        compiler_params=_params(("arbitrary",)),
        name="final_norm",
    )(x2, g)


def kernel(x, c, positions, ada_w, ada_b, norm_mix_g, norm_ffn_g, w_in, w_out, sink, lam_q1, lam_k1, lam_q2, lam_k2, subln_g, ffn_w_gate, ffn_w_up, ffn_w_down, router_w, moe_w_gate, moe_w_up, moe_w_down, final_g):
    batch, seq, d = x.shape
    depth = w_in.shape[0]
    n = batch * seq

    c_pad = jnp.zeros((8, d), F32).at[:batch].set(c)
    mod_all = _adaln(c_pad, ada_w, ada_b)
    cos_t, sin_t = _rope_tables(positions)
    w_in_p = _permute_in_proj(w_in).astype(BF16)
    w_vdt = _value_rows(w_in).astype(BF16)
    w_out_b = w_out.astype(BF16)
    ffn_w = [ffn_w_gate, ffn_w_up, ffn_w_down]
    moe_w = [moe_w_gate, moe_w_up, moe_w_down]
    final_g2 = final_g.reshape(1, d)

    x2 = x.reshape(n, d)
    for layer in range(depth):
        lambda_init = 0.8 - 0.6 * math.exp(-0.3 * layer)
        mod = mod_all[layer, :batch].reshape(batch, N_MOD, d)
        proj, vt = _inproj(x2, mod, norm_mix_g[layer].reshape(1, d), w_in_p, w_vdt, layer, cos_t, sin_t, seq)
        ya = _window_attention(proj, vt, sink[layer], batch, seq)
        lam_vecs = jnp.stack([lam_q1[layer], lam_k1[layer], lam_q2[layer], lam_k2[layer]])
        yd = _diff_attention(proj, vt, lam_vecs, subln_g[layer].reshape(1, LANES), batch, seq, lambda_init)
        i = layer // 2
        g_ffn = norm_ffn_g[layer].reshape(1, d)
        if layer % 2 == 0:
            x1, h = _outproj(ya, yd, w_out_b, layer, x2, mod, g_ffn, seq)
            x2 = _dense_ffn(h, *ffn_w, i, x1, mod, seq)
        else:
            rw = jnp.zeros((d, LANES), F32).at[:, :N_EXPERTS].set(router_w[i])
            rw_hi = rw.astype(BF16)
            rw_lo = (rw - rw_hi.astype(F32)).astype(BF16)
            x1, h, idx, wgt = _outproj(ya, yd, w_out_b, layer, x2, mod, g_ffn, seq, router_w=(rw_hi, rw_lo))
            dest, pad_ends, tile_e, tile_rows, n_tiles, rows_total = _moe_plan(idx[:, :2], TM_MOE)
            pos = dest.reshape(n, 2)
            xs = _moe_scatter(h, dest, pad_ends, n_tiles, rows_total, TM_MOE)
            ys = _moe_ffn(xs, *moe_w, i, tile_e, tile_rows, n_tiles, TM_MOE)
            final = layer == depth - 1
            x2 = _moe_combine(ys, pos, wgt, x1, mod, final_g2, seq, final)
    if depth % 2 == 1:
        x2 = _final_norm(x2, final_g2)
    return x2.reshape(batch, seq, d)
```

```python
import functools
import math

import jax
import jax.numpy as jnp
from jax import lax
from jax.experimental import pallas as pl
from jax.experimental.pallas import tpu as pltpu

F32 = jnp.float32
BF16 = jnp.bfloat16

HEAD_DIM = 64
HALF_DIM = HEAD_DIM // 2
ROPE_THETA = 10000.0
EPS = 1e-6
SWA_Q_HEADS = 8
SWA_KV_HEADS = 2
WINDOW = 128
DIFF_HEADS = 4
N_EXPERTS = 8
N_MOD = 6
LANES = 128
SUBROWS = 8
VMEM_LIMIT = 56 * 1024 * 1024

QA_BLK, KA_BLK, QD_BLK, KD_BLK = 0, 4, 6, 10
ROPE_BLKS = 14
PROJ_COLS = ROPE_BLKS * LANES
VA_COL0 = 640
VD_COL0 = 1792
VT_WIN_BLK = 4
Q_SCALE = HEAD_DIM ** -0.5
LOG2E = math.log2(math.e)

TM_PROJ = 1024
TQ_WIN = 256
TQ_DIFF = 512
TK_DIFF = 1024
TM_FFN = 1024
TF_FFN = 512
TM_MOE = 1024
TM_COMB = 512


def _permute_in_proj(w_in):
    depth, d, _ = w_in.shape
    qa, ka, qd, kd = 0, 512, 768, 1280

    def pairs(lo, n_pairs):
        t = w_in[:, :, lo:lo + n_pairs * LANES].reshape(depth, d, n_pairs, 2, 2, HALF_DIM)
        return t.transpose(0, 1, 2, 4, 3, 5).reshape(depth, d, n_pairs * LANES)

    def dup_halves(lo):
        t = w_in[:, :, lo:lo + SWA_KV_HEADS * HEAD_DIM].reshape(depth, d, SWA_KV_HEADS, 2, 1, HALF_DIM)
        return jnp.broadcast_to(t, (depth, d, SWA_KV_HEADS, 2, 2, HALF_DIM)).reshape(depth, d, SWA_KV_HEADS * LANES)

    return jnp.concatenate([pairs(qa, 4), dup_halves(ka), pairs(qd, DIFF_HEADS), pairs(kd, DIFF_HEADS)], axis=-1)


def _value_rows(w_in):
    depth, d, _ = w_in.shape
    vd = jnp.swapaxes(w_in[:, :, VD_COL0:], 1, 2)
    va = jnp.swapaxes(w_in[:, :, VA_COL0:VA_COL0 + SWA_KV_HEADS * HEAD_DIM], 1, 2)
    va = jnp.broadcast_to(va.reshape(depth, SWA_KV_HEADS, 1, HEAD_DIM, d), (depth, SWA_KV_HEADS, 2, HEAD_DIM, d))
    return jnp.concatenate([vd, va.reshape(depth, SWA_KV_HEADS * LANES, d)], axis=1)


def _first_of_pair_mask():
    lane = lax.broadcasted_iota(jnp.int32, (1, LANES), 1)
    return (lane // HALF_DIM) % 2 == 0


def _silu(g):
    return g / (1.0 + jnp.exp(-g))


def _params(sem):
    return pltpu.CompilerParams(dimension_semantics=sem, vmem_limit_bytes=VMEM_LIMIT)


def _adaln_kernel(c_ref, w_ref, b_ref, o_ref):
    a = _silu(c_ref[...])
    o_ref[0] = jnp.dot(a, w_ref[0], preferred_element_type=F32,
                       precision=lax.Precision.HIGHEST) + b_ref[0]


def _adaln(c_pad, ada_w, ada_b):
    depth, d, cols = ada_w.shape
    tn = 1536
    return pl.pallas_call(
        _adaln_kernel,
        grid=(depth, cols // tn),
        in_specs=[pl.BlockSpec((8, d), lambda l, j: (0, 0)),
                  pl.BlockSpec((1, d, tn), lambda l, j: (l, 0, j)),
                  pl.BlockSpec((1, 1, tn), lambda l, j: (l, 0, j))],
        out_specs=pl.BlockSpec((1, 8, tn), lambda l, j: (l, 0, j)),
        out_shape=jax.ShapeDtypeStruct((depth, 8, cols), F32),
        compiler_params=_params(("arbitrary", "arbitrary")),
        name="adaln",
    )(c_pad, ada_w, ada_b.reshape(depth, 1, cols))


def _rope_table_kernel(pos_ref, cos_ref, sin_ref):
    lane = lax.broadcasted_iota(jnp.int32, (1, LANES), 1)
    freq = (lane % HALF_DIM).astype(F32)
    inv = 1.0 / (ROPE_THETA ** (2.0 * freq / HEAD_DIM))
    ang = pos_ref[...].astype(F32) * inv
    sign = jnp.where(lane < HEAD_DIM, -1.0, 1.0)
    cos_ref[...] = jnp.cos(ang)
    sin_ref[...] = jnp.sin(ang) * sign


def _rope_tables(positions):
    n = positions.size
    tm = 1024
    pos = positions.reshape(n, 1)
    return pl.pallas_call(
        _rope_table_kernel,
        grid=(n // tm,),
        in_specs=[pl.BlockSpec((tm, 1), lambda i: (i, 0))],
        out_specs=[pl.BlockSpec((tm, LANES), lambda i: (i, 0))] * 2,
        out_shape=[jax.ShapeDtypeStruct((n, LANES), F32)] * 2,
        compiler_params=_params(("arbitrary",)),
        name="rope_tables",
    )(pos)


def _rms_mod(x, g, scale, shift):
    y = x * lax.rsqrt(jnp.mean(x * x, axis=-1, keepdims=True) + EPS) * g
    return y * (1.0 + scale) + shift


def _inproj_kernel(x_ref, mod_ref, g_ref, w_ref, wvt_ref, cos_ref, sin_ref, o_ref, vt_ref):
    h = _rms_mod(x_ref[...], g_ref[...], mod_ref[0, 1:2, :], mod_ref[0, 0:1, :]).astype(BF16)
    proj = jnp.dot(h, w_ref[...], preferred_element_type=F32)
    vt_ref[...] = lax.dot_general(wvt_ref[...], h, (((1,), (1,)), ((), ())),
                                  preferred_element_type=F32).astype(BF16)
    cos = cos_ref[...]
    sin = sin_ref[...]
    for blk in range(ROPE_BLKS):
        t = proj[:, blk * LANES:(blk + 1) * LANES]
        r = t * cos + pltpu.roll(t, HEAD_DIM, axis=1) * sin
        if blk < KA_BLK or QD_BLK <= blk < KD_BLK:
            r = r * (Q_SCALE * LOG2E)
        o_ref[:, blk * LANES:(blk + 1) * LANES] = r.astype(BF16)


def _inproj(x2, mod, g, w, wvt, layer, cos_t, sin_t, seq):
    n, d = x2.shape
    tm = TM_PROJ
    per_b = seq // tm
    vt_rows = wvt.shape[1]
    return pl.pallas_call(
        _inproj_kernel,
        grid=(n // tm,),
        in_specs=[pl.BlockSpec((tm, d), lambda i: (i, 0)),
                  pl.BlockSpec((1, N_MOD, d), lambda i: (i // per_b, 0, 0)),
                  pl.BlockSpec((1, d), lambda i: (0, 0)),
                  pl.BlockSpec((None, d, PROJ_COLS), lambda i: (layer, 0, 0)),
                  pl.BlockSpec((None, vt_rows, d), lambda i: (layer, 0, 0)),
                  pl.BlockSpec((tm, LANES), lambda i: (i, 0)),
                  pl.BlockSpec((tm, LANES), lambda i: (i, 0))],
        out_specs=[pl.BlockSpec((tm, PROJ_COLS), lambda i: (i, 0)),
                   pl.BlockSpec((vt_rows, tm), lambda i: (0, i))],
        out_shape=[jax.ShapeDtypeStruct((n, PROJ_COLS), BF16),
                   jax.ShapeDtypeStruct((vt_rows, n), BF16)],
        compiler_params=_params(("arbitrary",)),
        name="inproj",
    )(x2, mod, g, w, wvt, cos_t, sin_t)


def _win_kernel(sink_ref, q_ref, k_ref, vt_ref, o_ref, qt_sc, *, tq, seq):
    i = pl.program_id(1)
    heads = SWA_Q_HEADS // SWA_KV_HEADS
    width = tq + 2 * WINDOW
    start = pl.multiple_of(jnp.clip(i * tq - WINDOW, 0, seq - width), LANES)
    first = _first_of_pair_mask()
    kpos = start + lax.broadcasted_iota(jnp.int32, (width, tq), 0)
    qpos = i * tq + lax.broadcasted_iota(jnp.int32, (width, tq), 1)
    valid = jnp.abs(qpos - kpos) <= WINDOW
    col = lax.broadcasted_iota(jnp.int32, (1, heads * tq), 1)
    lane = lax.broadcasted_iota(jnp.int32, (1, LANES), 1)
    for g in range(SWA_KV_HEADS):
        for blk in range(heads // 2):
            qb = (heads // 2) * g + blk
            q = q_ref[:, qb * LANES:(qb + 1) * LANES].astype(F32)
            zero = jnp.zeros_like(q)
            qt_sc[g, :, (2 * blk) * tq:(2 * blk + 1) * tq] = jnp.where(first, q, zero).T.astype(BF16)
            qt_sc[g, :, (2 * blk + 1) * tq:(2 * blk + 2) * tq] = jnp.where(first, zero, q).T.astype(BF16)

        s = jnp.dot(k_ref[pl.ds(start, width), g * LANES:(g + 1) * LANES], qt_sc[g],
                    preferred_element_type=F32)
        s = jnp.concatenate([jnp.where(valid, s[:, c * tq:(c + 1) * tq], -1e30) for c in range(heads)], axis=1)
        sk = jnp.zeros((1, heads * tq), F32)
        for c in range(heads):
            sk = jnp.where(col // tq == c, sink_ref[heads * g + c] * LOG2E, sk)
        m = jnp.maximum(jnp.max(s, axis=0, keepdims=True), sk)
        p = jnp.exp2(s - m)
        den = jnp.sum(p, axis=0, keepdims=True) + jnp.exp2(sk - m)
        o = jnp.dot(vt_ref[g * LANES:(g + 1) * LANES, pl.ds(start, width)], p.astype(BF16),
                    preferred_element_type=F32) / den
        for blk in range(heads // 2):
            qb = (heads // 2) * g + blk
            oa = o[:, (2 * blk) * tq:(2 * blk + 1) * tq].T
            ob = o[:, (2 * blk + 1) * tq:(2 * blk + 2) * tq].T
            o_ref[:, qb * LANES:(qb + 1) * LANES] = jnp.where(lane < HEAD_DIM, oa, ob).astype(o_ref.dtype)


def _window_attention(proj, vt, sink, batch, seq):
    n = proj.shape[0]
    tq = TQ_WIN
    nq = seq // tq
    heads = SWA_Q_HEADS // SWA_KV_HEADS
    q_cols = SWA_Q_HEADS // 2 * LANES
    kv_cols = SWA_KV_HEADS * LANES
    assert QA_BLK == 0 and KA_BLK * LANES % kv_cols == 0 and VT_WIN_BLK * LANES % kv_cols == 0
    kern = functools.partial(_win_kernel, tq=tq, seq=seq)
    return pl.pallas_call(
        kern,
        grid=(batch, nq),
        in_specs=[pl.BlockSpec(memory_space=pltpu.SMEM),
                  pl.BlockSpec((tq, q_cols), lambda b, i: (b * nq + i, 0)),
                  pl.BlockSpec((seq, kv_cols), lambda b, i: (b, KA_BLK * LANES // kv_cols)),
                  pl.BlockSpec((kv_cols, seq), lambda b, i: (VT_WIN_BLK * LANES // kv_cols, b))],
        out_specs=pl.BlockSpec((tq, q_cols), lambda b, i: (b * nq + i, 0)),
        out_shape=jax.ShapeDtypeStruct((n, q_cols), BF16),
        scratch_shapes=[pltpu.VMEM((SWA_KV_HEADS, LANES, heads * tq), BF16)],
        compiler_params=_params(("arbitrary", "arbitrary")),
        name="window_attn",
    )(sink, proj, proj, vt)


def _diff_kernel(lam_ref, g_ref, q_ref, k_ref, vt_ref, o_ref, qt_sc, s_a, s_b, cm_a, cm_b, m_sc, l_sc, acc_sc,
                 *, tq, tk, seq, lambda_init):
    nq = seq // tq
    nk = seq // tk
    first = _first_of_pair_mask()
    bufs = ((s_a, cm_a), (s_b, cm_b))

    def load_queries(i):
        row = pl.multiple_of(i * tq, tq)
        q = q_ref[pl.ds(row, tq), :].astype(F32)
        zero = jnp.zeros_like(q)
        qt_sc[:, :tq] = jnp.where(first, q, zero).T.astype(BF16)
        qt_sc[:, tq:] = jnp.where(first, zero, q).T.astype(BF16)

    def reset():
        m_sc[...] = jnp.full(m_sc.shape, -1e30, F32)
        l_sc[...] = jnp.zeros(l_sc.shape, F32)
        acc_sc[...] = jnp.zeros(acc_sc.shape, F32)

    def scores(t, s_ref, cm_ref):
        s = jnp.dot(k_ref[t * tk:(t + 1) * tk, :], qt_sc[...], preferred_element_type=F32)
        s_ref[...] = s
        cm_ref[...] = jnp.max(s, axis=0, keepdims=True)

    def consume(t, s_ref, cm_ref):
        m_prev = m_sc[...]
        m_new = jnp.maximum(m_prev, cm_ref[...])
        alpha = jnp.exp2(m_prev - m_new)
        p = jnp.exp2(s_ref[...] - m_new)
        l_sc[...] = alpha * l_sc[...] + jnp.sum(p, axis=0, keepdims=True)
        vb = vt_ref[:, t * tk:(t + 1) * tk]
        acc_sc[...] = alpha * acc_sc[...] + jnp.dot(vb, p.astype(BF16), preferred_element_type=F32)
        m_sc[...] = m_new

    lam1 = jnp.sum(lam_ref[0:1, :] * lam_ref[1:2, :], axis=-1, keepdims=True)
    lam2 = jnp.sum(lam_ref[2:3, :] * lam_ref[3:4, :], axis=-1, keepdims=True)
    lam = jnp.exp(lam1) - jnp.exp(lam2) + lambda_init
    gain = g_ref[...] * (1.0 - lambda_init)

    def finish(i):
        o = acc_sc[...] / l_sc[...]
        o = o[:, :tq] - lam * o[:, tq:]
        y = o * lax.rsqrt(jnp.mean(o * o, axis=0, keepdims=True) + EPS)
        row = pl.multiple_of(i * tq, tq)
        o_ref[pl.ds(row, tq), :] = (y.T * gain).astype(o_ref.dtype)

    load_queries(0)
    scores(0, *bufs[0])

    def tile(i, carry):
        reset()
        for t in range(nk):
            if t + 1 < nk:
                scores(t + 1, *bufs[(t + 1) % 2])
            else:
                load_queries(jnp.minimum(i + 1, nq - 1))
                scores(0, *bufs[0])
            consume(t, *bufs[t % 2])
        finish(i)
        return carry

    lax.fori_loop(0, nq, tile, 0)


def _diff_attention(proj, vt, lam_vecs, subln_g, batch, seq, lambda_init):
    n = proj.shape[0]
    tq, tk = TQ_DIFF, TK_DIFF
    assert seq % (2 * tk) == 0 and seq % tq == 0, "key blocks alternate between two score buffers"
    kern = functools.partial(_diff_kernel, tq=tq, tk=tk, seq=seq, lambda_init=lambda_init)
    return pl.pallas_call(
        kern,
        grid=(batch, DIFF_HEADS),
        in_specs=[pl.BlockSpec((4, HEAD_DIM), lambda b, h: (0, 0)),
                  pl.BlockSpec((1, LANES), lambda b, h: (0, 0)),
                  pl.BlockSpec((seq, LANES), lambda b, h: (b, QD_BLK + h)),
                  pl.BlockSpec((seq, LANES), lambda b, h: (b, KD_BLK + h)),
                  pl.BlockSpec((LANES, seq), lambda b, h: (h, b))],
        out_specs=pl.BlockSpec((seq, LANES), lambda b, h: (b, h)),
        out_shape=jax.ShapeDtypeStruct((n, DIFF_HEADS * LANES), BF16),
        scratch_shapes=[pltpu.VMEM((LANES, 2 * tq), BF16),
                        pltpu.VMEM((tk, 2 * tq), F32), pltpu.VMEM((tk, 2 * tq), F32),
                        pltpu.VMEM((1, 2 * tq), F32), pltpu.VMEM((1, 2 * tq), F32),
                        pltpu.VMEM((1, 2 * tq), F32), pltpu.VMEM((1, 2 * tq), F32),
                        pltpu.VMEM((LANES, 2 * tq), F32)],
        compiler_params=_params(("arbitrary", "arbitrary")),
        name="diff_attn",
    )(lam_vecs, subln_g, proj, proj, vt)


def _route_top2(logits):
    lane = lax.broadcasted_iota(jnp.int32, logits.shape, 1)
    lg = jnp.where(lane < N_EXPERTS, logits, -jnp.inf)
    v1 = jnp.max(lg, axis=-1, keepdims=True)
    i1 = jnp.min(jnp.where(lg == v1, lane, LANES), axis=-1, keepdims=True)
    lg2 = jnp.where(lane == i1, -jnp.inf, lg)
    v2 = jnp.max(lg2, axis=-1, keepdims=True)
    i2 = jnp.min(jnp.where(lg2 == v2, lane, LANES), axis=-1, keepdims=True)
    e = jnp.exp(v2 - v1)
    w1 = 1.0 / (1.0 + e)
    w2 = e / (1.0 + e)
    idx = jnp.where(lane == 0, i1, jnp.where(lane == 1, i2, 0))
    wgt = jnp.where(lane == 0, w1, jnp.where(lane == 1, w2, 0.0))
    return idx, wgt


def _outproj_kernel(ya_ref, yd_ref, w_ref, x_ref, mod_ref, g_ref, *rest, moe):
    if moe:
        rwh_ref, rwl_ref, x1_ref, h_ref, idx_ref, wgt_ref = rest
    else:
        x1_ref, h_ref = rest
    half = ya_ref.shape[1]
    y = (jnp.dot(ya_ref[...], w_ref[:half, :], preferred_element_type=F32)
         + jnp.dot(yd_ref[...], w_ref[half:, :], preferred_element_type=F32))
    x1 = x_ref[...] + mod_ref[0, 2:3, :] * y
    x1_ref[...] = x1
    h = _rms_mod(x1, g_ref[...], mod_ref[0, 4:5, :], mod_ref[0, 3:4, :])
    if moe:
        for c in range(SUBROWS):
            h_ref[pl.ds(c, h.shape[0], stride=SUBROWS), :] = h[:, c * LANES:(c + 1) * LANES]
    else:
        h_ref[...] = h.astype(h_ref.dtype)
    if moe:
        h_hi = h.astype(BF16)
        h_lo = (h - h_hi.astype(F32)).astype(BF16)
        logits = (jnp.dot(h_hi, rwh_ref[...], preferred_element_type=F32)
                  + jnp.dot(h_hi, rwl_ref[...], preferred_element_type=F32)
                  + jnp.dot(h_lo, rwh_ref[...], preferred_element_type=F32))
        idx, wgt = _route_top2(logits)
        idx_ref[...] = idx
        wgt_ref[...] = wgt


def _outproj(ya, yd, w, layer, x2, mod, g, seq, router_w=None):
    n, d = x2.shape
    tm = TM_PROJ
    per_b = seq // tm
    moe = router_w is not None
    row = lambda i: (i, 0)
    in_specs = [pl.BlockSpec((tm, ya.shape[1]), row),
                pl.BlockSpec((tm, yd.shape[1]), row),
                pl.BlockSpec((None,) + w.shape[1:], lambda i: (layer, 0, 0)),
                pl.BlockSpec((tm, d), row),
                pl.BlockSpec((1, N_MOD, d), lambda i: (i // per_b, 0, 0)),
                pl.BlockSpec((1, d), lambda i: (0, 0))]
    out_specs = [pl.BlockSpec((tm, d), row), pl.BlockSpec(_to_subrows((tm, d)) if moe else (tm, d), row)]
    out_shape = [jax.ShapeDtypeStruct((n, d), F32),
                 jax.ShapeDtypeStruct(_to_subrows((n, d)), F32) if moe else jax.ShapeDtypeStruct((n, d), BF16)]
    args = [ya, yd, w, x2, mod, g]
    if moe:
        in_specs += [pl.BlockSpec((d, LANES), lambda i: (0, 0))] * 2
        out_specs += [pl.BlockSpec((tm, LANES), row)] * 2
        out_shape += [jax.ShapeDtypeStruct((n, LANES), jnp.int32), jax.ShapeDtypeStruct((n, LANES), F32)]
        args += list(router_w)
    return pl.pallas_call(
        functools.partial(_outproj_kernel, moe=moe),
        grid=(n // tm,),
        in_specs=in_specs, out_specs=out_specs, out_shape=out_shape,
        compiler_params=_params(("arbitrary",)),
        name="outproj_moe" if moe else "outproj",
    )(*args)


def _swiglu_step(h, wg_ref, wu_ref, wd_ref):
    g = jnp.dot(h, wg_ref[...].astype(BF16), preferred_element_type=F32)
    u = jnp.dot(h, wu_ref[...].astype(BF16), preferred_element_type=F32)
    return jnp.dot((_silu(g) * u).astype(BF16), wd_ref[...].astype(BF16), preferred_element_type=F32)


def _ffn_kernel(h_ref, wg_ref, wu_ref, wd_ref, x_ref, mod_ref, o_ref, acc_ref):
    j = pl.program_id(1)

    @pl.when(j == 0)
    def _():
        acc_ref[...] = jnp.zeros(acc_ref.shape, F32)

    acc_ref[...] += _swiglu_step(h_ref[...], wg_ref, wu_ref, wd_ref)

    @pl.when(j == pl.num_programs(1) - 1)
    def _():
        o_ref[...] = x_ref[...] + mod_ref[0, 5:6, :] * acc_ref[...]


def _dense_ffn(h, wg, wu, wd, li, x1, mod, seq):
    n, d = x1.shape
    f = wg.shape[2]
    tm, tf = TM_FFN, TF_FFN
    per_b = seq // tm
    return pl.pallas_call(
        _ffn_kernel,
        grid=(n // tm, f // tf),
        in_specs=[pl.BlockSpec((tm, d), lambda i, j: (i, 0)),
                  pl.BlockSpec((None, d, tf), lambda i, j: (li, 0, j)),
                  pl.BlockSpec((None, d, tf), lambda i, j: (li, 0, j)),
                  pl.BlockSpec((None, tf, d), lambda i, j: (li, j, 0)),
                  pl.BlockSpec((tm, d), lambda i, j: (i, 0)),
                  pl.BlockSpec((1, N_MOD, d), lambda i, j: (i // per_b, 0, 0))],
        out_specs=pl.BlockSpec((tm, d), lambda i, j: (i, 0)),
        out_shape=jax.ShapeDtypeStruct((n, d), F32),
        scratch_shapes=[pltpu.VMEM((tm, d), F32)],
        compiler_params=_params(("arbitrary", "arbitrary")),
        name="dense_ffn",
    )(h, wg, wu, wd, x1, mod)


def _moe_plan(idx, tm):
    n = idx.shape[0]
    flat_e = idx.reshape(-1)
    onehot = (flat_e[:, None] == jnp.arange(N_EXPERTS, dtype=jnp.int32)[None, :]).astype(jnp.int32)
    csum = jnp.cumsum(onehot, axis=0)
    rank = jnp.sum((csum - onehot) * onehot, axis=1)
    counts = csum[-1]
    padded = (counts + tm - 1) // tm * tm
    pad_ends = jnp.cumsum(padded)
    pad_starts = pad_ends - padded
    dest = (pad_starts[flat_e] + rank).astype(jnp.int32)
    rows = 2 * n + N_EXPERTS * tm
    n_tiles = (pad_ends[-1] // tm).astype(jnp.int32).reshape(1)
    tile_start = jnp.arange(rows // tm, dtype=jnp.int32) * tm
    tile_e = jnp.minimum(jnp.sum((pad_ends[None, :] <= tile_start[:, None]).astype(jnp.int32), axis=1),
                         N_EXPERTS - 1).astype(jnp.int32)
    tile_rows = jnp.clip((pad_starts + counts)[tile_e] - tile_start, 0, tm).astype(jnp.int32)
    return dest, pad_ends.astype(jnp.int32), tile_e, tile_rows, n_tiles, rows


def _to_subrows(ref_or_shape):
    rows, d = ref_or_shape
    assert d == SUBROWS * LANES
    return (rows * SUBROWS, LANES)


def _row_copy(src_hbm, src_row, dst_ref, dst_row, sem):
    s = pl.multiple_of(src_row * SUBROWS, SUBROWS)
    t = pl.multiple_of(dst_row * SUBROWS, SUBROWS)
    return pltpu.make_async_copy(src_hbm.at[pl.ds(s, SUBROWS)], dst_ref.at[pl.ds(t, SUBROWS)], sem)


def _row_gather(idx_fns, src_hbm, dst_refs, sem, rows):
    def issue(r, carry):
        for k, (idx_of, dst_ref) in enumerate(zip(idx_fns, dst_refs)):
            _row_copy(src_hbm, idx_of(r), dst_ref, r, sem).start(priority=k % 2)
        return carry

    lax.fori_loop(0, rows, issue, 0, unroll=4)


def _row_gather_wait(src_hbm, dst_ref, sem, rows):
    pltpu.make_async_copy(src_hbm.at[pl.ds(0, rows * SUBROWS)], dst_ref, sem).wait()


def _moe_scatter_kernel(dest_ref, pe_ref, nt_ref, h_ref, xs_hbm, zeros, sem, *, tb, tm, n_tiles_max):
    i = pl.program_id(0)
    zero_sem, row_sem = sem.at[0], sem.at[1]
    tile_rows = tm * SUBROWS

    def zero_tile(t):
        start = pl.multiple_of(t * tile_rows, tile_rows)
        return pltpu.make_async_copy(zeros, xs_hbm.at[pl.ds(start, tile_rows)], zero_sem)

    @pl.when(i == 0)
    def _():
        zeros[...] = jnp.zeros(zeros.shape, zeros.dtype)
        for e in range(N_EXPERTS):
            last_tile = jnp.maximum(pe_ref[e] // tm - 1, 0)
            zero_tile(last_tile).start()
            zero_tile(last_tile).wait()
        for t in range(n_tiles_max):
            @pl.when(t >= nt_ref[0])
            def _():
                zero_tile(t).start()
                zero_tile(t).wait()

    base = i * tb

    def issue(r, carry):
        src = h_ref.at[pl.ds(pl.multiple_of(r * SUBROWS, SUBROWS), SUBROWS)]
        for k in range(2):
            d = dest_ref[2 * (base + r) + k]
            pltpu.make_async_copy(src, xs_hbm.at[pl.ds(pl.multiple_of(d * SUBROWS, SUBROWS), SUBROWS)],
                                  row_sem).start(priority=k)
        return carry

    lax.fori_loop(0, tb, issue, 0, unroll=4)
    for k in range(2):
        pltpu.make_async_copy(h_ref, xs_hbm.at[pl.ds(0, tb * SUBROWS)], row_sem).wait()


def _moe_scatter(h_sub, dest, pad_ends, n_tiles, rows_total, tm):
    n_sub = h_sub.shape[0]
    tb = TM_COMB
    d = SUBROWS * LANES
    n_tiles_max = rows_total // tm
    return pl.pallas_call(
        functools.partial(_moe_scatter_kernel, tb=tb, tm=tm, n_tiles_max=n_tiles_max),
        grid_spec=pltpu.PrefetchScalarGridSpec(
            num_scalar_prefetch=3,
            grid=(n_sub // (tb * SUBROWS),),
            in_specs=[pl.BlockSpec(_to_subrows((tb, d)), lambda i, de, pe, nt: (i, 0))],
            out_specs=pl.BlockSpec(memory_space=pl.ANY),
            scratch_shapes=[pltpu.VMEM(_to_subrows((tm, d)), F32), pltpu.SemaphoreType.DMA((2,))]),
        out_shape=jax.ShapeDtypeStruct(_to_subrows((rows_total, d)), F32),
        compiler_params=_params(("arbitrary",)),
        name="moe_scatter",
    )(dest, pad_ends, n_tiles, h_sub)


def _moe_ffn_kernel(te_ref, tr_ref, nt_ref, xs_ref, wg_ref, wu_ref, wd_ref, o_ref, xb16, acc_ref, *, nf):
    i = pl.program_id(0)
    j = pl.program_id(1)
    tm = xb16.shape[0]
    half = tm // 2

    @pl.when(i < nt_ref[0])
    def _():
        @pl.when(j == 0)
        def _():
            for c in range(SUBROWS):
                xb16[:, c * LANES:(c + 1) * LANES] = xs_ref[pl.ds(c, tm, stride=SUBROWS), :].astype(BF16)
            acc_ref[...] = jnp.zeros(acc_ref.shape, F32)

        @pl.when(tr_ref[i] > half)
        def _():
            acc_ref[...] += _swiglu_step(xb16[...], wg_ref.at[0], wu_ref.at[0], wd_ref.at[0])

        @pl.when(tr_ref[i] <= half)
        def _():
            acc_ref[:half, :] += _swiglu_step(xb16[:half, :], wg_ref.at[0], wu_ref.at[0], wd_ref.at[0])

        @pl.when(j == nf - 1)
        def _():
            for c in range(SUBROWS):
                o_ref[pl.ds(c, tm, stride=SUBROWS), :] = acc_ref[:, c * LANES:(c + 1) * LANES]

    @pl.when(jnp.logical_and(i >= nt_ref[0], j == nf - 1))
    def _():
        o_ref[...] = jnp.zeros(o_ref.shape, o_ref.dtype)


def _moe_ffn(xs, wg, wu, wd, li, tile_e, tile_rows, n_tiles, tm):
    d = wg.shape[2]
    f = wg.shape[3]
    rows_total = xs.shape[0] // SUBROWS
    tf = TF_FFN
    nf = f // tf

    def row_map(i, j, te, tr, nt):
        return (jnp.minimum(i, nt[0] - 1), 0)

    def _fj(i, j, nt):
        return jnp.where(i < nt[0], j, nf - 1)

    def up_map(i, j, te, tr, nt):
        return (li, te[jnp.minimum(i, nt[0] - 1)], 0, _fj(i, j, nt))

    def down_map(i, j, te, tr, nt):
        return (li, te[jnp.minimum(i, nt[0] - 1)], _fj(i, j, nt), 0)

    return pl.pallas_call(
        functools.partial(_moe_ffn_kernel, nf=nf),
        grid_spec=pltpu.PrefetchScalarGridSpec(
            num_scalar_prefetch=3,
            grid=(rows_total // tm, nf),
            in_specs=[pl.BlockSpec(_to_subrows((tm, d)), row_map),
                      pl.BlockSpec((None, 1, d, tf), up_map),
                      pl.BlockSpec((None, 1, d, tf), up_map),
                      pl.BlockSpec((None, 1, tf, d), down_map)],
            out_specs=pl.BlockSpec(_to_subrows((tm, d)), lambda i, j, te, tr, nt: (i, 0)),
            scratch_shapes=[pltpu.VMEM((tm, d), BF16), pltpu.VMEM((tm, d), F32)]),
        out_shape=jax.ShapeDtypeStruct(_to_subrows((rows_total, d)), F32),
        compiler_params=_params(("arbitrary", "arbitrary")),
        name="moe_ffn",
    )(tile_e, tile_rows, n_tiles, xs, wg, wu, wd)


def _combine_kernel(pos_ref, ys_hbm, wgt_ref, x_ref, mod_ref, fg_ref, o_ref, buf, sem, *, tm, final):
    i = pl.program_id(0)
    slot = i % 2

    def gather(step, to_slot):
        _row_gather([lambda r: pos_ref[2 * (step * tm + r)], lambda r: pos_ref[2 * (step * tm + r) + 1]],
                    ys_hbm, [buf.at[to_slot, 0], buf.at[to_slot, 1]], sem.at[to_slot], tm)

    @pl.when(i == 0)
    def _():
        gather(0, 0)

    @pl.when(i + 1 < pl.num_programs(0))
    def _():
        gather(i + 1, 1 - slot)

    for k in range(2):
        _row_gather_wait(ys_hbm, buf.at[slot, k], sem.at[slot], tm)

    wgt = wgt_ref[...]
    w0, w1 = wgt[:, 0:1], wgt[:, 1:2]
    y = jnp.concatenate([w0 * buf[slot, 0, pl.ds(c, tm, stride=SUBROWS), :]
                         + w1 * buf[slot, 1, pl.ds(c, tm, stride=SUBROWS), :] for c in range(SUBROWS)], axis=1)
    out = x_ref[...] + mod_ref[0, 5:6, :] * y
    if final:
        out = out * lax.rsqrt(jnp.mean(out * out, axis=-1, keepdims=True) + EPS) * fg_ref[...]
    o_ref[...] = out


def _moe_combine(ys, pos, wgt, x1, mod, final_g, seq, final):
    n, d = x1.shape
    tm = TM_COMB
    per_b = seq // tm
    return pl.pallas_call(
        functools.partial(_combine_kernel, tm=tm, final=final),
        grid_spec=pltpu.PrefetchScalarGridSpec(
            num_scalar_prefetch=1,
            grid=(n // tm,),
            in_specs=[pl.BlockSpec(memory_space=pl.ANY),
                      pl.BlockSpec((tm, LANES), lambda i, p: (i, 0)),
                      pl.BlockSpec((tm, d), lambda i, p: (i, 0)),
                      pl.BlockSpec((1, N_MOD, d), lambda i, p: (i // per_b, 0, 0)),
                      pl.BlockSpec((1, d), lambda i, p: (0, 0))],
            out_specs=pl.BlockSpec((tm, d), lambda i, p: (i, 0)),
            scratch_shapes=[pltpu.VMEM((2, 2) + _to_subrows((tm, d)), F32),
                            pltpu.SemaphoreType.DMA((2,))]),
        out_shape=jax.ShapeDtypeStruct((n, d), F32),
        compiler_params=_params(("arbitrary",)),
        name="moe_combine_final" if final else "moe_combine",
    )(pos.reshape(-1), ys, wgt, x1, mod, final_g)


def _final_norm_kernel(x_ref, g_ref, o_ref):
    x = x_ref[...]
    o_ref[...] = x * lax.rsqrt(jnp.mean(x * x, axis=-1, keepdims=True) + EPS) * g_ref[...]


def _final_norm(x2, g):
    n, d = x2.shape
    tm = TM_PROJ
    return pl.pallas_call(
        _final_norm_kernel,
        grid=(n // tm,),
        in_specs=[pl.BlockSpec((tm, d), lambda i: (i, 0)), pl.BlockSpec((1, d), lambda i: (0, 0))],
        out_specs=pl.BlockSpec((tm, d), lambda i: (i, 0)),
        out_shape=jax.ShapeDtypeStruct((n, d), F32),
        compiler_params=_params(("arbitrary",)),
        name="final_norm",
    )(x2, g)


def kernel(x, c, positions, ada_w, ada_b, norm_mix_g, norm_ffn_g, w_in, w_out, sink, lam_q1, lam_k1, lam_q2, lam_k2, subln_g, ffn_w_gate, ffn_w_up, ffn_w_down, router_w, moe_w_gate, moe_w_up, moe_w_down, final_g):
    batch, seq, d = x.shape
    depth = w_in.shape[0]
    n = batch * seq

    c_pad = jnp.zeros((8, d), F32).at[:batch].set(c)
    mod_all = _adaln(c_pad, ada_w, ada_b)
    cos_t, sin_t = _rope_tables(positions)
    w_in_p = _permute_in_proj(w_in).astype(BF16)
    w_vdt = _value_rows(w_in).astype(BF16)
    w_out_b = w_out.astype(BF16)
    ffn_w = [ffn_w_gate, ffn_w_up, ffn_w_down]
    moe_w = [moe_w_gate, moe_w_up, moe_w_down]
    final_g2 = final_g.reshape(1, d)

    x2 = x.reshape(n, d)
    for layer in range(depth):
        lambda_init = 0.8 - 0.6 * math.exp(-0.3 * layer)
        mod = mod_all[layer, :batch].reshape(batch, N_MOD, d)
        proj, vt = _inproj(x2, mod, norm_mix_g[layer].reshape(1, d), w_in_p, w_vdt, layer, cos_t, sin_t, seq)
        ya = _window_attention(proj, vt, sink[layer], batch, seq)
        lam_vecs = jnp.stack([lam_q1[layer], lam_k1[layer], lam_q2[layer], lam_k2[layer]])
        yd = _diff_attention(proj, vt, lam_vecs, subln_g[layer].reshape(1, LANES), batch, seq, lambda_init)
        i = layer // 2
        g_ffn = norm_ffn_g[layer].reshape(1, d)
        if layer % 2 == 0:
            x1, h = _outproj(ya, yd, w_out_b, layer, x2, mod, g_ffn, seq)
            x2 = _dense_ffn(h, *ffn_w, i, x1, mod, seq)
        else:
            rw = jnp.zeros((d, LANES), F32).at[:, :N_EXPERTS].set(router_w[i])
            rw_hi = rw.astype(BF16)
            rw_lo = (rw - rw_hi.astype(F32)).astype(BF16)
            x1, h, idx, wgt = _outproj(ya, yd, w_out_b, layer, x2, mod, g_ffn, seq, router_w=(rw_hi, rw_lo))
            dest, pad_ends, tile_e, tile_rows, n_tiles, rows_total = _moe_plan(idx[:, :2], TM_MOE)
            pos = dest.reshape(n, 2)
            xs = _moe_scatter(h, dest, pad_ends, n_tiles, rows_total, TM_MOE)
            ys = _moe_ffn(xs, *moe_w, i, tile_e, tile_rows, n_tiles, TM_MOE)
            final = layer == depth - 1
            x2 = _moe_combine(ys, pos, wgt, x1, mod, final_g2, seq, final)
    if depth % 2 == 1:
        x2 = _final_norm(x2, final_g2)
    return x2.reshape(batch, seq, d)
```

```python
import functools
import math

import jax
import jax.numpy as jnp
from jax import lax
from jax.experimental import pallas as pl
from jax.experimental.pallas import tpu as pltpu

F32 = jnp.float32
BF16 = jnp.bfloat16

HEAD_DIM = 64
HALF_DIM = HEAD_DIM // 2
ROPE_THETA = 10000.0
EPS = 1e-6
SWA_Q_HEADS = 8
SWA_KV_HEADS = 2
WINDOW = 128
DIFF_HEADS = 4
N_EXPERTS = 8
N_MOD = 6
LANES = 128
SUBROWS = 8
VMEM_LIMIT = 56 * 1024 * 1024

QA_BLK, KA_BLK, QD_BLK, KD_BLK = 0, 4, 6, 10
ROPE_BLKS = 14
PROJ_COLS = ROPE_BLKS * LANES
VA_COL0 = 640
VD_COL0 = 1792
VT_WIN_BLK = 4
Q_SCALE = HEAD_DIM ** -0.5
LOG2E = math.log2(math.e)

TN_ADALN = 1536
TM_PROJ = 1024
TQ_WIN = 256
TQ_DIFF = 512
TK_DIFF = 1024
TM_FFN = 1024
TF_FFN = 512
TM_MOE = 1024
TM_COMB = 512


def _permute_in_proj(w_in):
    depth, d, _ = w_in.shape
    qa, ka, qd, kd = 0, 512, 768, 1280

    def pairs(lo, n_pairs):
        t = w_in[:, :, lo:lo + n_pairs * LANES].reshape(depth, d, n_pairs, 2, 2, HALF_DIM)
        return t.transpose(0, 1, 2, 4, 3, 5).reshape(depth, d, n_pairs * LANES)

    def dup_halves(lo):
        t = w_in[:, :, lo:lo + SWA_KV_HEADS * HEAD_DIM].reshape(depth, d, SWA_KV_HEADS, 2, 1, HALF_DIM)
        return jnp.broadcast_to(t, (depth, d, SWA_KV_HEADS, 2, 2, HALF_DIM)).reshape(depth, d, SWA_KV_HEADS * LANES)

    return jnp.concatenate([pairs(qa, 4), dup_halves(ka), pairs(qd, DIFF_HEADS), pairs(kd, DIFF_HEADS)], axis=-1)


def _value_rows(w_in):
    depth, d, _ = w_in.shape
    vd = jnp.swapaxes(w_in[:, :, VD_COL0:], 1, 2)
    va = jnp.swapaxes(w_in[:, :, VA_COL0:VA_COL0 + SWA_KV_HEADS * HEAD_DIM], 1, 2)
    va = jnp.broadcast_to(va.reshape(depth, SWA_KV_HEADS, 1, HEAD_DIM, d), (depth, SWA_KV_HEADS, 2, HEAD_DIM, d))
    return jnp.concatenate([vd, va.reshape(depth, SWA_KV_HEADS * LANES, d)], axis=1)


def _first_of_pair_mask():
    lane = lax.broadcasted_iota(jnp.int32, (1, LANES), 1)
    return (lane // HALF_DIM) % 2 == 0


def _silu(g):
    return g / (1.0 + jnp.exp(-g))


def _params(sem):
    return pltpu.CompilerParams(dimension_semantics=sem, vmem_limit_bytes=VMEM_LIMIT)


def _adaln_kernel(c_ref, w_ref, b_ref, o_ref):
    a = _silu(c_ref[...])
    o_ref[0] = jnp.dot(a, w_ref[0], preferred_element_type=F32,
                       precision=lax.Precision.HIGHEST) + b_ref[0]


def _adaln(c_pad, ada_w, ada_b):
    depth, d, cols = ada_w.shape
    tn = TN_ADALN
    return pl.pallas_call(
        _adaln_kernel,
        grid=(depth, cols // tn),
        in_specs=[pl.BlockSpec((8, d), lambda l, j: (0, 0)),
                  pl.BlockSpec((1, d, tn), lambda l, j: (l, 0, j)),
                  pl.BlockSpec((1, 1, tn), lambda l, j: (l, 0, j))],
        out_specs=pl.BlockSpec((1, 8, tn), lambda l, j: (l, 0, j)),
        out_shape=jax.ShapeDtypeStruct((depth, 8, cols), F32),
        compiler_params=_params(("arbitrary", "arbitrary")),
        name="adaln",
    )(c_pad, ada_w, ada_b.reshape(depth, 1, cols))


def _rope_table_kernel(pos_ref, cos_ref, sin_ref):
    lane = lax.broadcasted_iota(jnp.int32, (1, LANES), 1)
    freq = (lane % HALF_DIM).astype(F32)
    inv = 1.0 / (ROPE_THETA ** (2.0 * freq / HEAD_DIM))
    ang = pos_ref[...].astype(F32) * inv
    sign = jnp.where(lane < HEAD_DIM, -1.0, 1.0)
    cos_ref[...] = jnp.cos(ang)
    sin_ref[...] = jnp.sin(ang) * sign


def _rope_tables(positions):
    n = positions.size
    tm = TM_PROJ
    pos = positions.reshape(n, 1)
    return pl.pallas_call(
        _rope_table_kernel,
        grid=(n // tm,),
        in_specs=[pl.BlockSpec((tm, 1), lambda i: (i, 0))],
        out_specs=[pl.BlockSpec((tm, LANES), lambda i: (i, 0))] * 2,
        out_shape=[jax.ShapeDtypeStruct((n, LANES), F32)] * 2,
        compiler_params=_params(("arbitrary",)),
        name="rope_tables",
    )(pos)


def _rms_mod(x, g, scale, shift):
    y = x * lax.rsqrt(jnp.mean(x * x, axis=-1, keepdims=True) + EPS) * g
    return y * (1.0 + scale) + shift


def _inproj_kernel(x_ref, mod_ref, g_ref, w_ref, wvt_ref, cos_ref, sin_ref, o_ref, vt_ref):
    h = _rms_mod(x_ref[...], g_ref[...], mod_ref[0, 1:2, :], mod_ref[0, 0:1, :]).astype(BF16)
    proj = jnp.dot(h, w_ref[...], preferred_element_type=F32)
    vt_ref[...] = lax.dot_general(wvt_ref[...], h, (((1,), (1,)), ((), ())),
                                  preferred_element_type=F32).astype(BF16)
    cos = cos_ref[...]
    sin = sin_ref[...]
    for blk in range(ROPE_BLKS):
        t = proj[:, blk * LANES:(blk + 1) * LANES]
        r = t * cos + pltpu.roll(t, HEAD_DIM, axis=1) * sin
        if blk < KA_BLK or QD_BLK <= blk < KD_BLK:
            r = r * (Q_SCALE * LOG2E)
        o_ref[:, blk * LANES:(blk + 1) * LANES] = r.astype(BF16)


def _inproj(x2, mod, g, w, wvt, layer, cos_t, sin_t, seq):
    n, d = x2.shape
    tm = TM_PROJ
    per_b = seq // tm
    vt_rows = wvt.shape[1]
    return pl.pallas_call(
        _inproj_kernel,
        grid=(n // tm,),
        in_specs=[pl.BlockSpec((tm, d), lambda i: (i, 0)),
                  pl.BlockSpec((1, N_MOD, d), lambda i: (i // per_b, 0, 0)),
                  pl.BlockSpec((1, d), lambda i: (0, 0)),
                  pl.BlockSpec((None, d, PROJ_COLS), lambda i: (layer, 0, 0)),
                  pl.BlockSpec((None, vt_rows, d), lambda i: (layer, 0, 0)),
                  pl.BlockSpec((tm, LANES), lambda i: (i, 0)),
                  pl.BlockSpec((tm, LANES), lambda i: (i, 0))],
        out_specs=[pl.BlockSpec((tm, PROJ_COLS), lambda i: (i, 0)),
                   pl.BlockSpec((vt_rows, tm), lambda i: (0, i))],
        out_shape=[jax.ShapeDtypeStruct((n, PROJ_COLS), BF16),
                   jax.ShapeDtypeStruct((vt_rows, n), BF16)],
        compiler_params=_params(("arbitrary",)),
        name="inproj",
    )(x2, mod, g, w, wvt, cos_t, sin_t)


def _win_kernel(sink_ref, q_ref, k_ref, vt_ref, o_ref, qt_sc, *, tq, seq):
    i = pl.program_id(1)
    heads = SWA_Q_HEADS // SWA_KV_HEADS
    width = tq + 2 * WINDOW
    start = pl.multiple_of(jnp.clip(i * tq - WINDOW, 0, seq - width), LANES)
    first = _first_of_pair_mask()
    kpos = start + lax.broadcasted_iota(jnp.int32, (width, tq), 0)
    qpos = i * tq + lax.broadcasted_iota(jnp.int32, (width, tq), 1)
    valid = jnp.abs(qpos - kpos) <= WINDOW
    col = lax.broadcasted_iota(jnp.int32, (1, heads * tq), 1)
    lane = lax.broadcasted_iota(jnp.int32, (1, LANES), 1)
    for g in range(SWA_KV_HEADS):
        for blk in range(heads // 2):
            qb = (heads // 2) * g + blk
            q = q_ref[:, qb * LANES:(qb + 1) * LANES].astype(F32)
            zero = jnp.zeros_like(q)
            qt_sc[g, :, (2 * blk) * tq:(2 * blk + 1) * tq] = jnp.where(first, q, zero).T.astype(BF16)
            qt_sc[g, :, (2 * blk + 1) * tq:(2 * blk + 2) * tq] = jnp.where(first, zero, q).T.astype(BF16)

        s = jnp.dot(k_ref[pl.ds(start, width), g * LANES:(g + 1) * LANES], qt_sc[g],
                    preferred_element_type=F32)
        s = jnp.concatenate([jnp.where(valid, s[:, c * tq:(c + 1) * tq], -1e30) for c in range(heads)], axis=1)
        sk = jnp.zeros((1, heads * tq), F32)
        for c in range(heads):
            sk = jnp.where(col // tq == c, sink_ref[heads * g + c] * LOG2E, sk)
        m = jnp.maximum(jnp.max(s, axis=0, keepdims=True), sk)
        p = jnp.exp2(s - m)
        den = jnp.sum(p, axis=0, keepdims=True) + jnp.exp2(sk - m)
        o = jnp.dot(vt_ref[g * LANES:(g + 1) * LANES, pl.ds(start, width)], p.astype(BF16),
                    preferred_element_type=F32) / den
        for blk in range(heads // 2):
            qb = (heads // 2) * g + blk
            oa = o[:, (2 * blk) * tq:(2 * blk + 1) * tq].T
            ob = o[:, (2 * blk + 1) * tq:(2 * blk + 2) * tq].T
            o_ref[:, qb * LANES:(qb + 1) * LANES] = jnp.where(lane < HEAD_DIM, oa, ob).astype(o_ref.dtype)


def _window_attention(proj, vt, sink, batch, seq):
    n = proj.shape[0]
    tq = TQ_WIN
    nq = seq // tq
    heads = SWA_Q_HEADS // SWA_KV_HEADS
    q_cols = SWA_Q_HEADS // 2 * LANES
    kv_cols = SWA_KV_HEADS * LANES
    assert QA_BLK == 0 and KA_BLK * LANES % kv_cols == 0 and VT_WIN_BLK * LANES % kv_cols == 0
    kern = functools.partial(_win_kernel, tq=tq, seq=seq)
    return pl.pallas_call(
        kern,
        grid=(batch, nq),
        in_specs=[pl.BlockSpec(memory_space=pltpu.SMEM),
                  pl.BlockSpec((tq, q_cols), lambda b, i: (b * nq + i, 0)),
                  pl.BlockSpec((seq, kv_cols), lambda b, i: (b, KA_BLK * LANES // kv_cols)),
                  pl.BlockSpec((kv_cols, seq), lambda b, i: (VT_WIN_BLK * LANES // kv_cols, b))],
        out_specs=pl.BlockSpec((tq, q_cols), lambda b, i: (b * nq + i, 0)),
        out_shape=jax.ShapeDtypeStruct((n, q_cols), BF16),
        scratch_shapes=[pltpu.VMEM((SWA_KV_HEADS, LANES, heads * tq), BF16)],
        compiler_params=_params(("arbitrary", "arbitrary")),
        name="window_attn",
    )(sink, proj, proj, vt)


def _diff_kernel(lam_ref, g_ref, q_ref, k_ref, vt_ref, o_ref, qt_sc, s_a, s_b, cm_a, cm_b, m_sc, l_sc, acc_sc,
                 *, tq, tk, seq, lambda_init):
    nq = seq // tq
    nk = seq // tk
    first = _first_of_pair_mask()
    bufs = ((s_a, cm_a), (s_b, cm_b))

    def load_queries(i):
        row = pl.multiple_of(i * tq, tq)
        q = q_ref[pl.ds(row, tq), :].astype(F32)
        zero = jnp.zeros_like(q)
        qt_sc[:, :tq] = jnp.where(first, q, zero).T.astype(BF16)
        qt_sc[:, tq:] = jnp.where(first, zero, q).T.astype(BF16)

    def reset():
        m_sc[...] = jnp.full(m_sc.shape, -1e30, F32)
        l_sc[...] = jnp.zeros(l_sc.shape, F32)
        acc_sc[...] = jnp.zeros(acc_sc.shape, F32)

    def scores(t, s_ref, cm_ref):
        s = jnp.dot(k_ref[t * tk:(t + 1) * tk, :], qt_sc[...], preferred_element_type=F32)
        s_ref[...] = s
        cm_ref[...] = jnp.max(s, axis=0, keepdims=True)

    def consume(t, s_ref, cm_ref):
        m_prev = m_sc[...]
        m_new = jnp.maximum(m_prev, cm_ref[...])
        alpha = jnp.exp2(m_prev - m_new)
        p = jnp.exp2(s_ref[...] - m_new)
        l_sc[...] = alpha * l_sc[...] + jnp.sum(p, axis=0, keepdims=True)
        vb = vt_ref[:, t * tk:(t + 1) * tk]
        acc_sc[...] = alpha * acc_sc[...] + jnp.dot(vb, p.astype(BF16), preferred_element_type=F32)
        m_sc[...] = m_new

    lam1 = jnp.sum(lam_ref[0:1, :] * lam_ref[1:2, :], axis=-1, keepdims=True)
    lam2 = jnp.sum(lam_ref[2:3, :] * lam_ref[3:4, :], axis=-1, keepdims=True)
    lam = jnp.exp(lam1) - jnp.exp(lam2) + lambda_init
    gain = g_ref[...] * (1.0 - lambda_init)

    def finish(i):
        o = acc_sc[...] / l_sc[...]
        o = o[:, :tq] - lam * o[:, tq:]
        y = o * lax.rsqrt(jnp.mean(o * o, axis=0, keepdims=True) + EPS)
        row = pl.multiple_of(i * tq, tq)
        o_ref[pl.ds(row, tq), :] = (y.T * gain).astype(o_ref.dtype)

    load_queries(0)
    scores(0, *bufs[0])

    def tile(i, carry):
        reset()
        for t in range(nk):
            if t + 1 < nk:
                scores(t + 1, *bufs[(t + 1) % 2])
            else:
                load_queries(jnp.minimum(i + 1, nq - 1))
                scores(0, *bufs[0])
            consume(t, *bufs[t % 2])
        finish(i)
        return carry

    lax.fori_loop(0, nq, tile, 0)


def _diff_attention(proj, vt, lam_vecs, subln_g, batch, seq, lambda_init):
    n = proj.shape[0]
    tq, tk = TQ_DIFF, TK_DIFF
    assert seq % (2 * tk) == 0 and seq % tq == 0, "key blocks alternate between two score buffers"
    kern = functools.partial(_diff_kernel, tq=tq, tk=tk, seq=seq, lambda_init=lambda_init)
    return pl.pallas_call(
        kern,
        grid=(batch, DIFF_HEADS),
        in_specs=[pl.BlockSpec((4, HEAD_DIM), lambda b, h: (0, 0)),
                  pl.BlockSpec((1, LANES), lambda b, h: (0, 0)),
                  pl.BlockSpec((seq, LANES), lambda b, h: (b, QD_BLK + h)),
                  pl.BlockSpec((seq, LANES), lambda b, h: (b, KD_BLK + h)),
                  pl.BlockSpec((LANES, seq), lambda b, h: (h, b))],
        out_specs=pl.BlockSpec((seq, LANES), lambda b, h: (b, h)),
        out_shape=jax.ShapeDtypeStruct((n, DIFF_HEADS * LANES), BF16),
        scratch_shapes=[pltpu.VMEM((LANES, 2 * tq), BF16),
                        pltpu.VMEM((tk, 2 * tq), F32), pltpu.VMEM((tk, 2 * tq), F32),
                        pltpu.VMEM((1, 2 * tq), F32), pltpu.VMEM((1, 2 * tq), F32),
                        pltpu.VMEM((1, 2 * tq), F32), pltpu.VMEM((1, 2 * tq), F32),
                        pltpu.VMEM((LANES, 2 * tq), F32)],
        compiler_params=_params(("arbitrary", "arbitrary")),
        name="diff_attn",
    )(lam_vecs, subln_g, proj, proj, vt)


def _route_top2(logits):
    lane = lax.broadcasted_iota(jnp.int32, logits.shape, 1)
    lg = jnp.where(lane < N_EXPERTS, logits, -jnp.inf)
    v1 = jnp.max(lg, axis=-1, keepdims=True)
    i1 = jnp.min(jnp.where(lg == v1, lane, LANES), axis=-1, keepdims=True)
    lg2 = jnp.where(lane == i1, -jnp.inf, lg)
    v2 = jnp.max(lg2, axis=-1, keepdims=True)
    i2 = jnp.min(jnp.where(lg2 == v2, lane, LANES), axis=-1, keepdims=True)
    e = jnp.exp(v2 - v1)
    w1 = 1.0 / (1.0 + e)
    w2 = e / (1.0 + e)
    idx = jnp.where(lane == 0, i1, jnp.where(lane == 1, i2, 0))
    wgt = jnp.where(lane == 0, w1, jnp.where(lane == 1, w2, 0.0))
    return idx, wgt


def _outproj_kernel(ya_ref, yd_ref, w_ref, x_ref, mod_ref, g_ref, *rest, moe):
    if moe:
        rwh_ref, rwl_ref, x1_ref, h_ref, idx_ref, wgt_ref = rest
    else:
        x1_ref, h_ref = rest
    half = ya_ref.shape[1]
    y = (jnp.dot(ya_ref[...], w_ref[:half, :], preferred_element_type=F32)
         + jnp.dot(yd_ref[...], w_ref[half:, :], preferred_element_type=F32))
    x1 = x_ref[...] + mod_ref[0, 2:3, :] * y
    x1_ref[...] = x1
    h = _rms_mod(x1, g_ref[...], mod_ref[0, 4:5, :], mod_ref[0, 3:4, :])
    if moe:
        for c in range(SUBROWS):
            h_ref[pl.ds(c, h.shape[0], stride=SUBROWS), :] = h[:, c * LANES:(c + 1) * LANES]
    else:
        h_ref[...] = h.astype(h_ref.dtype)
    if moe:
        h_hi = h.astype(BF16)
        h_lo = (h - h_hi.astype(F32)).astype(BF16)
        logits = (jnp.dot(h_hi, rwh_ref[...], preferred_element_type=F32)
                  + jnp.dot(h_hi, rwl_ref[...], preferred_element_type=F32)
                  + jnp.dot(h_lo, rwh_ref[...], preferred_element_type=F32))
        idx, wgt = _route_top2(logits)
        idx_ref[...] = idx
        wgt_ref[...] = wgt


def _outproj(ya, yd, w, layer, x2, mod, g, seq, router_w=None):
    n, d = x2.shape
    tm = TM_PROJ
    per_b = seq // tm
    moe = router_w is not None
    row = lambda i: (i, 0)
    in_specs = [pl.BlockSpec((tm, ya.shape[1]), row),
                pl.BlockSpec((tm, yd.shape[1]), row),
                pl.BlockSpec((None,) + w.shape[1:], lambda i: (layer, 0, 0)),
                pl.BlockSpec((tm, d), row),
                pl.BlockSpec((1, N_MOD, d), lambda i: (i // per_b, 0, 0)),
                pl.BlockSpec((1, d), lambda i: (0, 0))]
    out_specs = [pl.BlockSpec((tm, d), row), pl.BlockSpec(_to_subrows((tm, d)) if moe else (tm, d), row)]
    out_shape = [jax.ShapeDtypeStruct((n, d), F32),
                 jax.ShapeDtypeStruct(_to_subrows((n, d)), F32) if moe else jax.ShapeDtypeStruct((n, d), BF16)]
    args = [ya, yd, w, x2, mod, g]
    if moe:
        in_specs += [pl.BlockSpec((d, LANES), lambda i: (0, 0))] * 2
        out_specs += [pl.BlockSpec((tm, LANES), row)] * 2
        out_shape += [jax.ShapeDtypeStruct((n, LANES), jnp.int32), jax.ShapeDtypeStruct((n, LANES), F32)]
        args += list(router_w)
    return pl.pallas_call(
        functools.partial(_outproj_kernel, moe=moe),
        grid=(n // tm,),
        in_specs=in_specs, out_specs=out_specs, out_shape=out_shape,
        compiler_params=_params(("arbitrary",)),
        name="outproj_moe" if moe else "outproj",
    )(*args)


def _swiglu_step(h, wg_ref, wu_ref, wd_ref):
    g = jnp.dot(h, wg_ref[...].astype(BF16), preferred_element_type=F32)
    u = jnp.dot(h, wu_ref[...].astype(BF16), preferred_element_type=F32)
    return jnp.dot((_silu(g) * u).astype(BF16), wd_ref[...].astype(BF16), preferred_element_type=F32)


def _ffn_kernel(h_ref, wg_ref, wu_ref, wd_ref, x_ref, mod_ref, o_ref, acc_ref):
    j = pl.program_id(1)

    @pl.when(j == 0)
    def _():
        acc_ref[...] = jnp.zeros(acc_ref.shape, F32)

    acc_ref[...] += _swiglu_step(h_ref[...], wg_ref, wu_ref, wd_ref)

    @pl.when(j == pl.num_programs(1) - 1)
    def _():
        o_ref[...] = x_ref[...] + mod_ref[0, 5:6, :] * acc_ref[...]


def _dense_ffn(h, wg, wu, wd, li, x1, mod, seq):
    n, d = x1.shape
    f = wg.shape[2]
    tm, tf = TM_FFN, TF_FFN
    per_b = seq // tm
    return pl.pallas_call(
        _ffn_kernel,
        grid=(n // tm, f // tf),
        in_specs=[pl.BlockSpec((tm, d), lambda i, j: (i, 0)),
                  pl.BlockSpec((None, d, tf), lambda i, j: (li, 0, j)),
                  pl.BlockSpec((None, d, tf), lambda i, j: (li, 0, j)),
                  pl.BlockSpec((None, tf, d), lambda i, j: (li, j, 0)),
                  pl.BlockSpec((tm, d), lambda i, j: (i, 0)),
                  pl.BlockSpec((1, N_MOD, d), lambda i, j: (i // per_b, 0, 0))],
        out_specs=pl.BlockSpec((tm, d), lambda i, j: (i, 0)),
        out_shape=jax.ShapeDtypeStruct((n, d), F32),
        scratch_shapes=[pltpu.VMEM((tm, d), F32)],
        compiler_params=_params(("arbitrary", "arbitrary")),
        name="dense_ffn",
    )(h, wg, wu, wd, x1, mod)


def _moe_plan(idx, tm):
    n = idx.shape[0]
    flat_e = idx.reshape(-1)
    onehot = (flat_e[:, None] == jnp.arange(N_EXPERTS, dtype=jnp.int32)[None, :]).astype(jnp.int32)
    csum = jnp.cumsum(onehot, axis=0)
    rank = jnp.sum((csum - onehot) * onehot, axis=1)
    counts = csum[-1]
    padded = (counts + tm - 1) // tm * tm
    pad_ends = jnp.cumsum(padded)
    pad_starts = pad_ends - padded
    dest = (pad_starts[flat_e] + rank).astype(jnp.int32)
    rows = 2 * n + N_EXPERTS * tm
    n_tiles = (pad_ends[-1] // tm).astype(jnp.int32).reshape(1)
    tile_start = jnp.arange(rows // tm, dtype=jnp.int32) * tm
    tile_e = jnp.minimum(jnp.sum((pad_ends[None, :] <= tile_start[:, None]).astype(jnp.int32), axis=1),
                         N_EXPERTS - 1).astype(jnp.int32)
    tile_rows = jnp.clip((pad_starts + counts)[tile_e] - tile_start, 0, tm).astype(jnp.int32)
    return dest, pad_ends.astype(jnp.int32), tile_e, tile_rows, n_tiles, rows


def _to_subrows(ref_or_shape):
    rows, d = ref_or_shape
    assert d == SUBROWS * LANES
    return (rows * SUBROWS, LANES)


def _row_copy(src_hbm, src_row, dst_ref, dst_row, sem):
    s = pl.multiple_of(src_row * SUBROWS, SUBROWS)
    t = pl.multiple_of(dst_row * SUBROWS, SUBROWS)
    return pltpu.make_async_copy(src_hbm.at[pl.ds(s, SUBROWS)], dst_ref.at[pl.ds(t, SUBROWS)], sem)


def _row_gather(idx_fns, src_hbm, dst_refs, sem, rows):
    def issue(r, carry):
        for k, (idx_of, dst_ref) in enumerate(zip(idx_fns, dst_refs)):
            _row_copy(src_hbm, idx_of(r), dst_ref, r, sem).start(priority=k % 2)
        return carry

    lax.fori_loop(0, rows, issue, 0, unroll=4)


def _row_gather_wait(src_hbm, dst_ref, sem, rows):
    pltpu.make_async_copy(src_hbm.at[pl.ds(0, rows * SUBROWS)], dst_ref, sem).wait()


def _moe_scatter_kernel(dest_ref, pe_ref, nt_ref, h_ref, xs_hbm, zeros, sem, *, tb, tm, n_tiles_max):
    i = pl.program_id(0)
    zero_sem, row_sem = sem.at[0], sem.at[1]
    tile_rows = tm * SUBROWS

    def zero_tile(t):
        start = pl.multiple_of(t * tile_rows, tile_rows)
        return pltpu.make_async_copy(zeros, xs_hbm.at[pl.ds(start, tile_rows)], zero_sem)

    @pl.when(i == 0)
    def _():
        zeros[...] = jnp.zeros(zeros.shape, zeros.dtype)
        for e in range(N_EXPERTS):
            last_tile = jnp.maximum(pe_ref[e] // tm - 1, 0)
            zero_tile(last_tile).start()
            zero_tile(last_tile).wait()
        for t in range(n_tiles_max):
            @pl.when(t >= nt_ref[0])
            def _():
                zero_tile(t).start()
                zero_tile(t).wait()

    base = i * tb

    def issue(r, carry):
        src = h_ref.at[pl.ds(pl.multiple_of(r * SUBROWS, SUBROWS), SUBROWS)]
        for k in range(2):
            d = dest_ref[2 * (base + r) + k]
            pltpu.make_async_copy(src, xs_hbm.at[pl.ds(pl.multiple_of(d * SUBROWS, SUBROWS), SUBROWS)],
                                  row_sem).start(priority=k)
        return carry

    lax.fori_loop(0, tb, issue, 0, unroll=4)
    for k in range(2):
        pltpu.make_async_copy(h_ref, xs_hbm.at[pl.ds(0, tb * SUBROWS)], row_sem).wait()


def _moe_scatter(h_sub, dest, pad_ends, n_tiles, rows_total, tm):
    n_sub = h_sub.shape[0]
    tb = TM_COMB
    d = SUBROWS * LANES
    n_tiles_max = rows_total // tm
    return pl.pallas_call(
        functools.partial(_moe_scatter_kernel, tb=tb, tm=tm, n_tiles_max=n_tiles_max),
        grid_spec=pltpu.PrefetchScalarGridSpec(
            num_scalar_prefetch=3,
            grid=(n_sub // (tb * SUBROWS),),
            in_specs=[pl.BlockSpec(_to_subrows((tb, d)), lambda i, de, pe, nt: (i, 0))],
            out_specs=pl.BlockSpec(memory_space=pl.ANY),
            scratch_shapes=[pltpu.VMEM(_to_subrows((tm, d)), F32), pltpu.SemaphoreType.DMA((2,))]),
        out_shape=jax.ShapeDtypeStruct(_to_subrows((rows_total, d)), F32),
        compiler_params=_params(("arbitrary",)),
        name="moe_scatter",
    )(dest, pad_ends, n_tiles, h_sub)


def _moe_ffn_kernel(te_ref, tr_ref, nt_ref, xs_ref, wg_ref, wu_ref, wd_ref, o_ref, xb16, acc_ref, *, nf):
    i = pl.program_id(0)
    j = pl.program_id(1)
    tm = xb16.shape[0]
    half = tm // 2

    @pl.when(i < nt_ref[0])
    def _():
        @pl.when(j == 0)
        def _():
            for c in range(SUBROWS):
                xb16[:, c * LANES:(c + 1) * LANES] = xs_ref[pl.ds(c, tm, stride=SUBROWS), :].astype(BF16)
            acc_ref[...] = jnp.zeros(acc_ref.shape, F32)

        @pl.when(tr_ref[i] > half)
        def _():
            acc_ref[...] += _swiglu_step(xb16[...], wg_ref.at[0], wu_ref.at[0], wd_ref.at[0])

        @pl.when(tr_ref[i] <= half)
        def _():
            acc_ref[:half, :] += _swiglu_step(xb16[:half, :], wg_ref.at[0], wu_ref.at[0], wd_ref.at[0])

        @pl.when(j == nf - 1)
        def _():
            for c in range(SUBROWS):
                o_ref[pl.ds(c, tm, stride=SUBROWS), :] = acc_ref[:, c * LANES:(c + 1) * LANES]

    @pl.when(jnp.logical_and(i >= nt_ref[0], j == nf - 1))
    def _():
        o_ref[...] = jnp.zeros(o_ref.shape, o_ref.dtype)


def _moe_ffn(xs, wg, wu, wd, li, tile_e, tile_rows, n_tiles, tm):
    d = wg.shape[2]
    f = wg.shape[3]
    rows_total = xs.shape[0] // SUBROWS
    tf = TF_FFN
    nf = f // tf

    def row_map(i, j, te, tr, nt):
        return (jnp.minimum(i, nt[0] - 1), 0)

    def _fj(i, j, nt):
        return jnp.where(i < nt[0], j, nf - 1)

    def up_map(i, j, te, tr, nt):
        return (li, te[jnp.minimum(i, nt[0] - 1)], 0, _fj(i, j, nt))

    def down_map(i, j, te, tr, nt):
        return (li, te[jnp.minimum(i, nt[0] - 1)], _fj(i, j, nt), 0)

    return pl.pallas_call(
        functools.partial(_moe_ffn_kernel, nf=nf),
        grid_spec=pltpu.PrefetchScalarGridSpec(
            num_scalar_prefetch=3,
            grid=(rows_total // tm, nf),
            in_specs=[pl.BlockSpec(_to_subrows((tm, d)), row_map),
                      pl.BlockSpec((None, 1, d, tf), up_map),
                      pl.BlockSpec((None, 1, d, tf), up_map),
                      pl.BlockSpec((None, 1, tf, d), down_map)],
            out_specs=pl.BlockSpec(_to_subrows((tm, d)), lambda i, j, te, tr, nt: (i, 0)),
            scratch_shapes=[pltpu.VMEM((tm, d), BF16), pltpu.VMEM((tm, d), F32)]),
        out_shape=jax.ShapeDtypeStruct(_to_subrows((rows_total, d)), F32),
        compiler_params=_params(("arbitrary", "arbitrary")),
        name="moe_ffn",
    )(tile_e, tile_rows, n_tiles, xs, wg, wu, wd)


def _combine_kernel(pos_ref, ys_hbm, wgt_ref, x_ref, mod_ref, fg_ref, o_ref, buf, sem, *, tm, final):
    i = pl.program_id(0)
    slot = i % 2

    def gather(step, to_slot):
        _row_gather([lambda r: pos_ref[2 * (step * tm + r)], lambda r: pos_ref[2 * (step * tm + r) + 1]],
                    ys_hbm, [buf.at[to_slot, 0], buf.at[to_slot, 1]], sem.at[to_slot], tm)

    @pl.when(i == 0)
    def _():
        gather(0, 0)

    @pl.when(i + 1 < pl.num_programs(0))
    def _():
        gather(i + 1, 1 - slot)

    for k in range(2):
        _row_gather_wait(ys_hbm, buf.at[slot, k], sem.at[slot], tm)

    wgt = wgt_ref[...]
    w0, w1 = wgt[:, 0:1], wgt[:, 1:2]
    y = jnp.concatenate([w0 * buf[slot, 0, pl.ds(c, tm, stride=SUBROWS), :]
                         + w1 * buf[slot, 1, pl.ds(c, tm, stride=SUBROWS), :] for c in range(SUBROWS)], axis=1)
    out = x_ref[...] + mod_ref[0, 5:6, :] * y
    if final:
        out = out * lax.rsqrt(jnp.mean(out * out, axis=-1, keepdims=True) + EPS) * fg_ref[...]
    o_ref[...] = out


def _moe_combine(ys, pos, wgt, x1, mod, final_g, seq, final):
    n, d = x1.shape
    tm = TM_COMB
    per_b = seq // tm
    return pl.pallas_call(
        functools.partial(_combine_kernel, tm=tm, final=final),
        grid_spec=pltpu.PrefetchScalarGridSpec(
            num_scalar_prefetch=1,
            grid=(n // tm,),
            in_specs=[pl.BlockSpec(memory_space=pl.ANY),
                      pl.BlockSpec((tm, LANES), lambda i, p: (i, 0)),
                      pl.BlockSpec((tm, d), lambda i, p: (i, 0)),
                      pl.BlockSpec((1, N_MOD, d), lambda i, p: (i // per_b, 0, 0)),
                      pl.BlockSpec((1, d), lambda i, p: (0, 0))],
            out_specs=pl.BlockSpec((tm, d), lambda i, p: (i, 0)),
            scratch_shapes=[pltpu.VMEM((2, 2) + _to_subrows((tm, d)), F32),
                            pltpu.SemaphoreType.DMA((2,))]),
        out_shape=jax.ShapeDtypeStruct((n, d), F32),
        compiler_params=_params(("arbitrary",)),
        name="moe_combine_final" if final else "moe_combine",
    )(pos.reshape(-1), ys, wgt, x1, mod, final_g)


def _final_norm_kernel(x_ref, g_ref, o_ref):
    x = x_ref[...]
    o_ref[...] = x * lax.rsqrt(jnp.mean(x * x, axis=-1, keepdims=True) + EPS) * g_ref[...]


def _final_norm(x2, g):
    n, d = x2.shape
    tm = TM_PROJ
    return pl.pallas_call(
        _final_norm_kernel,
        grid=(n // tm,),
        in_specs=[pl.BlockSpec((tm, d), lambda i: (i, 0)), pl.BlockSpec((1, d), lambda i: (0, 0))],
        out_specs=pl.BlockSpec((tm, d), lambda i: (i, 0)),
        out_shape=jax.ShapeDtypeStruct((n, d), F32),
        compiler_params=_params(("arbitrary",)),
        name="final_norm",
    )(x2, g)


def kernel(x, c, positions, ada_w, ada_b, norm_mix_g, norm_ffn_g, w_in, w_out, sink, lam_q1, lam_k1, lam_q2, lam_k2, subln_g, ffn_w_gate, ffn_w_up, ffn_w_down, router_w, moe_w_gate, moe_w_up, moe_w_down, final_g):
    batch, seq, d = x.shape
    depth = w_in.shape[0]
    n = batch * seq

    c_pad = jnp.zeros((8, d), F32).at[:batch].set(c)
    mod_all = _adaln(c_pad, ada_w, ada_b)
    cos_t, sin_t = _rope_tables(positions)
    w_in_p = _permute_in_proj(w_in).astype(BF16)
    w_vdt = _value_rows(w_in).astype(BF16)
    w_out_b = w_out.astype(BF16)
    ffn_w = [ffn_w_gate, ffn_w_up, ffn_w_down]
    moe_w = [moe_w_gate, moe_w_up, moe_w_down]
    final_g2 = final_g.reshape(1, d)

    x2 = x.reshape(n, d)
    for layer in range(depth):
        lambda_init = 0.8 - 0.6 * math.exp(-0.3 * layer)
        mod = mod_all[layer, :batch].reshape(batch, N_MOD, d)
        proj, vt = _inproj(x2, mod, norm_mix_g[layer].reshape(1, d), w_in_p, w_vdt, layer, cos_t, sin_t, seq)
        ya = _window_attention(proj, vt, sink[layer], batch, seq)
        lam_vecs = jnp.stack([lam_q1[layer], lam_k1[layer], lam_q2[layer], lam_k2[layer]])
        yd = _diff_attention(proj, vt, lam_vecs, subln_g[layer].reshape(1, LANES), batch, seq, lambda_init)
        i = layer // 2
        g_ffn = norm_ffn_g[layer].reshape(1, d)
        if layer % 2 == 0:
            x1, h = _outproj(ya, yd, w_out_b, layer, x2, mod, g_ffn, seq)
            x2 = _dense_ffn(h, *ffn_w, i, x1, mod, seq)
        else:
            rw = jnp.zeros((d, LANES), F32).at[:, :N_EXPERTS].set(router_w[i])
            rw_hi = rw.astype(BF16)
            rw_lo = (rw - rw_hi.astype(F32)).astype(BF16)
            x1, h, idx, wgt = _outproj(ya, yd, w_out_b, layer, x2, mod, g_ffn, seq, router_w=(rw_hi, rw_lo))
            dest, pad_ends, tile_e, tile_rows, n_tiles, rows_total = _moe_plan(idx[:, :2], TM_MOE)
            pos = dest.reshape(n, 2)
            xs = _moe_scatter(h, dest, pad_ends, n_tiles, rows_total, TM_MOE)
            ys = _moe_ffn(xs, *moe_w, i, tile_e, tile_rows, n_tiles, TM_MOE)
            final = layer == depth - 1
            x2 = _moe_combine(ys, pos, wgt, x1, mod, final_g2, seq, final)
    if depth % 2 == 1:
        x2 = _final_norm(x2, final_g2)
    return x2.reshape(batch, seq, d)
```
